```python
import math
import jax, jax.numpy as jnp
from jax import lax
import numpy as np

D_MODEL = 1024
BATCH = 8
SEQ = 2048
DEPTH = 1
DEC_BATCH = 16
DEC_SEQ = 2048
PAST_LEN = 128

HEAD_DIM = 64
DIFF_HEADS = 4
WIN_HEADS = 8
WIN_KV_HEADS = 2
WIN_GROUP = WIN_HEADS // WIN_KV_HEADS
WINDOW = 128
BLOCK = 128
N_BUCKETS = 32
MAX_DISTANCE = 128
N_BIAS_HEADS = DIFF_HEADS + WIN_HEADS
D_FF = 2816
CONV_WIDTH = 3
EPS = 1e-6
NEG_INF = -1e30

DIFF_QK_W = DIFF_HEADS * 2 * HEAD_DIM
DIFF_V_W = DIFF_HEADS * 2 * HEAD_DIM
WIN_Q_W = WIN_HEADS * HEAD_DIM
WIN_KV_W = WIN_KV_HEADS * HEAD_DIM
D_IN = 2 * DIFF_QK_W + DIFF_V_W + WIN_Q_W + 2 * WIN_KV_W
D_MIX = DIFF_V_W + WIN_Q_W
SPLIT_POINTS = (DIFF_QK_W, 2 * DIFF_QK_W, 2 * DIFF_QK_W + DIFF_V_W,
                2 * DIFF_QK_W + DIFF_V_W + WIN_Q_W,
                2 * DIFF_QK_W + DIFF_V_W + WIN_Q_W + WIN_KV_W)

kernel_name = 'hymba_diff_window_encoder'


def rms_norm(x, g):
    xf = x.astype(jnp.float32)
    y = xf * lax.rsqrt(jnp.mean(xf * xf, axis=-1, keepdims=True) + EPS)
    return (y * g.astype(jnp.float32)).astype(x.dtype)


def rel_bucket(rel):
    nb = N_BUCKETS // 2
    max_exact = nb // 2
    ret = jnp.where(rel > 0, nb, 0)
    n = jnp.abs(rel)
    nf = jnp.maximum(n, 1).astype(jnp.float32)
    large = max_exact + (jnp.log(nf / max_exact) / math.log(MAX_DISTANCE / max_exact)
                         * (nb - max_exact)).astype(jnp.int32)
    large = jnp.minimum(large, nb - 1)
    return ret + jnp.where(n < max_exact, n, large)


def diff_attention(q, k, v, bias_table, lam):
    B, S = q.shape[0], q.shape[1]
    nb = S // BLOCK
    scale = HEAD_DIM ** -0.5
    table = bias_table[:, :DIFF_HEADS]
    kpos = jnp.arange(S)
    qb = q.reshape(B, nb, BLOCK, DIFF_HEADS, 2, HEAD_DIM).transpose(1, 0, 2, 3, 4, 5)
    lam32 = lam.astype(jnp.float32)

    def one_block(args):
        n, qblk = args
        qpos = n * BLOCK + jnp.arange(BLOCK)
        bias = table[rel_bucket(kpos[None, :] - qpos[:, None])]
        s = jnp.einsum('bqhmd,bkhmd->bhmqk', qblk, k).astype(jnp.float32) * scale
        s = s + bias.transpose(2, 0, 1)[None, :, None].astype(jnp.float32)
        p = jax.nn.softmax(s, axis=-1)
        a = p[:, :, 0] - lam32 * p[:, :, 1]
        return jnp.einsum('bhqk,bkhe->bqhe', a.astype(v.dtype), v)

    out = lax.map(one_block, (jnp.arange(nb), qb))
    return out.transpose(1, 0, 2, 3, 4).reshape(B, S, DIFF_HEADS, 2 * HEAD_DIM)


def window_attention(q, k, v, bias_table, sink):
    B, S = q.shape[0], q.shape[1]
    nb = S // BLOCK
    scale = HEAD_DIM ** -0.5
    qb = q.reshape(B, nb, BLOCK, WIN_KV_HEADS, WIN_GROUP, HEAD_DIM)

    def band(t):
        tp = jnp.pad(t, ((0, 0), (BLOCK, BLOCK), (0, 0), (0, 0)))
        tp = tp.reshape(B, nb + 2, BLOCK, WIN_KV_HEADS, HEAD_DIM)
        return jnp.concatenate([tp[:, :-2], tp[:, 1:-1], tp[:, 2:]], axis=2)

    kb, vb = band(k), band(v)
    qi = jnp.arange(BLOCK)
    kj = jnp.arange(3 * BLOCK) - BLOCK
    rel = kj[None, :] - qi[:, None]
    bias = bias_table[rel_bucket(rel)][..., DIFF_HEADS:]
    bias = bias.transpose(2, 0, 1).reshape(WIN_KV_HEADS, WIN_GROUP, BLOCK, 3 * BLOCK)
    kpos = jnp.arange(nb)[:, None] * BLOCK + kj[None, :]
    valid = ((jnp.abs(rel) <= WINDOW)[None]
             & (kpos >= 0)[:, None, :] & (kpos < S)[:, None, :])
    s = jnp.einsum('bnqhgd,bnkhd->bnhgqk', qb, kb).astype(jnp.float32) * scale
    s = s + bias[None, None].astype(jnp.float32)
    s = jnp.where(valid[None, :, None, None], s, NEG_INF)
    sink_col = jnp.broadcast_to(
        sink.astype(jnp.float32).reshape(1, 1, WIN_KV_HEADS, WIN_GROUP, 1, 1),
        s.shape[:-1] + (1,))
    p = jax.nn.softmax(jnp.concatenate([s, sink_col], axis=-1), axis=-1)[..., :-1]
    out = jnp.einsum('bnhgqk,bnkhd->bnqhgd', p.astype(v.dtype), vb)
    return out.reshape(B, S, WIN_HEADS * HEAD_DIM)


def dwconv_centered(u, w, b):
    S = u.shape[1]
    pad = CONV_WIDTH // 2
    up = jnp.pad(u, ((0, 0), (pad, pad), (0, 0)))
    return sum(up[:, i:i + S] * w[i] for i in range(CONV_WIDTH)) + b


def encoder_layer(x, layer_idx, norm_attn_g, w_in, diff_q_norm_g, diff_k_norm_g,
                  diff_lambda_q1, diff_lambda_k1, diff_lambda_q2, diff_lambda_k2,
                  diff_subln_g, win_q_norm_g, win_k_norm_g, win_sink, rel_bias,
                  w_out, norm_ffn_g, w_gate, w_up, conv_w, conv_b, w_down):
    B, S = x.shape[0], x.shape[1]
    h = rms_norm(x, norm_attn_g)
    proj = h @ w_in
    dq, dk, dv, wq, wk, wv = jnp.split(proj, SPLIT_POINTS, axis=-1)

    dq = rms_norm(dq.reshape(B, S, DIFF_HEADS, 2, HEAD_DIM), diff_q_norm_g)
    dk = rms_norm(dk.reshape(B, S, DIFF_HEADS, 2, HEAD_DIM), diff_k_norm_g)
    dv = dv.reshape(B, S, DIFF_HEADS, 2 * HEAD_DIM)
    lam_init = 0.8 - 0.6 * math.exp(-0.3 * layer_idx)
    lam = (jnp.exp(jnp.sum(diff_lambda_q1.astype(jnp.float32) * diff_lambda_k1.astype(jnp.float32)))
           - jnp.exp(jnp.sum(diff_lambda_q2.astype(jnp.float32) * diff_lambda_k2.astype(jnp.float32)))
           + lam_init)
    o_a = diff_attention(dq, dk, dv, rel_bias, lam)
    o_a = (rms_norm(o_a, diff_subln_g) * (1.0 - lam_init)).reshape(B, S, DIFF_V_W)

    wq = rms_norm(wq.reshape(B, S, WIN_HEADS, HEAD_DIM), win_q_norm_g)
    wk = rms_norm(wk.reshape(B, S, WIN_KV_HEADS, HEAD_DIM), win_k_norm_g)
    wv = wv.reshape(B, S, WIN_KV_HEADS, HEAD_DIM)
    o_b = window_attention(wq, wk, wv, rel_bias, win_sink)

    x = x + jnp.concatenate([o_a, o_b], axis=-1) @ w_out

    h = rms_norm(x, norm_ffn_g)
    u = dwconv_centered(h @ w_gate, conv_w, conv_b)
    x = x + (jax.nn.silu(u) * (h @ w_up)) @ w_down
    return x


def setup_inputs(seed: int = 0) -> dict:
    key = jax.random.key(seed)
    ks = jax.random.split(key, 22)
    f32 = jnp.float32

    def nrm(k, shape, scale):
        return jax.random.normal(k, shape, f32) * scale

    def gain(k, shape):
        return 1.0 + 0.05 * jax.random.normal(k, shape, f32)

    return {
        'x_prompt': nrm(ks[0], (BATCH, SEQ, D_MODEL), 1.0),
        'x_sample': nrm(ks[1], (DEC_BATCH, DEC_SEQ, D_MODEL), 1.0),
        'norm_attn_g': gain(ks[2], (DEPTH, D_MODEL)),
        'w_in': nrm(ks[3], (DEPTH, D_MODEL, D_IN), D_MODEL ** -0.5),
        'diff_q_norm_g': gain(ks[4], (DEPTH, HEAD_DIM)),
        'diff_k_norm_g': gain(ks[5], (DEPTH, HEAD_DIM)),
        'diff_lambda_q1': nrm(ks[6], (DEPTH, HEAD_DIM), 0.1),
        'diff_lambda_k1': nrm(ks[7], (DEPTH, HEAD_DIM), 0.1),
        'diff_lambda_q2': nrm(ks[8], (DEPTH, HEAD_DIM), 0.1),
        'diff_lambda_k2': nrm(ks[9], (DEPTH, HEAD_DIM), 0.1),
        'diff_subln_g': gain(ks[10], (DEPTH, 2 * HEAD_DIM)),
        'win_q_norm_g': gain(ks[11], (DEPTH, HEAD_DIM)),
        'win_k_norm_g': gain(ks[12], (DEPTH, HEAD_DIM)),
        'win_sink': nrm(ks[13], (DEPTH, WIN_HEADS), 0.5),
        'rel_bias': nrm(ks[14], (N_BUCKETS, N_BIAS_HEADS), 0.5),
        'w_out': nrm(ks[15], (DEPTH, D_MIX, D_MODEL), D_MIX ** -0.5),
        'norm_ffn_g': gain(ks[16], (DEPTH, D_MODEL)),
        'w_gate': nrm(ks[17], (DEPTH, D_MODEL, D_FF), D_MODEL ** -0.5),
        'w_up': nrm(ks[18], (DEPTH, D_MODEL, D_FF), D_MODEL ** -0.5),
        'conv_w': nrm(ks[19], (DEPTH, CONV_WIDTH, D_FF), CONV_WIDTH ** -0.5),
        'conv_b': nrm(ks[20], (DEPTH, D_FF), 0.02),
        'w_down': nrm(ks[21], (DEPTH, D_FF, D_MODEL), D_FF ** -0.5),
    }


def reference(x_prompt, x_sample, norm_attn_g, w_in, diff_q_norm_g, diff_k_norm_g,
              diff_lambda_q1, diff_lambda_k1, diff_lambda_q2, diff_lambda_k2,
              diff_subln_g, win_q_norm_g, win_k_norm_g, win_sink, rel_bias,
              w_out, norm_ffn_g, w_gate, w_up, conv_w, conv_b, w_down):
    def run(x):
        for l in range(DEPTH):
            x = encoder_layer(
                x, l, norm_attn_g[l], w_in[l], diff_q_norm_g[l], diff_k_norm_g[l],
                diff_lambda_q1[l], diff_lambda_k1[l], diff_lambda_q2[l], diff_lambda_k2[l],
                diff_subln_g[l], win_q_norm_g[l], win_k_norm_g[l], win_sink[l], rel_bias,
                w_out[l], norm_ffn_g[l], w_gate[l], w_up[l], conv_w[l], conv_b[l], w_down[l])
        return x

    y_prompt = run(x_prompt)
    y_sample = run(x_sample)
    return (y_prompt, y_sample)
```

```python
import functools
import math

import jax
import jax.numpy as jnp
import numpy as np
from jax import lax
from jax.experimental import pallas as pl
from jax.experimental.pallas import tpu as pltpu

D_MODEL = 1024
SEQ = 2048
HEAD_DIM = 64
DIFF_HEADS = 4
WIN_HEADS = 8
WIN_KV_HEADS = 2
WIN_GROUP = WIN_HEADS // WIN_KV_HEADS
WINDOW = 128
BLOCK = 128
N_BUCKETS = 32
MAX_DISTANCE = 128
D_FF = 2816
EPS = 1e-6
NEG_INF = -1e30

DIFF_W = DIFF_HEADS * 2 * HEAD_DIM
WIN_Q_W = WIN_HEADS * HEAD_DIM
WIN_KV_W = WIN_KV_HEADS * HEAD_DIM
D_IN = 3 * DIFF_W + WIN_Q_W + 2 * WIN_KV_W
NORMED_W = 2 * DIFF_W + WIN_Q_W + WIN_KV_W

LANES = 128
MXU_DIM = 256
VMEM_LIMIT = 56 * 1024 * 1024

TM_PROJ = 512
TQ = 256
TK = 256
TM_FFN = 512
FF_CHUNK = 256
HALO = 16

BF16 = jnp.bfloat16
F32 = jnp.float32


def _rel_bucket(rel):
    nb = N_BUCKETS // 2
    max_exact = nb // 2
    ret = jnp.where(rel > 0, nb, 0)
    n = jnp.abs(rel)
    nf = jnp.maximum(n, 1).astype(F32)
    large = max_exact + (jnp.log(nf / max_exact) / math.log(MAX_DISTANCE / max_exact)
                         * (nb - max_exact)).astype(jnp.int32)
    large = jnp.minimum(large, nb - 1)
    return ret + jnp.where(n < max_exact, n, large)


def _compiler_params(semantics):
    return pltpu.CompilerParams(dimension_semantics=semantics, vmem_limit_bytes=VMEM_LIMIT)


def _resident(shape):
    nd = len(shape)
    return pl.BlockSpec(shape, lambda *_: (0,) * nd, pipeline_mode=pl.Buffered(1))


def _in_proj_kernel(x_ref, g_ref, w_ref, gv_ref, bd_ref,
                    dq_ref, dk_ref, wq_ref, wk_ref, dv_ref, wv_ref):
    x = x_ref[...]
    ms = jnp.mean(x * x, axis=-1, keepdims=True)
    h = (x * lax.rsqrt(ms + EPS) * g_ref[...]).astype(BF16)
    proj = jnp.dot(h, w_ref[...], preferred_element_type=F32)
    bd = bd_ref[...]

    def head_norm(c0, width):
        blk = proj[:, c0:c0 + width]
        sq = (blk * blk).astype(BF16)
        ss = jnp.dot(sq, bd[:width, :width], preferred_element_type=F32)
        return (blk * lax.rsqrt(ss * (1.0 / HEAD_DIM) + EPS) * gv_ref[:, c0:c0 + width]).astype(BF16)

    for ref, base in ((dq_ref, 0), (dk_ref, DIFF_W), (wq_ref, 2 * DIFF_W)):
        for c in range(0, DIFF_W, MXU_DIM):
            ref[:, c:c + MXU_DIM] = head_norm(base + c, MXU_DIM)
    wk_ref[...] = head_norm(2 * DIFF_W + WIN_Q_W, WIN_KV_W)
    dv_ref[...] = proj[:, NORMED_W:NORMED_W + DIFF_W].astype(BF16)
    wv_ref[...] = proj[:, NORMED_W + DIFF_W:].astype(BF16)


def _in_proj(x2d, g_attn, w_in_p, gvec, bd):
    t = x2d.shape[0]
    tile = lambda w: pl.BlockSpec((TM_PROJ, w), lambda i: (i, 0))
    widths = (DIFF_W, DIFF_W, WIN_Q_W, WIN_KV_W, DIFF_W, WIN_KV_W)
    return pl.pallas_call(
        _in_proj_kernel,
        grid=(t // TM_PROJ,),
        in_specs=[tile(D_MODEL), _resident((1, D_MODEL)), _resident((D_MODEL, D_IN)),
                  _resident((1, NORMED_W)), _resident((MXU_DIM, MXU_DIM))],
        out_specs=[tile(w) for w in widths],
        out_shape=[jax.ShapeDtypeStruct((t, w), BF16) for w in widths],
        compiler_params=_compiler_params(("parallel",)),
        name="in_proj",
    )(x2d, g_attn, w_in_p, gvec, bd)


def _diff_attn_kernel(lam_init, lamv_ref, q_ref, k_ref, v_ref, bias_ref, g_ref, o_ref, s_scr):
    qi = pl.program_id(2)
    nk = k_ref.shape[0] // TK
    lv = lamv_ref[...]
    lam = (jnp.exp(jnp.sum(lv[0:1] * lv[1:2], axis=-1, keepdims=True))
           - jnp.exp(jnp.sum(lv[2:3] * lv[3:4], axis=-1, keepdims=True)) + lam_init)

    q = q_ref[...]
    lane = lax.broadcasted_iota(jnp.int32, q.shape, 1)
    zero = jnp.zeros_like(q)
    qs = jnp.concatenate([jnp.where(lane < HEAD_DIM, q, zero),
                          jnp.where(lane >= HEAD_DIM, q, zero)], axis=0)

    m = jnp.full((2 * TQ, LANES), -jnp.inf, F32)
    for kj in range(nk):
        kt = k_ref[kj * TK:(kj + 1) * TK, :]
        s = lax.dot_general(qs, kt, (((1,), (1,)), ((), ())), preferred_element_type=F32)
        b = bias_ref[jnp.clip(kj - qi, -2, 2) + 2]
        s = s + jnp.concatenate([b, b], axis=0)
        s_scr[:, kj * TK:(kj + 1) * TK] = s
        for c in range(0, TK, LANES):
            m = jnp.maximum(m, s[:, c:c + LANES])
    mrow = jnp.max(m, axis=-1, keepdims=True)

    l = jnp.zeros((2 * TQ, LANES), F32)
    acc = jnp.zeros((2 * TQ, v_ref.shape[1]), F32)
    for kj in range(nk):
        p = jnp.exp(s_scr[:, kj * TK:(kj + 1) * TK] - mrow)
        for c in range(0, TK, LANES):
            l = l + p[:, c:c + LANES]
        acc = acc + jnp.dot(p.astype(BF16), v_ref[kj * TK:(kj + 1) * TK, :], preferred_element_type=F32)
    r = 1.0 / jnp.sum(l, axis=-1, keepdims=True)
    o = acc[:TQ] * r[:TQ] - lam * (acc[TQ:] * r[TQ:])
    ms = jnp.mean(o * o, axis=-1, keepdims=True)
    o = o * lax.rsqrt(ms + EPS) * g_ref[...] * (1.0 - lam_init)
    o_ref[...] = o.astype(BF16)


def _diff_attn(dq, dk, dv, lamv, bias5, subln_g, lam_init, batch, seq):
    nq = seq // TQ
    return pl.pallas_call(
        functools.partial(_diff_attn_kernel, lam_init),
        grid=(batch, DIFF_HEADS, nq),
        in_specs=[
            pl.BlockSpec((8, LANES), lambda b, h, i: (0, 0)),
            pl.BlockSpec((TQ, LANES), lambda b, h, i: (b * nq + i, h)),
            pl.BlockSpec((seq, LANES), lambda b, h, i: (b, h)),
            pl.BlockSpec((seq, LANES), lambda b, h, i: (b, h)),
            pl.BlockSpec((None, 5, TQ, TK), lambda b, h, i: (h, 0, 0, 0)),
            pl.BlockSpec((1, LANES), lambda b, h, i: (0, 0)),
        ],
        out_specs=pl.BlockSpec((TQ, LANES), lambda b, h, i: (b * nq + i, h)),
        out_shape=jax.ShapeDtypeStruct((batch * seq, DIFF_W), BF16),
        scratch_shapes=[pltpu.VMEM((2 * TQ, seq), F32)],
        compiler_params=_compiler_params(("parallel", "parallel", "arbitrary")),
        name="diff_attn",
    )(lamv, dq, dk, dv, bias5, subln_g)


WIN_KEYS = 3 * BLOCK


def _win_attn_kernel(q_ref, k_ref, v_ref, bias_ref, sink_ref, o_ref):
    n = pl.program_id(1)
    seq = k_ref.shape[0]
    nb = seq // BLOCK
    start = pl.multiple_of(jnp.clip(n * BLOCK - BLOCK, 0, seq - WIN_KEYS), BLOCK)
    variant = jnp.where(n == 0, 0, jnp.where(n == nb - 1, 2, 1))
    kk = k_ref[pl.ds(start, WIN_KEYS), :]
    vv = v_ref[pl.ds(start, WIN_KEYS), :]
    q = q_ref[...]
    lane = lax.broadcasted_iota(jnp.int32, (BLOCK, LANES), 1)
    zero = jnp.zeros((BLOCK, LANES), q.dtype)
    lo, hi = [], []
    for j in range(WIN_GROUP):
        qj = q[:, j * LANES:(j + 1) * LANES]
        lo.append(jnp.where(lane < HEAD_DIM, qj, zero))
        hi.append(jnp.where(lane >= HEAD_DIM, qj, zero))
    qs = jnp.concatenate(lo + hi, axis=0)
    s = lax.dot_general(qs, kk, (((1,), (1,)), ((), ())), preferred_element_type=F32)
    s = s + bias_ref[variant]
    sink = sink_ref[...]
    m = jnp.maximum(jnp.max(s, axis=-1, keepdims=True), sink)
    p = jnp.exp(s - m)
    l = jnp.sum(p, axis=-1, keepdims=True) + jnp.exp(sink - m)
    o = jnp.dot((p * (1.0 / l)).astype(BF16), vv, preferred_element_type=F32)
    for j in range(WIN_GROUP):
        blk = jnp.where(lane < HEAD_DIM, o[j * BLOCK:(j + 1) * BLOCK],
                        o[(j + WIN_GROUP) * BLOCK:(j + WIN_GROUP + 1) * BLOCK])
        o_ref[:, j * LANES:(j + 1) * LANES] = blk.astype(BF16)


def _win_attn(wq, wk, wv, bias3, sinkcol, batch, seq):
    nb = seq // BLOCK
    rows = WIN_HEADS * BLOCK
    return pl.pallas_call(
        _win_attn_kernel,
        grid=(batch, nb),
        in_specs=[
            pl.BlockSpec((BLOCK, WIN_Q_W), lambda b, n: (b * nb + n, 0)),
            pl.BlockSpec((seq, WIN_KV_W), lambda b, n: (b, 0)),
            pl.BlockSpec((seq, WIN_KV_W), lambda b, n: (b, 0)),
            _resident((3, rows, WIN_KEYS)),
            _resident((rows, 1)),
        ],
        out_specs=pl.BlockSpec((BLOCK, WIN_Q_W), lambda b, n: (b * nb + n, 0)),
        out_shape=jax.ShapeDtypeStruct((batch * seq, WIN_Q_W), BF16),
        compiler_params=_compiler_params(("parallel", "arbitrary")),
        name="win_attn",
    )(wq, wk, wv, bias3, sinkcol)


def _out_proj_kernel(x_ref, oa_ref, ob_ref, wa_ref, wb_ref, g_ref, x1_ref, h_ref):
    x1 = (x_ref[...]
          + jnp.dot(oa_ref[...], wa_ref[...], preferred_element_type=F32)
          + jnp.dot(ob_ref[...], wb_ref[...], preferred_element_type=F32))
    x1_ref[...] = x1
    ms = jnp.mean(x1 * x1, axis=-1, keepdims=True)
    h_ref[...] = (x1 * lax.rsqrt(ms + EPS) * g_ref[...]).astype(BF16)


def _out_proj(x2d, oa, ob, wa, wb, g_ffn):
    t = x2d.shape[0]
    tile = lambda w: pl.BlockSpec((TM_PROJ, w), lambda i: (i, 0))
    return pl.pallas_call(
        _out_proj_kernel,
        grid=(t // TM_PROJ,),
        in_specs=[tile(D_MODEL), tile(DIFF_W), tile(WIN_Q_W),
                  _resident((DIFF_W, D_MODEL)), _resident((WIN_Q_W, D_MODEL)), _resident((1, D_MODEL))],
        out_specs=[tile(D_MODEL), tile(D_MODEL)],
        out_shape=[jax.ShapeDtypeStruct((t, D_MODEL), F32), jax.ShapeDtypeStruct((t, D_MODEL), BF16)],
        compiler_params=_compiler_params(("parallel",)),
        name="out_proj",
    )(x2d, oa, ob, wa, wb, g_ffn)


N_FF_CHUNKS = D_FF // FF_CHUNK


def _ffn_kernel(x1_ref, h_ref, hp_ref, hn_ref, wg_ref, wu_ref, wd_ref, cw_ref, o_ref, act_scr):
    i = pl.program_id(1)
    last = pl.num_programs(1) - 1
    h = h_ref[...]
    hp = hp_ref[...]
    hn = hn_ref[...]
    row = lax.broadcasted_iota(jnp.int32, (TM_FFN, FF_CHUNK), 0)
    for c in range(N_FF_CHUNKS):
        wg = wg_ref[c]
        g = jnp.dot(h, wg, preferred_element_type=F32)
        g_prev = jnp.dot(hp, wg, preferred_element_type=F32)[HALO - 1:HALO]
        g_next = jnp.dot(hn, wg, preferred_element_type=F32)[0:1]
        g_prev = jnp.where(i == 0, 0.0, g_prev)
        g_next = jnp.where(i == last, 0.0, g_next)
        g_dn = jnp.where(row == 0, g_prev, pltpu.roll(g, 1, 0))
        g_up = jnp.where(row == TM_FFN - 1, g_next, pltpu.roll(g, TM_FFN - 1, 0))
        cw = cw_ref[c]
        u = g_dn * cw[0:1] + g * cw[1:2] + g_up * cw[2:3] + cw[3:4]
        up = jnp.dot(h, wu_ref[c], preferred_element_type=F32)
        act = (u * (1.0 / (1.0 + jnp.exp(-u)))) * up
        act_scr[:, c * FF_CHUNK:(c + 1) * FF_CHUNK] = act.astype(BF16)
    o_ref[...] = x1_ref[...] + jnp.dot(act_scr[...], wd_ref[...], preferred_element_type=F32)


def _ffn(x1, h2, wg, wu, wd, cw, batch, seq):
    nt = seq // TM_FFN
    hb = TM_FFN // HALO
    n_halo = batch * seq // HALO
    tile = lambda: pl.BlockSpec((TM_FFN, D_MODEL), lambda b, i: (b * nt + i, 0))
    return pl.pallas_call(
        _ffn_kernel,
        grid=(batch, nt),
        in_specs=[
            tile(), tile(),
            pl.BlockSpec((HALO, D_MODEL), lambda b, i: (jnp.maximum((b * nt + i) * hb - 1, 0), 0)),
            pl.BlockSpec((HALO, D_MODEL), lambda b, i: (jnp.minimum((b * nt + i + 1) * hb, n_halo - 1), 0)),
            _resident((N_FF_CHUNKS, D_MODEL, FF_CHUNK)),
            _resident((N_FF_CHUNKS, D_MODEL, FF_CHUNK)),
            _resident((D_FF, D_MODEL)),
            _resident((N_FF_CHUNKS, 4, FF_CHUNK)),
        ],
        out_specs=tile(),
        out_shape=jax.ShapeDtypeStruct((batch * seq, D_MODEL), F32),
        scratch_shapes=[pltpu.VMEM((TM_FFN, D_FF), BF16)],
        compiler_params=_compiler_params(("parallel", "arbitrary")),
        name="ffn",
    )(x1, h2, h2, h2, wg, wu, wd, cw)


def _bias_tables(rel_bias, seq):
    d = jnp.arange(-2, 3)[:, None, None] * TK
    rel = d + jnp.arange(TK)[None, None, :] - jnp.arange(TQ)[None, :, None]
    diff = rel_bias[_rel_bucket(rel)][..., :DIFF_HEADS]
    diff = diff.transpose(3, 0, 1, 2).astype(F32)
    off = jnp.arange(3)[:, None, None] * BLOCK
    rel = jnp.arange(WIN_KEYS)[None, None, :] - off - jnp.arange(BLOCK)[None, :, None]
    win = rel_bias[_rel_bucket(rel)][..., DIFF_HEADS:]
    win = jnp.where((jnp.abs(rel) <= WINDOW)[..., None], win, NEG_INF)
    win = win.transpose(0, 3, 1, 2).reshape(3, WIN_HEADS * BLOCK, WIN_KEYS).astype(F32)
    return diff, win


def _layer_params(l, norm_attn_g, w_in, diff_q_norm_g, diff_k_norm_g, diff_lambda_q1, diff_lambda_k1,
                  diff_lambda_q2, diff_lambda_k2, diff_subln_g, win_q_norm_g, win_k_norm_g, win_sink,
                  w_out, norm_ffn_g, w_gate, w_up, conv_w, conv_b, w_down):
    scale = HEAD_DIM ** -0.5
    w = w_in[l]
    dq, dk, dv = w[:, :DIFF_W], w[:, DIFF_W:2 * DIFF_W], w[:, 2 * DIFF_W:3 * DIFF_W]
    wq = w[:, 3 * DIFF_W:3 * DIFF_W + WIN_Q_W].reshape(D_MODEL, WIN_KV_HEADS, WIN_GROUP, HEAD_DIM)
    wq = wq.transpose(0, 2, 1, 3).reshape(D_MODEL, WIN_Q_W)
    wk = w[:, 3 * DIFF_W + WIN_Q_W:3 * DIFF_W + WIN_Q_W + WIN_KV_W]
    wv = w[:, 3 * DIFF_W + WIN_Q_W + WIN_KV_W:]
    w_in_p = jnp.concatenate([dq, dk, wq, wk, dv, wv], axis=1).astype(BF16)
    gvec = jnp.concatenate([jnp.tile(diff_q_norm_g[l], 2 * DIFF_HEADS) * scale,
                            jnp.tile(diff_k_norm_g[l], 2 * DIFF_HEADS),
                            jnp.tile(win_q_norm_g[l], WIN_HEADS) * scale,
                            jnp.tile(win_k_norm_g[l], WIN_KV_HEADS)]).reshape(1, NORMED_W).astype(F32)
    lamv = jnp.zeros((8, LANES), F32)
    lamv = lamv.at[0:4, :HEAD_DIM].set(jnp.stack([diff_lambda_q1[l], diff_lambda_k1[l],
                                                  diff_lambda_q2[l], diff_lambda_k2[l]]).astype(F32))
    wo = w_out[l]
    wb = wo[DIFF_W:].reshape(WIN_KV_HEADS, WIN_GROUP, HEAD_DIM, D_MODEL)
    wb = wb.transpose(1, 0, 2, 3).reshape(WIN_Q_W, D_MODEL)
    chunked = lambda m: m.reshape(D_MODEL, N_FF_CHUNKS, FF_CHUNK).transpose(1, 0, 2).astype(BF16)
    cw = jnp.concatenate([conv_w[l], conv_b[l][None]], axis=0)
    cw = cw.reshape(4, N_FF_CHUNKS, FF_CHUNK).transpose(1, 0, 2).astype(F32)
    return dict(
        g_attn=norm_attn_g[l].reshape(1, D_MODEL).astype(F32), w_in_p=w_in_p, gvec=gvec, lamv=lamv,
        subln_g=diff_subln_g[l].reshape(1, 2 * HEAD_DIM).astype(F32),
        sinkcol=jnp.repeat(win_sink[l].astype(F32), BLOCK).reshape(WIN_HEADS * BLOCK, 1),
        wa=wo[:DIFF_W].astype(BF16), wb=wb.astype(BF16),
        g_ffn=norm_ffn_g[l].reshape(1, D_MODEL).astype(F32),
        wg=chunked(w_gate[l]), wu=chunked(w_up[l]), wd=w_down[l].astype(BF16), cw=cw,
        lam_init=0.8 - 0.6 * math.exp(-0.3 * l),
    )


def _group_sum_matrix():
    idx = np.arange(MXU_DIM) // HEAD_DIM
    return jnp.asarray(idx[:, None] == idx[None, :], dtype=BF16)


def _encoder_layer(x, p, bias5, bias3, bd):
    batch, seq, _ = x.shape
    x2d = x.reshape(batch * seq, D_MODEL)
    dq, dk, wq, wk, dv, wv = _in_proj(x2d, p["g_attn"], p["w_in_p"], p["gvec"], bd)
    oa = _diff_attn(dq, dk, dv, p["lamv"], bias5, p["subln_g"], p["lam_init"], batch, seq)
    ob = _win_attn(wq, wk, wv, bias3, p["sinkcol"], batch, seq)
    x1, h2 = _out_proj(x2d, oa, ob, p["wa"], p["wb"], p["g_ffn"])
    y = _ffn(x1, h2, p["wg"], p["wu"], p["wd"], p["cw"], batch, seq)
    return y.reshape(batch, seq, D_MODEL)


def kernel(x_prompt, x_sample, norm_attn_g, w_in, diff_q_norm_g, diff_k_norm_g, diff_lambda_q1, diff_lambda_k1, diff_lambda_q2, diff_lambda_k2, diff_subln_g, win_q_norm_g, win_k_norm_g, win_sink, rel_bias, w_out, norm_ffn_g, w_gate, w_up, conv_w, conv_b, w_down):
    depth = w_in.shape[0]
    layers = [_layer_params(l, norm_attn_g, w_in, diff_q_norm_g, diff_k_norm_g, diff_lambda_q1,
                            diff_lambda_k1, diff_lambda_q2, diff_lambda_k2, diff_subln_g, win_q_norm_g,
                            win_k_norm_g, win_sink, w_out, norm_ffn_g, w_gate, w_up, conv_w, conv_b, w_down)
              for l in range(depth)]
    bias5, bias3 = _bias_tables(rel_bias, x_prompt.shape[1])
    bd = _group_sum_matrix()

    def run(x):
        for p in layers:
            x = _encoder_layer(x, p, bias5, bias3, bd)
        return x

    return run(x_prompt), run(x_sample)
```

```python
import functools
import math

import jax
import jax.numpy as jnp
import numpy as np
from jax import lax
from jax.experimental import pallas as pl
from jax.experimental.pallas import tpu as pltpu

D_MODEL = 1024
SEQ = 2048
HEAD_DIM = 64
DIFF_HEADS = 4
WIN_HEADS = 8
WIN_KV_HEADS = 2
WIN_GROUP = WIN_HEADS // WIN_KV_HEADS
WINDOW = 128
BLOCK = 128
N_BUCKETS = 32
MAX_DISTANCE = 128
D_FF = 2816
EPS = 1e-6
NEG_INF = -1e30

DIFF_W = DIFF_HEADS * 2 * HEAD_DIM
WIN_Q_W = WIN_HEADS * HEAD_DIM
WIN_KV_W = WIN_KV_HEADS * HEAD_DIM
D_IN = 3 * DIFF_W + WIN_Q_W + 2 * WIN_KV_W
NORMED_W = 2 * DIFF_W + WIN_Q_W + WIN_KV_W

LANES = 128
MXU_DIM = 256
VMEM_LIMIT = 56 * 1024 * 1024

TM_PROJ = 512
TQ = 256
TK = 256
STRIP = 16
LOG2E = math.log2(math.e)
TM_FFN = 512
FF_CHUNK = 256
HALO = 16

BF16 = jnp.bfloat16
F32 = jnp.float32


def _rel_bucket(rel):
    nb = N_BUCKETS // 2
    max_exact = nb // 2
    ret = jnp.where(rel > 0, nb, 0)
    n = jnp.abs(rel)
    nf = jnp.maximum(n, 1).astype(F32)
    large = max_exact + (jnp.log(nf / max_exact) / math.log(MAX_DISTANCE / max_exact)
                         * (nb - max_exact)).astype(jnp.int32)
    large = jnp.minimum(large, nb - 1)
    return ret + jnp.where(n < max_exact, n, large)


def _compiler_params(semantics):
    return pltpu.CompilerParams(dimension_semantics=semantics, vmem_limit_bytes=VMEM_LIMIT)


def _resident(shape):
    nd = len(shape)
    return pl.BlockSpec(shape, lambda *_: (0,) * nd, pipeline_mode=pl.Buffered(1))


def _in_proj_kernel(x_ref, g_ref, w_ref, gv_ref, bd_ref,
                    dq_ref, dk_ref, wq_ref, wk_ref, dv_ref, wv_ref):
    x = x_ref[...]
    ms = jnp.mean(x * x, axis=-1, keepdims=True)
    h = (x * lax.rsqrt(ms + EPS) * g_ref[...]).astype(BF16)
    proj = jnp.dot(h, w_ref[...], preferred_element_type=F32)
    bd = bd_ref[...]

    def head_norm(c0, width):
        blk = proj[:, c0:c0 + width]
        sq = (blk * blk).astype(BF16)
        ss = jnp.dot(sq, bd[:width, :width], preferred_element_type=F32)
        return (blk * lax.rsqrt(ss * (1.0 / HEAD_DIM) + EPS) * gv_ref[:, c0:c0 + width]).astype(BF16)

    for ref, base in ((dq_ref, 0), (dk_ref, DIFF_W), (wq_ref, 2 * DIFF_W)):
        for c in range(0, DIFF_W, MXU_DIM):
            ref[:, c:c + MXU_DIM] = head_norm(base + c, MXU_DIM)
    wk_ref[...] = head_norm(2 * DIFF_W + WIN_Q_W, WIN_KV_W)
    dv_ref[...] = proj[:, NORMED_W:NORMED_W + DIFF_W].astype(BF16)
    wv_ref[...] = proj[:, NORMED_W + DIFF_W:].astype(BF16)


def _in_proj(x2d, g_attn, w_in_p, gvec, bd):
    t = x2d.shape[0]
    tile = lambda w: pl.BlockSpec((TM_PROJ, w), lambda i: (i, 0))
    widths = (DIFF_W, DIFF_W, WIN_Q_W, WIN_KV_W, DIFF_W, WIN_KV_W)
    return pl.pallas_call(
        _in_proj_kernel,
        grid=(t // TM_PROJ,),
        in_specs=[tile(D_MODEL), _resident((1, D_MODEL)), _resident((D_MODEL, D_IN)),
                  _resident((1, NORMED_W)), _resident((MXU_DIM, MXU_DIM))],
        out_specs=[tile(w) for w in widths],
        out_shape=[jax.ShapeDtypeStruct((t, w), BF16) for w in widths],
        compiler_params=_compiler_params(("parallel",)),
        name="in_proj",
    )(x2d, g_attn, w_in_p, gvec, bd)


def _diff_attn_kernel(lam_init, lamv_ref, q_ref, k_ref, v_ref, bias_ref, g_ref, o_ref,
                      s1_scr, s2_scr, p1_scr, p2_scr):
    qi = pl.program_id(2)
    nk = k_ref.shape[0] // TK
    lv = lamv_ref[...]
    lam = (jnp.exp(jnp.sum(lv[0:1] * lv[1:2], axis=-1, keepdims=True))
           - jnp.exp(jnp.sum(lv[2:3] * lv[3:4], axis=-1, keepdims=True)) + lam_init)

    q = q_ref[...]
    lane = lax.broadcasted_iota(jnp.int32, q.shape, 1)
    zero = jnp.zeros_like(q)
    q_maps = (jnp.where(lane < HEAD_DIM, q, zero), jnp.where(lane >= HEAD_DIM, q, zero))
    k = k_ref[...]
    v = v_ref[...]
    tiles = [jnp.clip(kj - qi, -2, 2) + 2 for kj in range(nk)]

    for qm, s_scr in zip(q_maps, (s1_scr, s2_scr)):
        s_scr[...] = lax.dot_general(qm, k, (((1,), (1,)), ((), ())), preferred_element_type=F32)

    outs = []
    for s_scr, p_scr in ((s1_scr, p1_scr), (s2_scr, p2_scr)):
        sums = []
        for r in range(0, TQ, STRIP):
            rows = slice(r, r + STRIP)
            t = jnp.concatenate([s_scr[rows, kj * TK:(kj + 1) * TK] + bias_ref[tiles[kj], rows, :]
                                 for kj in range(nk)], axis=1)
            p = jnp.exp2(t - jnp.max(t, axis=-1, keepdims=True))
            sums.append(jnp.sum(p, axis=-1, keepdims=True))
            p_scr[rows, :] = p.astype(BF16)
        l = jnp.concatenate(sums, axis=0)
        outs.append(jnp.dot(p_scr[...], v, preferred_element_type=F32) * (1.0 / l))
    o = outs[0] - lam * outs[1]
    ms = jnp.mean(o * o, axis=-1, keepdims=True)
    o = o * lax.rsqrt(ms + EPS) * g_ref[...] * (1.0 - lam_init)
    o_ref[...] = o.astype(BF16)


def _diff_attn(dq, dk, dv, lamv, bias5, subln_g, lam_init, batch, seq):
    nq = seq // TQ
    return pl.pallas_call(
        functools.partial(_diff_attn_kernel, lam_init),
        grid=(batch, DIFF_HEADS, nq),
        in_specs=[
            pl.BlockSpec((8, LANES), lambda b, h, i: (0, 0)),
            pl.BlockSpec((TQ, LANES), lambda b, h, i: (b * nq + i, h)),
            pl.BlockSpec((seq, LANES), lambda b, h, i: (b, h)),
            pl.BlockSpec((seq, LANES), lambda b, h, i: (b, h)),
            pl.BlockSpec((None, 5, TQ, TK), lambda b, h, i: (h, 0, 0, 0)),
            pl.BlockSpec((1, LANES), lambda b, h, i: (0, 0)),
        ],
        out_specs=pl.BlockSpec((TQ, LANES), lambda b, h, i: (b * nq + i, h)),
        out_shape=jax.ShapeDtypeStruct((batch * seq, DIFF_W), BF16),
        scratch_shapes=[pltpu.VMEM((TQ, seq), F32), pltpu.VMEM((TQ, seq), F32),
                        pltpu.VMEM((TQ, seq), BF16), pltpu.VMEM((TQ, seq), BF16)],
        compiler_params=_compiler_params(("parallel", "parallel", "arbitrary")),
        name="diff_attn",
    )(lamv, dq, dk, dv, bias5, subln_g)


WIN_KEYS = 3 * BLOCK


def _win_attn_kernel(q_ref, k_ref, v_ref, bias_ref, sink_ref, o_ref, s0_scr, s1_scr, p0_scr, p1_scr):
    n = pl.program_id(1)
    seq = k_ref.shape[0]
    nb = seq // BLOCK
    start = pl.multiple_of(jnp.clip(n * BLOCK - BLOCK, 0, seq - WIN_KEYS), BLOCK)
    variant = jnp.where(n == 0, 0, jnp.where(n == nb - 1, 2, 1))
    kk = k_ref[pl.ds(start, WIN_KEYS), :]
    vv = v_ref[pl.ds(start, WIN_KEYS), :]
    q = q_ref[...]
    lane = lax.broadcasted_iota(jnp.int32, (BLOCK, LANES), 1)
    zero = jnp.zeros((BLOCK, LANES), q.dtype)
    group_rows = WIN_GROUP * BLOCK
    for kv, s_scr in enumerate((s0_scr, s1_scr)):
        keep = (lane < HEAD_DIM) if kv == 0 else (lane >= HEAD_DIM)
        qg = jnp.concatenate([jnp.where(keep, q[:, j * LANES:(j + 1) * LANES], zero)
                              for j in range(WIN_GROUP)], axis=0)
        s_scr[...] = lax.dot_general(qg, kk, (((1,), (1,)), ((), ())), preferred_element_type=F32)

    outs = []
    for kv, (s_scr, p_scr) in enumerate(((s0_scr, p0_scr), (s1_scr, p1_scr))):
        sums = []
        for r in range(0, group_rows, STRIP):
            rows = slice(r, r + STRIP)
            grows = slice(kv * group_rows + r, kv * group_rows + r + STRIP)
            t = s_scr[rows, :] + bias_ref[variant, grows, :]
            sink = sink_ref[(kv * group_rows + r) // BLOCK]
            m = jnp.maximum(jnp.max(t, axis=-1, keepdims=True), sink)
            p = jnp.exp2(t - m)
            sums.append(jnp.sum(p, axis=-1, keepdims=True) + jnp.exp2(sink - m))
            p_scr[rows, :] = p.astype(BF16)
        l = jnp.concatenate(sums, axis=0)
        outs.append(jnp.dot(p_scr[...], vv, preferred_element_type=F32) * (1.0 / l))
    for j in range(WIN_GROUP):
        blk = jnp.where(lane < HEAD_DIM, outs[0][j * BLOCK:(j + 1) * BLOCK], outs[1][j * BLOCK:(j + 1) * BLOCK])
        o_ref[:, j * LANES:(j + 1) * LANES] = blk.astype(BF16)


def _win_attn(wq, wk, wv, bias3, sink, batch, seq):
    nb = seq // BLOCK
    rows = WIN_HEADS * BLOCK
    return pl.pallas_call(
        _win_attn_kernel,
        grid=(batch, nb),
        in_specs=[
            pl.BlockSpec((BLOCK, WIN_Q_W), lambda b, n: (b * nb + n, 0)),
            pl.BlockSpec((seq, WIN_KV_W), lambda b, n: (b, 0)),
            pl.BlockSpec((seq, WIN_KV_W), lambda b, n: (b, 0)),
            _resident((3, rows, WIN_KEYS)),
            pl.BlockSpec(memory_space=pltpu.SMEM),
        ],
        out_specs=pl.BlockSpec((BLOCK, WIN_Q_W), lambda b, n: (b * nb + n, 0)),
        out_shape=jax.ShapeDtypeStruct((batch * seq, WIN_Q_W), BF16),
        scratch_shapes=[pltpu.VMEM((rows // 2, WIN_KEYS), F32), pltpu.VMEM((rows // 2, WIN_KEYS), F32),
                        pltpu.VMEM((rows // 2, WIN_KEYS), BF16), pltpu.VMEM((rows // 2, WIN_KEYS), BF16)],
        compiler_params=_compiler_params(("parallel", "arbitrary")),
        name="win_attn",
    )(wq, wk, wv, bias3, sink)


def _out_proj_kernel(x_ref, oa_ref, ob_ref, wa_ref, wb_ref, g_ref, x1_ref, h_ref):
    x1 = (x_ref[...]
          + jnp.dot(oa_ref[...], wa_ref[...], preferred_element_type=F32)
          + jnp.dot(ob_ref[...], wb_ref[...], preferred_element_type=F32))
    x1_ref[...] = x1
    ms = jnp.mean(x1 * x1, axis=-1, keepdims=True)
    h_ref[...] = (x1 * lax.rsqrt(ms + EPS) * g_ref[...]).astype(BF16)


def _out_proj(x2d, oa, ob, wa, wb, g_ffn):
    t = x2d.shape[0]
    tile = lambda w: pl.BlockSpec((TM_PROJ, w), lambda i: (i, 0))
    return pl.pallas_call(
        _out_proj_kernel,
        grid=(t // TM_PROJ,),
        in_specs=[tile(D_MODEL), tile(DIFF_W), tile(WIN_Q_W),
                  _resident((DIFF_W, D_MODEL)), _resident((WIN_Q_W, D_MODEL)), _resident((1, D_MODEL))],
        out_specs=[tile(D_MODEL), tile(D_MODEL)],
        out_shape=[jax.ShapeDtypeStruct((t, D_MODEL), F32), jax.ShapeDtypeStruct((t, D_MODEL), BF16)],
        compiler_params=_compiler_params(("parallel",)),
        name="out_proj",
    )(x2d, oa, ob, wa, wb, g_ffn)


N_FF_CHUNKS = D_FF // FF_CHUNK


def _ffn_kernel(x1_ref, h_ref, hp_ref, hn_ref, wg_ref, wu_ref, wd_ref, cw_ref, o_ref, act_scr):
    i = pl.program_id(1)
    last = pl.num_programs(1) - 1
    h = h_ref[...]
    hp = hp_ref[...]
    hn = hn_ref[...]
    row = lax.broadcasted_iota(jnp.int32, (TM_FFN, FF_CHUNK), 0)
    for c in range(N_FF_CHUNKS):
        cols = slice(c * FF_CHUNK, (c + 1) * FF_CHUNK)
        wg = wg_ref[:, cols]
        g = jnp.dot(h, wg, preferred_element_type=F32)
        g_prev = jnp.dot(hp, wg, preferred_element_type=F32)[HALO - 1:HALO]
        g_next = jnp.dot(hn, wg, preferred_element_type=F32)[0:1]
        g_prev = jnp.where(i == 0, 0.0, g_prev)
        g_next = jnp.where(i == last, 0.0, g_next)
        g_dn = jnp.where(row == 0, g_prev, pltpu.roll(g, 1, 0))
        g_up = jnp.where(row == TM_FFN - 1, g_next, pltpu.roll(g, TM_FFN - 1, 0))
        cw = cw_ref[:, cols]
        u = g_dn * cw[0:1] + g * cw[1:2] + g_up * cw[2:3] + cw[3:4]
        up = jnp.dot(h, wu_ref[:, cols], preferred_element_type=F32)
        act = (u * (1.0 / (1.0 + jnp.exp(-u)))) * up
        act_scr[:, cols] = act.astype(BF16)
    o_ref[...] = x1_ref[...] + jnp.dot(act_scr[...], wd_ref[...], preferred_element_type=F32)


def _ffn(x1, h2, wg, wu, wd, cw, batch, seq):
    nt = seq // TM_FFN
    hb = TM_FFN // HALO
    n_halo = batch * seq // HALO
    tile = lambda: pl.BlockSpec((TM_FFN, D_MODEL), lambda b, i: (b * nt + i, 0))
    return pl.pallas_call(
        _ffn_kernel,
        grid=(batch, nt),
        in_specs=[
            tile(), tile(),
            pl.BlockSpec((HALO, D_MODEL), lambda b, i: (jnp.maximum((b * nt + i) * hb - 1, 0), 0)),
            pl.BlockSpec((HALO, D_MODEL), lambda b, i: (jnp.minimum((b * nt + i + 1) * hb, n_halo - 1), 0)),
            _resident((D_MODEL, D_FF)),
            _resident((D_MODEL, D_FF)),
            _resident((D_FF, D_MODEL)),
            _resident((4, D_FF)),
        ],
        out_specs=tile(),
        out_shape=jax.ShapeDtypeStruct((batch * seq, D_MODEL), F32),
        scratch_shapes=[pltpu.VMEM((TM_FFN, D_FF), BF16)],
        compiler_params=_compiler_params(("parallel", "arbitrary")),
        name="ffn",
    )(x1, h2, h2, h2, wg, wu, wd, cw)


def _skew(vec, rows):
    h, n = vec.shape
    w = jnp.pad(vec, ((0, 0), (0, 1)))
    return jnp.tile(w, (1, rows))[:, :rows * n].reshape(h, rows, n)


def _bias_tables(rel_bias):
    table = rel_bias.astype(F32).T
    n = 4 * TQ - 1
    rel = jnp.arange(n) - (2 * TQ - 1)
    vec = table[:DIFF_HEADS][:, _rel_bucket(rel)]
    near = _skew(vec, TQ)[:, :, TQ - 1:TQ - 1 + 3 * TK]
    near = near.reshape(DIFF_HEADS, TQ, 3, TK).transpose(0, 2, 1, 3)
    far = lambda r: jnp.broadcast_to(table[:DIFF_HEADS, _rel_bucket(jnp.int32(r))][:, None, None, None],
                                     (DIFF_HEADS, 1, TQ, TK))
    diff = jnp.concatenate([far(-2 * TK), near, far(2 * TK)], axis=1) * LOG2E
    n = 6 * BLOCK - 1
    rel = jnp.arange(n) - (3 * BLOCK - 1)
    vec = jnp.where(jnp.abs(rel) <= WINDOW, table[DIFF_HEADS:][:, _rel_bucket(rel)], NEG_INF)
    band = _skew(vec, BLOCK)[:, :, BLOCK - 1:BLOCK - 1 + 5 * BLOCK]
    win = jnp.stack([band[:, :, (2 - v) * BLOCK:(2 - v) * BLOCK + WIN_KEYS] for v in range(3)])
    win = win.reshape(3, WIN_HEADS * BLOCK, WIN_KEYS) * LOG2E
    return diff, win


def _layer_params(l, norm_attn_g, w_in, diff_q_norm_g, diff_k_norm_g, diff_lambda_q1, diff_lambda_k1,
                  diff_lambda_q2, diff_lambda_k2, diff_subln_g, win_q_norm_g, win_k_norm_g, win_sink,
                  w_out, norm_ffn_g, w_gate, w_up, conv_w, conv_b, w_down):
    scale = HEAD_DIM ** -0.5
    w = w_in[l]
    dq, dk, dv = w[:, :DIFF_W], w[:, DIFF_W:2 * DIFF_W], w[:, 2 * DIFF_W:3 * DIFF_W]
    wq = w[:, 3 * DIFF_W:3 * DIFF_W + WIN_Q_W].reshape(D_MODEL, WIN_KV_HEADS, WIN_GROUP, HEAD_DIM)
    wq = wq.transpose(0, 2, 1, 3).reshape(D_MODEL, WIN_Q_W)
    wk = w[:, 3 * DIFF_W + WIN_Q_W:3 * DIFF_W + WIN_Q_W + WIN_KV_W]
    wv = w[:, 3 * DIFF_W + WIN_Q_W + WIN_KV_W:]
    w_in_p = jnp.concatenate([dq, dk, wq, wk, dv, wv], axis=1).astype(BF16)
    gvec = jnp.concatenate([jnp.tile(diff_q_norm_g[l], 2 * DIFF_HEADS) * (scale * LOG2E),
                            jnp.tile(diff_k_norm_g[l], 2 * DIFF_HEADS),
                            jnp.tile(win_q_norm_g[l], WIN_HEADS) * (scale * LOG2E),
                            jnp.tile(win_k_norm_g[l], WIN_KV_HEADS)]).reshape(1, NORMED_W).astype(F32)
    lamv = jnp.zeros((8, LANES), F32)
    lamv = lamv.at[0:4, :HEAD_DIM].set(jnp.stack([diff_lambda_q1[l], diff_lambda_k1[l],
                                                  diff_lambda_q2[l], diff_lambda_k2[l]]).astype(F32))
    wo = w_out[l]
    wb = wo[DIFF_W:].reshape(WIN_KV_HEADS, WIN_GROUP, HEAD_DIM, D_MODEL)
    wb = wb.transpose(1, 0, 2, 3).reshape(WIN_Q_W, D_MODEL)
    cw = jnp.concatenate([conv_w[l], conv_b[l][None]], axis=0).astype(F32)
    return dict(
        g_attn=norm_attn_g[l].reshape(1, D_MODEL).astype(F32), w_in_p=w_in_p, gvec=gvec, lamv=lamv,
        subln_g=diff_subln_g[l].reshape(1, 2 * HEAD_DIM).astype(F32),
        sink=win_sink[l].astype(F32) * LOG2E,
        wa=wo[:DIFF_W].astype(BF16), wb=wb.astype(BF16),
        g_ffn=norm_ffn_g[l].reshape(1, D_MODEL).astype(F32),
        wg=w_gate[l].astype(BF16), wu=w_up[l].astype(BF16), wd=w_down[l].astype(BF16), cw=cw,
        lam_init=0.8 - 0.6 * math.exp(-0.3 * l),
    )


def _group_sum_matrix():
    idx = np.arange(MXU_DIM) // HEAD_DIM
    return jnp.asarray(idx[:, None] == idx[None, :], dtype=BF16)


def _encoder_layer(x, p, bias5, bias3, bd):
    batch, seq, _ = x.shape
    x2d = x.reshape(batch * seq, D_MODEL)
    dq, dk, wq, wk, dv, wv = _in_proj(x2d, p["g_attn"], p["w_in_p"], p["gvec"], bd)
    oa = _diff_attn(dq, dk, dv, p["lamv"], bias5, p["subln_g"], p["lam_init"], batch, seq)
    ob = _win_attn(wq, wk, wv, bias3, p["sink"], batch, seq)
    x1, h2 = _out_proj(x2d, oa, ob, p["wa"], p["wb"], p["g_ffn"])
    y = _ffn(x1, h2, p["wg"], p["wu"], p["wd"], p["cw"], batch, seq)
    return y.reshape(batch, seq, D_MODEL)


def kernel(x_prompt, x_sample, norm_attn_g, w_in, diff_q_norm_g, diff_k_norm_g, diff_lambda_q1, diff_lambda_k1, diff_lambda_q2, diff_lambda_k2, diff_subln_g, win_q_norm_g, win_k_norm_g, win_sink, rel_bias, w_out, norm_ffn_g, w_gate, w_up, conv_w, conv_b, w_down):
    depth = w_in.shape[0]
    layers = [_layer_params(l, norm_attn_g, w_in, diff_q_norm_g, diff_k_norm_g, diff_lambda_q1,
                            diff_lambda_k1, diff_lambda_q2, diff_lambda_k2, diff_subln_g, win_q_norm_g,
                            win_k_norm_g, win_sink, w_out, norm_ffn_g, w_gate, w_up, conv_w, conv_b, w_down)
              for l in range(depth)]
    bias5, bias3 = _bias_tables(rel_bias)
    bd = _group_sum_matrix()

    def run(x):
        for p in layers:
            x = _encoder_layer(x, p, bias5, bias3, bd)
        return x

    return run(x_prompt), run(x_sample)
```

```python
import functools
import math

import jax
import jax.numpy as jnp
import numpy as np
from jax import lax
from jax.experimental import pallas as pl
from jax.experimental.pallas import tpu as pltpu

D_MODEL = 1024
SEQ = 2048
HEAD_DIM = 64
DIFF_HEADS = 4
WIN_HEADS = 8
WIN_KV_HEADS = 2
WIN_GROUP = WIN_HEADS // WIN_KV_HEADS
WINDOW = 128
BLOCK = 128
N_BUCKETS = 32
MAX_DISTANCE = 128
D_FF = 2816
EPS = 1e-6
NEG_INF = -1e30

DIFF_W = DIFF_HEADS * 2 * HEAD_DIM
WIN_Q_W = WIN_HEADS * HEAD_DIM
WIN_KV_W = WIN_KV_HEADS * HEAD_DIM
D_IN = 3 * DIFF_W + WIN_Q_W + 2 * WIN_KV_W
NORMED_W = 2 * DIFF_W + WIN_Q_W + WIN_KV_W

LANES = 128
MXU_DIM = 256
VMEM_LIMIT = 56 * 1024 * 1024

TM_PROJ = 512
TQ = 256
TK = 256
TILES_PER_ITER = 4
STRIP = 16
LOG2E = math.log2(math.e)
MAX_UNSHIFTED_LOGIT = 100.0
NORM_SLACK = 1.02
TM_FFN = 512
FF_CHUNK = 256
HALO = 16

BF16 = jnp.bfloat16
F32 = jnp.float32


def _rel_bucket(rel):
    nb = N_BUCKETS // 2
    max_exact = nb // 2
    ret = jnp.where(rel > 0, nb, 0)
    n = jnp.abs(rel)
    nf = jnp.maximum(n, 1).astype(F32)
    large = max_exact + (jnp.log(nf / max_exact) / math.log(MAX_DISTANCE / max_exact)
                         * (nb - max_exact)).astype(jnp.int32)
    large = jnp.minimum(large, nb - 1)
    return ret + jnp.where(n < max_exact, n, large)


def _compiler_params(semantics):
    return pltpu.CompilerParams(dimension_semantics=semantics, vmem_limit_bytes=VMEM_LIMIT)


def _resident(shape):
    nd = len(shape)
    return pl.BlockSpec(shape, lambda *_: (0,) * nd, pipeline_mode=pl.Buffered(1))


def _in_proj_kernel(x_ref, g_ref, w_ref, gv_ref, bd_ref,
                    dq_ref, dk_ref, wq_ref, wk_ref, dv_ref, wv_ref):
    x = x_ref[...]
    ms = jnp.mean(x * x, axis=-1, keepdims=True)
    h = (x * lax.rsqrt(ms + EPS) * g_ref[...]).astype(BF16)
    proj = jnp.dot(h, w_ref[...], preferred_element_type=F32)
    bd = bd_ref[...]

    def head_norm(c0, width):
        blk = proj[:, c0:c0 + width]
        sq = (blk * blk).astype(BF16)
        ss = jnp.dot(sq, bd[:width, :width], preferred_element_type=F32)
        return (blk * lax.rsqrt(ss * (1.0 / HEAD_DIM) + EPS) * gv_ref[:, c0:c0 + width]).astype(BF16)

    for ref, base in ((dq_ref, 0), (dk_ref, DIFF_W), (wq_ref, 2 * DIFF_W)):
        for c in range(0, DIFF_W, MXU_DIM):
            ref[:, c:c + MXU_DIM] = head_norm(base + c, MXU_DIM)
    wk_ref[...] = head_norm(2 * DIFF_W + WIN_Q_W, WIN_KV_W)
    dv_ref[...] = proj[:, NORMED_W:NORMED_W + DIFF_W].astype(BF16)
    wv_ref[...] = proj[:, NORMED_W + DIFF_W:].astype(BF16)


def _in_proj(x2d, g_attn, w_in_p, gvec, bd):
    t = x2d.shape[0]
    tile = lambda w: pl.BlockSpec((TM_PROJ, w), lambda i: (i, 0))
    widths = (DIFF_W, DIFF_W, WIN_Q_W, WIN_KV_W, DIFF_W, WIN_KV_W)
    return pl.pallas_call(
        _in_proj_kernel,
        grid=(t // TM_PROJ,),
        in_specs=[tile(D_MODEL), _resident((1, D_MODEL)), _resident((D_MODEL, D_IN)),
                  _resident((1, NORMED_W)), _resident((MXU_DIM, MXU_DIM))],
        out_specs=[tile(w) for w in widths],
        out_shape=[jax.ShapeDtypeStruct((t, w), BF16) for w in widths],
        compiler_params=_compiler_params(("parallel",)),
        name="in_proj",
    )(x2d, g_attn, w_in_p, gvec, bd)


def _diff_attn_kernel(lam_init, bounded_ref, lamv_ref, q_ref, k_ref, v_ref, bias_ref, g_ref, o_ref,
                      s_scr, p_scr, p_alt_scr, v_ones_scr):
    seq = k_ref.shape[0]
    nq, nk = seq // TQ, seq // TK
    lv = lamv_ref[...]
    lam = (jnp.exp(jnp.sum(lv[0:1] * lv[1:2], axis=-1, keepdims=True))
           - jnp.exp(jnp.sum(lv[2:3] * lv[3:4], axis=-1, keepdims=True)) + lam_init)
    lane = lax.broadcasted_iota(jnp.int32, (TQ, LANES), 1)
    v_ones_scr[:, :LANES] = v_ref[...]
    v_ones_scr[:, LANES:] = jnp.ones(v_ref.shape, BF16)

    def stacked_q(tile):
        q = q_ref[pl.ds(pl.multiple_of(tile * TQ, TQ), TQ), :]
        zero = jnp.zeros_like(q)
        return jnp.concatenate([jnp.where(lane < HEAD_DIM, q, zero), jnp.where(lane >= HEAD_DIM, q, zero)], axis=0)

    def key_logits(qs, kj):
        return lax.dot_general(qs, k_ref[kj * TK:(kj + 1) * TK, :], (((1,), (1,)), ((), ())),
                               preferred_element_type=F32)

    def finish(tile, p_ref):
        res = jnp.dot(p_ref[...], v_ones_scr[...], preferred_element_type=F32)
        out = res[:, :LANES] * (1.0 / res[:, LANES:])
        o = out[:TQ] - lam * out[TQ:]
        ms = jnp.mean(o * o, axis=-1, keepdims=True)
        o = o * lax.rsqrt(ms + EPS) * g_ref[...] * (1.0 - lam_init)
        o_ref[pl.ds(pl.multiple_of(tile * TQ, TQ), TQ), :] = o.astype(BF16)

    def bias_tiles(tile):
        return [jnp.clip(kj - tile, -2, 2) + 2 for kj in range(nk)]

    def unshifted_tile(tile, p_ref):
        qs, tiles = stacked_q(tile), bias_tiles(tile)
        for kj in range(nk):
            b = bias_ref[tiles[kj]]
            s = key_logits(qs, kj)
            p_ref[:TQ, kj * TK:(kj + 1) * TK] = jnp.exp2(s[:TQ] + b).astype(BF16)
            p_ref[TQ:, kj * TK:(kj + 1) * TK] = jnp.exp2(s[TQ:] + b).astype(BF16)
        finish(tile, p_ref)

    def unshifted_tiles(i, carry):
        for u in range(TILES_PER_ITER):
            unshifted_tile(i * TILES_PER_ITER + u, (p_scr, p_alt_scr)[u % 2])
        return carry

    def shifted_tile(tile, carry):
        qs, tiles = stacked_q(tile), bias_tiles(tile)
        for kj in range(nk):
            s_scr[:, kj * TK:(kj + 1) * TK] = key_logits(qs, kj)

        def strip(i, c):
            for mp in range(2):
                rows = pl.ds(pl.multiple_of(mp * TQ + i * STRIP, STRIP), STRIP)
                brows = pl.ds(pl.multiple_of(i * STRIP, STRIP), STRIP)
                t = jnp.concatenate([s_scr[rows, kj * TK:(kj + 1) * TK] + bias_ref[tiles[kj], brows, :]
                                     for kj in range(nk)], axis=1)
                p_scr[rows, :] = jnp.exp2(t - jnp.max(t, axis=-1, keepdims=True)).astype(BF16)
            return c

        lax.fori_loop(0, TQ // STRIP, strip, 0)
        finish(tile, p_scr)
        return carry

    bounded = bounded_ref[0] != 0

    @pl.when(bounded)
    def _():
        lax.fori_loop(0, nq // TILES_PER_ITER, unshifted_tiles, 0)

    @pl.when(jnp.logical_not(bounded))
    def _():
        lax.fori_loop(0, nq, shifted_tile, 0)


def _diff_attn(dq, dk, dv, bounded, lamv, bias5, subln_g, lam_init, batch, seq):
    seq_block = pl.BlockSpec((seq, LANES), lambda b, h: (b, h))
    return pl.pallas_call(
        functools.partial(_diff_attn_kernel, lam_init),
        grid=(batch, DIFF_HEADS),
        in_specs=[
            pl.BlockSpec(memory_space=pltpu.SMEM),
            pl.BlockSpec((8, LANES), lambda b, h: (0, 0)),
            seq_block, seq_block, seq_block,
            pl.BlockSpec((None, 5, TQ, TK), lambda b, h: (h, 0, 0, 0)),
            pl.BlockSpec((1, LANES), lambda b, h: (0, 0)),
        ],
        out_specs=seq_block,
        out_shape=jax.ShapeDtypeStruct((batch * seq, DIFF_W), BF16),
        scratch_shapes=[pltpu.VMEM((2 * TQ, seq), F32), pltpu.VMEM((2 * TQ, seq), BF16),
                        pltpu.VMEM((2 * TQ, seq), BF16), pltpu.VMEM((seq, 2 * LANES), BF16)],
        compiler_params=_compiler_params(("parallel", "parallel")),
        name="diff_attn",
    )(bounded, lamv, dq, dk, dv, bias5, subln_g)


WIN_KEYS = 3 * BLOCK
BLOCKS_PER_ITER = 4


def _win_attn_kernel(bounded_ref, sink_ref, q_ref, k_ref, v_ref, bias_ref, o_ref, s_scr, p_scr, v_ones_scr):
    seq = k_ref.shape[0]
    nb = seq // BLOCK
    group_rows = WIN_GROUP * BLOCK
    lane = lax.broadcasted_iota(jnp.int32, (BLOCK, LANES), 1)
    v_ones_scr[:, :LANES] = v_ref[...]
    v_ones_scr[:, LANES:] = jnp.ones(v_ref.shape, BF16)

    def window(n):
        start = pl.multiple_of(jnp.clip(n * BLOCK - BLOCK, 0, seq - WIN_KEYS), BLOCK)
        variant = jnp.where(n == 0, 0, jnp.where(n == nb - 1, 2, 1))
        return start, variant

    def grouped_q(n, kv):
        q = q_ref[pl.ds(pl.multiple_of(n * BLOCK, BLOCK), BLOCK), :]
        zero = jnp.zeros((BLOCK, LANES), q.dtype)
        keep = (lane < HEAD_DIM) if kv == 0 else (lane >= HEAD_DIM)
        return jnp.concatenate([jnp.where(keep, q[:, j * LANES:(j + 1) * LANES], zero)
                                for j in range(WIN_GROUP)], axis=0)

    def group_logits(n, kv, start):
        return lax.dot_general(grouped_q(n, kv), k_ref[pl.ds(start, WIN_KEYS), :], (((1,), (1,)), ((), ())),
                               preferred_element_type=F32)

    def store(n, outs):
        rows = pl.ds(pl.multiple_of(n * BLOCK, BLOCK), BLOCK)
        for j in range(WIN_GROUP):
            blk = jnp.where(lane < HEAD_DIM, outs[0][j * BLOCK:(j + 1) * BLOCK], outs[1][j * BLOCK:(j + 1) * BLOCK])
            o_ref[rows, j * LANES:(j + 1) * LANES] = blk.astype(BF16)

    def unshifted_block(n):
        start, variant = window(n)
        outs = []
        for kv in range(WIN_KV_HEADS):
            grows = slice(kv * group_rows, (kv + 1) * group_rows)
            p = jnp.exp2(group_logits(n, kv, start) + bias_ref[variant, grows, :]).astype(BF16)
            res = jnp.dot(p, v_ones_scr[pl.ds(start, WIN_KEYS), :], preferred_element_type=F32)
            sink_w = jnp.concatenate(
                [jnp.broadcast_to(jnp.exp2(jnp.full((1, LANES), sink_ref[kv * WIN_GROUP + g], F32)), (BLOCK, LANES))
                 for g in range(WIN_GROUP)], axis=0)
            outs.append(res[:, :LANES] * (1.0 / (res[:, LANES:] + sink_w)))
        store(n, outs)

    def unshifted_blocks(i, carry):
        for u in range(BLOCKS_PER_ITER):
            unshifted_block(i * BLOCKS_PER_ITER + u)
        return carry

    def shifted_block(n, carry):
        start, variant = window(n)
        outs = []
        for kv in range(WIN_KV_HEADS):
            s_scr[...] = group_logits(n, kv, start)
            sink_terms = []
            for r in range(0, group_rows, STRIP):
                rows = slice(r, r + STRIP)
                grows = slice(kv * group_rows + r, kv * group_rows + r + STRIP)
                t = s_scr[rows, :] + bias_ref[variant, grows, :]
                sink = sink_ref[(kv * group_rows + r) // BLOCK]
                row_max = jnp.max(t, axis=-1, keepdims=True)
                p_scr[rows, :] = jnp.exp2(t - jnp.maximum(row_max, sink)).astype(BF16)
                m_wide = jnp.maximum(jnp.broadcast_to(row_max, (STRIP, LANES)), sink)
                sink_terms.append(jnp.exp2(sink - m_wide))
            res = jnp.dot(p_scr[...], v_ones_scr[pl.ds(start, WIN_KEYS), :], preferred_element_type=F32)
            l = res[:, LANES:] + jnp.concatenate(sink_terms, axis=0)
            outs.append(res[:, :LANES] * (1.0 / l))
        store(n, outs)
        return carry

    bounded = bounded_ref[0] != 0

    @pl.when(bounded)
    def _():
        lax.fori_loop(0, nb // BLOCKS_PER_ITER, unshifted_blocks, 0)

    @pl.when(jnp.logical_not(bounded))
    def _():
        lax.fori_loop(0, nb, shifted_block, 0)


def _win_attn(wq, wk, wv, bounded, bias3, sink, batch, seq):
    rows = WIN_HEADS * BLOCK
    smem = pl.BlockSpec(memory_space=pltpu.SMEM)
    seq_block = lambda w: pl.BlockSpec((seq, w), lambda b: (b, 0))
    return pl.pallas_call(
        _win_attn_kernel,
        grid=(batch,),
        in_specs=[smem, smem, seq_block(WIN_Q_W), seq_block(WIN_KV_W), seq_block(WIN_KV_W),
                  _resident((3, rows, WIN_KEYS))],
        out_specs=seq_block(WIN_Q_W),
        out_shape=jax.ShapeDtypeStruct((batch * seq, WIN_Q_W), BF16),
        scratch_shapes=[pltpu.VMEM((rows // 2, WIN_KEYS), F32), pltpu.VMEM((rows // 2, WIN_KEYS), BF16),
                        pltpu.VMEM((seq, 2 * LANES), BF16)],
        compiler_params=_compiler_params(("parallel",)),
        name="win_attn",
    )(bounded, sink, wq, wk, wv, bias3)


def _out_proj_kernel(x_ref, oa_ref, ob_ref, wa_ref, wb_ref, g_ref, x1_ref, h_ref):
    x1 = (x_ref[...]
          + jnp.dot(oa_ref[...], wa_ref[...], preferred_element_type=F32)
          + jnp.dot(ob_ref[...], wb_ref[...], preferred_element_type=F32))
    x1_ref[...] = x1
    ms = jnp.mean(x1 * x1, axis=-1, keepdims=True)
    h_ref[...] = (x1 * lax.rsqrt(ms + EPS) * g_ref[...]).astype(BF16)


def _out_proj(x2d, oa, ob, wa, wb, g_ffn):
    t = x2d.shape[0]
    tile = lambda w: pl.BlockSpec((TM_PROJ, w), lambda i: (i, 0))
    return pl.pallas_call(
        _out_proj_kernel,
        grid=(t // TM_PROJ,),
        in_specs=[tile(D_MODEL), tile(DIFF_W), tile(WIN_Q_W),
                  _resident((DIFF_W, D_MODEL)), _resident((WIN_Q_W, D_MODEL)), _resident((1, D_MODEL))],
        out_specs=[tile(D_MODEL), tile(D_MODEL)],
        out_shape=[jax.ShapeDtypeStruct((t, D_MODEL), F32), jax.ShapeDtypeStruct((t, D_MODEL), BF16)],
        compiler_params=_compiler_params(("parallel",)),
        name="out_proj",
    )(x2d, oa, ob, wa, wb, g_ffn)


N_FF_CHUNKS = D_FF // FF_CHUNK


def _ffn_kernel(x1_ref, h_ref, hp_ref, hn_ref, wg_ref, wu_ref, wd_ref, cw_ref, o_ref, act_scr):
    i = pl.program_id(1)
    last = pl.num_programs(1) - 1
    h = h_ref[...]
    hp = hp_ref[...]
    hn = hn_ref[...]
    row = lax.broadcasted_iota(jnp.int32, (TM_FFN, FF_CHUNK), 0)
    for c in range(N_FF_CHUNKS):
        cols = slice(c * FF_CHUNK, (c + 1) * FF_CHUNK)
        wg = wg_ref[:, cols]
        g = jnp.dot(h, wg, preferred_element_type=F32)
        g_prev = jnp.dot(hp, wg, preferred_element_type=F32)[HALO - 1:HALO]
        g_next = jnp.dot(hn, wg, preferred_element_type=F32)[0:1]
        g_prev = jnp.where(i == 0, 0.0, g_prev)
        g_next = jnp.where(i == last, 0.0, g_next)
        g_dn = jnp.where(row == 0, g_prev, pltpu.roll(g, 1, 0))
        g_up = jnp.where(row == TM_FFN - 1, g_next, pltpu.roll(g, TM_FFN - 1, 0))
        cw = cw_ref[:, cols]
        u = g_dn * cw[0:1] + g * cw[1:2] + g_up * cw[2:3] + cw[3:4]
        up = jnp.dot(h, wu_ref[:, cols], preferred_element_type=F32)
        act = (u * (1.0 / (1.0 + jnp.exp(-u)))) * up
        act_scr[:, cols] = act.astype(BF16)
    o_ref[...] = x1_ref[...] + jnp.dot(act_scr[...], wd_ref[...], preferred_element_type=F32)


def _ffn(x1, h2, wg, wu, wd, cw, batch, seq):
    nt = seq // TM_FFN
    hb = TM_FFN // HALO
    n_halo = batch * seq // HALO
    tile = lambda: pl.BlockSpec((TM_FFN, D_MODEL), lambda b, i: (b * nt + i, 0))
    return pl.pallas_call(
        _ffn_kernel,
        grid=(batch, nt),
        in_specs=[
            tile(), tile(),
            pl.BlockSpec((HALO, D_MODEL), lambda b, i: (jnp.maximum((b * nt + i) * hb - 1, 0), 0)),
            pl.BlockSpec((HALO, D_MODEL), lambda b, i: (jnp.minimum((b * nt + i + 1) * hb, n_halo - 1), 0)),
            _resident((D_MODEL, D_FF)),
            _resident((D_MODEL, D_FF)),
            _resident((D_FF, D_MODEL)),
            _resident((4, D_FF)),
        ],
        out_specs=tile(),
        out_shape=jax.ShapeDtypeStruct((batch * seq, D_MODEL), F32),
        scratch_shapes=[pltpu.VMEM((TM_FFN, D_FF), BF16)],
        compiler_params=_compiler_params(("parallel", "arbitrary")),
        name="ffn",
    )(x1, h2, h2, h2, wg, wu, wd, cw)


def _skew(vec, rows):
    h, n = vec.shape
    w = jnp.pad(vec, ((0, 0), (0, 1)))
    return jnp.tile(w, (1, rows))[:, :rows * n].reshape(h, rows, n)


def _bias_tables(rel_bias):
    table = rel_bias.astype(F32).T
    n = 4 * TQ - 1
    rel = jnp.arange(n) - (2 * TQ - 1)
    vec = table[:DIFF_HEADS][:, _rel_bucket(rel)]
    near = _skew(vec, TQ)[:, :, TQ - 1:TQ - 1 + 3 * TK]
    near = near.reshape(DIFF_HEADS, TQ, 3, TK).transpose(0, 2, 1, 3)
    far = lambda r: jnp.broadcast_to(table[:DIFF_HEADS, _rel_bucket(jnp.int32(r))][:, None, None, None],
                                     (DIFF_HEADS, 1, TQ, TK))
    diff = jnp.concatenate([far(-2 * TK), near, far(2 * TK)], axis=1)
    hi, lo = jnp.max(table[:DIFF_HEADS], axis=1), jnp.min(table[:DIFF_HEADS], axis=1)
    diff = (diff - (0.5 * (hi + lo))[:, None, None, None]) * LOG2E
    diff_half_range = jnp.max(0.5 * (hi - lo)) * LOG2E
    n = 6 * BLOCK - 1
    rel = jnp.arange(n) - (3 * BLOCK - 1)
    vec = jnp.where(jnp.abs(rel) <= WINDOW, table[DIFF_HEADS:][:, _rel_bucket(rel)], NEG_INF)
    band = _skew(vec, BLOCK)[:, :, BLOCK - 1:BLOCK - 1 + 5 * BLOCK]
    win = jnp.stack([band[:, :, (2 - v) * BLOCK:(2 - v) * BLOCK + WIN_KEYS] for v in range(3)])
    win = win.reshape(3, WIN_HEADS * BLOCK, WIN_KEYS) * LOG2E
    win_hi = jnp.max(table[DIFF_HEADS:], axis=1) * LOG2E
    win_lo = jnp.min(table[DIFF_HEADS:], axis=1) * LOG2E
    return dict(diff=diff, diff_half_range=diff_half_range, win=win, win_hi=win_hi, win_lo=win_lo)


def _qk_bound(q_gain, k_gain):
    return HEAD_DIM * NORM_SLACK * jnp.max(jnp.abs(q_gain)) * jnp.max(jnp.abs(k_gain))


def _is_bounded(half_range):
    return (half_range <= MAX_UNSHIFTED_LOGIT).astype(jnp.int32).reshape(1)


def _layer_params(l, tables, norm_attn_g, w_in, diff_q_norm_g, diff_k_norm_g, diff_lambda_q1, diff_lambda_k1,
                  diff_lambda_q2, diff_lambda_k2, diff_subln_g, win_q_norm_g, win_k_norm_g, win_sink,
                  w_out, norm_ffn_g, w_gate, w_up, conv_w, conv_b, w_down):
    scale = HEAD_DIM ** -0.5
    w = w_in[l]
    dq, dk, dv = w[:, :DIFF_W], w[:, DIFF_W:2 * DIFF_W], w[:, 2 * DIFF_W:3 * DIFF_W]
    wq = w[:, 3 * DIFF_W:3 * DIFF_W + WIN_Q_W].reshape(D_MODEL, WIN_KV_HEADS, WIN_GROUP, HEAD_DIM)
    wq = wq.transpose(0, 2, 1, 3).reshape(D_MODEL, WIN_Q_W)
    wk = w[:, 3 * DIFF_W + WIN_Q_W:3 * DIFF_W + WIN_Q_W + WIN_KV_W]
    wv = w[:, 3 * DIFF_W + WIN_Q_W + WIN_KV_W:]
    w_in_p = jnp.concatenate([dq, dk, wq, wk, dv, wv], axis=1).astype(BF16)
    gvec = jnp.concatenate([jnp.tile(diff_q_norm_g[l], 2 * DIFF_HEADS) * (scale * LOG2E),
                            jnp.tile(diff_k_norm_g[l], 2 * DIFF_HEADS),
                            jnp.tile(win_q_norm_g[l], WIN_HEADS) * (scale * LOG2E),
                            jnp.tile(win_k_norm_g[l], WIN_KV_HEADS)]).reshape(1, NORMED_W).astype(F32)
    lamv = jnp.zeros((8, LANES), F32)
    lamv = lamv.at[0:4, :HEAD_DIM].set(jnp.stack([diff_lambda_q1[l], diff_lambda_k1[l],
                                                  diff_lambda_q2[l], diff_lambda_k2[l]]).astype(F32))
    wo = w_out[l]
    wb = wo[DIFF_W:].reshape(WIN_KV_HEADS, WIN_GROUP, HEAD_DIM, D_MODEL)
    wb = wb.transpose(1, 0, 2, 3).reshape(WIN_Q_W, D_MODEL)
    cw = jnp.concatenate([conv_w[l], conv_b[l][None]], axis=0).astype(F32)
    return dict(
        g_attn=norm_attn_g[l].reshape(1, D_MODEL).astype(F32), w_in_p=w_in_p, gvec=gvec, lamv=lamv,
        subln_g=diff_subln_g[l].reshape(1, 2 * HEAD_DIM).astype(F32),
        wa=wo[:DIFF_W].astype(BF16), wb=wb.astype(BF16),
        g_ffn=norm_ffn_g[l].reshape(1, D_MODEL).astype(F32),
        wg=w_gate[l].astype(BF16), wu=w_up[l].astype(BF16), wd=w_down[l].astype(BF16), cw=cw,
        lam_init=0.8 - 0.6 * math.exp(-0.3 * l),
        diff_bounded=_is_bounded(_qk_bound(diff_q_norm_g[l].astype(F32) * (scale * LOG2E), diff_k_norm_g[l])
                                 + tables["diff_half_range"]),
        **_window_softmax_params(tables, win_q_norm_g[l].astype(F32) * (scale * LOG2E), win_k_norm_g[l],
                                 win_sink[l].astype(F32) * LOG2E),
    )


def _window_softmax_params(tables, q_gain, k_gain, sink):
    qk = _qk_bound(q_gain, k_gain)
    hi = jnp.maximum(tables["win_hi"] + qk, sink)
    lo = jnp.minimum(tables["win_lo"] - qk, sink)
    centre = 0.5 * (hi + lo)
    return dict(win_bias=tables["win"] - jnp.repeat(centre, BLOCK)[None, :, None], win_sink=sink - centre,
                win_bounded=_is_bounded(jnp.max(0.5 * (hi - lo))))


def _group_sum_matrix():
    idx = np.arange(MXU_DIM) // HEAD_DIM
    return jnp.asarray(idx[:, None] == idx[None, :], dtype=BF16)


def _encoder_layer(x, p, bias5, bd):
    batch, seq, _ = x.shape
    x2d = x.reshape(batch * seq, D_MODEL)
    dq, dk, wq, wk, dv, wv = _in_proj(x2d, p["g_attn"], p["w_in_p"], p["gvec"], bd)
    oa = _diff_attn(dq, dk, dv, p["diff_bounded"], p["lamv"], bias5, p["subln_g"], p["lam_init"], batch, seq)
    ob = _win_attn(wq, wk, wv, p["win_bounded"], p["win_bias"], p["win_sink"], batch, seq)
    x1, h2 = _out_proj(x2d, oa, ob, p["wa"], p["wb"], p["g_ffn"])
    y = _ffn(x1, h2, p["wg"], p["wu"], p["wd"], p["cw"], batch, seq)
    return y.reshape(batch, seq, D_MODEL)


def kernel(x_prompt, x_sample, norm_attn_g, w_in, diff_q_norm_g, diff_k_norm_g, diff_lambda_q1, diff_lambda_k1, diff_lambda_q2, diff_lambda_k2, diff_subln_g, win_q_norm_g, win_k_norm_g, win_sink, rel_bias, w_out, norm_ffn_g, w_gate, w_up, conv_w, conv_b, w_down):
    depth = w_in.shape[0]
    tables = _bias_tables(rel_bias)
    layers = [_layer_params(l, tables, norm_attn_g, w_in, diff_q_norm_g, diff_k_norm_g, diff_lambda_q1,
                            diff_lambda_k1, diff_lambda_q2, diff_lambda_k2, diff_subln_g, win_q_norm_g,
                            win_k_norm_g, win_sink, w_out, norm_ffn_g, w_gate, w_up, conv_w, conv_b, w_down)
              for l in range(depth)]
    bd = _group_sum_matrix()

    def run(x):
        for p in layers:
            x = _encoder_layer(x, p, tables["diff"], bd)
        return x

    return run(x_prompt), run(x_sample)
```

```python
import functools
import math

import jax
import jax.numpy as jnp
import numpy as np
from jax import lax
from jax.experimental import pallas as pl
from jax.experimental.pallas import tpu as pltpu

D_MODEL = 1024
SEQ = 2048
HEAD_DIM = 64
DIFF_HEADS = 4
WIN_HEADS = 8
WIN_KV_HEADS = 2
WIN_GROUP = WIN_HEADS // WIN_KV_HEADS
WINDOW = 128
BLOCK = 128
N_BUCKETS = 32
MAX_DISTANCE = 128
D_FF = 2816
EPS = 1e-6
NEG_INF = -1e30

DIFF_W = DIFF_HEADS * 2 * HEAD_DIM
WIN_Q_W = WIN_HEADS * HEAD_DIM
WIN_KV_W = WIN_KV_HEADS * HEAD_DIM
D_IN = 3 * DIFF_W + WIN_Q_W + 2 * WIN_KV_W
NORMED_W = 2 * DIFF_W + WIN_Q_W + WIN_KV_W

LANES = 128
MXU_DIM = 256
VMEM_LIMIT = 56 * 1024 * 1024

TM_PROJ = 1024
TQ = 256
TK = 256
TILES_PER_ITER = 4
STRIP = 16
LOG2E = math.log2(math.e)
MAX_UNSHIFTED_LOGIT = 100.0
NORM_SLACK = 1.02
TM_FFN = 1024
FF_CHUNK = 512
HALO = 16

BF16 = jnp.bfloat16
F32 = jnp.float32


def _rel_bucket(rel):
    nb = N_BUCKETS // 2
    max_exact = nb // 2
    ret = jnp.where(rel > 0, nb, 0)
    n = jnp.abs(rel)
    nf = jnp.maximum(n, 1).astype(F32)
    large = max_exact + (jnp.log(nf / max_exact) / math.log(MAX_DISTANCE / max_exact)
                         * (nb - max_exact)).astype(jnp.int32)
    large = jnp.minimum(large, nb - 1)
    return ret + jnp.where(n < max_exact, n, large)


def _compiler_params(semantics):
    return pltpu.CompilerParams(dimension_semantics=semantics, vmem_limit_bytes=VMEM_LIMIT)


def _resident(shape):
    nd = len(shape)
    return pl.BlockSpec(shape, lambda *_: (0,) * nd, pipeline_mode=pl.Buffered(1))


def _in_proj_kernel(x_ref, g_ref, w_ref, gv_ref, bd_ref,
                    dq_ref, dk_ref, wq_ref, wk_ref, dv_ref, wv_ref):
    x = x_ref[...]
    ms = jnp.mean(x * x, axis=-1, keepdims=True)
    h = (x * lax.rsqrt(ms + EPS) * g_ref[...]).astype(BF16)
    proj = jnp.dot(h, w_ref[...], preferred_element_type=F32)
    bd = bd_ref[...]

    def head_norm(c0, width):
        blk = proj[:, c0:c0 + width]
        sq = (blk * blk).astype(BF16)
        ss = jnp.dot(sq, bd[:width, :width], preferred_element_type=F32)
        return (blk * lax.rsqrt(ss * (1.0 / HEAD_DIM) + EPS) * gv_ref[:, c0:c0 + width]).astype(BF16)

    for ref, base in ((dq_ref, 0), (dk_ref, DIFF_W), (wq_ref, 2 * DIFF_W)):
        for c in range(0, DIFF_W, MXU_DIM):
            ref[:, c:c + MXU_DIM] = head_norm(base + c, MXU_DIM)
    wk_ref[...] = head_norm(2 * DIFF_W + WIN_Q_W, WIN_KV_W)
    dv_ref[...] = proj[:, NORMED_W:NORMED_W + DIFF_W].astype(BF16)
    wv_ref[...] = proj[:, NORMED_W + DIFF_W:].astype(BF16)


def _in_proj(x2d, g_attn, w_in_p, gvec, bd):
    t = x2d.shape[0]
    tile = lambda w: pl.BlockSpec((TM_PROJ, w), lambda i: (i, 0))
    widths = (DIFF_W, DIFF_W, WIN_Q_W, WIN_KV_W, DIFF_W, WIN_KV_W)
    return pl.pallas_call(
        _in_proj_kernel,
        grid=(t // TM_PROJ,),
        in_specs=[tile(D_MODEL), _resident((1, D_MODEL)), _resident((D_MODEL, D_IN)),
                  _resident((1, NORMED_W)), _resident((MXU_DIM, MXU_DIM))],
        out_specs=[tile(w) for w in widths],
        out_shape=[jax.ShapeDtypeStruct((t, w), BF16) for w in widths],
        compiler_params=_compiler_params(("parallel",)),
        name="in_proj",
    )(x2d, g_attn, w_in_p, gvec, bd)


def _diff_attn_kernel(lam_init, bounded_ref, lamv_ref, q_ref, k_ref, v_ref, bias_ref, g_ref, o_ref,
                      s_scr, p_scr, p_alt_scr, v_ones_scr):
    seq = k_ref.shape[0]
    nq, nk = seq // TQ, seq // TK
    lv = lamv_ref[...]
    lam = (jnp.exp(jnp.sum(lv[0:1] * lv[1:2], axis=-1, keepdims=True))
           - jnp.exp(jnp.sum(lv[2:3] * lv[3:4], axis=-1, keepdims=True)) + lam_init)
    lane = lax.broadcasted_iota(jnp.int32, (TQ, LANES), 1)
    v_ones_scr[:, :LANES] = v_ref[...]
    v_ones_scr[:, LANES:] = jnp.ones(v_ref.shape, BF16)

    def stacked_q(tile):
        q = q_ref[pl.ds(pl.multiple_of(tile * TQ, TQ), TQ), :]
        zero = jnp.zeros_like(q)
        return jnp.concatenate([jnp.where(lane < HEAD_DIM, q, zero), jnp.where(lane >= HEAD_DIM, q, zero)], axis=0)

    def key_logits(qs, kj):
        return lax.dot_general(qs, k_ref[kj * TK:(kj + 1) * TK, :], (((1,), (1,)), ((), ())),
                               preferred_element_type=F32)

    def finish(tile, p_ref):
        res = jnp.dot(p_ref[...], v_ones_scr[...], preferred_element_type=F32)
        out = res[:, :LANES] * (1.0 / res[:, LANES:])
        o = out[:TQ] - lam * out[TQ:]
        ms = jnp.mean(o * o, axis=-1, keepdims=True)
        o = o * lax.rsqrt(ms + EPS) * g_ref[...] * (1.0 - lam_init)
        o_ref[pl.ds(pl.multiple_of(tile * TQ, TQ), TQ), :] = o.astype(BF16)

    def bias_tiles(tile):
        return [jnp.clip(kj - tile, -2, 2) + 2 for kj in range(nk)]

    def unshifted_tile(tile, p_ref):
        qs, tiles = stacked_q(tile), bias_tiles(tile)
        for kj in range(nk):
            b = bias_ref[tiles[kj]]
            s = key_logits(qs, kj)
            p_ref[:TQ, kj * TK:(kj + 1) * TK] = jnp.exp2(s[:TQ] + b).astype(BF16)
            p_ref[TQ:, kj * TK:(kj + 1) * TK] = jnp.exp2(s[TQ:] + b).astype(BF16)
        finish(tile, p_ref)

    def unshifted_tiles(i, carry):
        for u in range(TILES_PER_ITER):
            unshifted_tile(i * TILES_PER_ITER + u, (p_scr, p_alt_scr)[u % 2])
        return carry

    def shifted_tile(tile, carry):
        qs, tiles = stacked_q(tile), bias_tiles(tile)
        for kj in range(nk):
            s_scr[:, kj * TK:(kj + 1) * TK] = key_logits(qs, kj)

        def strip(i, c):
            for mp in range(2):
                rows = pl.ds(pl.multiple_of(mp * TQ + i * STRIP, STRIP), STRIP)
                brows = pl.ds(pl.multiple_of(i * STRIP, STRIP), STRIP)
                t = jnp.concatenate([s_scr[rows, kj * TK:(kj + 1) * TK] + bias_ref[tiles[kj], brows, :]
                                     for kj in range(nk)], axis=1)
                p_scr[rows, :] = jnp.exp2(t - jnp.max(t, axis=-1, keepdims=True)).astype(BF16)
            return c

        lax.fori_loop(0, TQ // STRIP, strip, 0)
        finish(tile, p_scr)
        return carry

    bounded = bounded_ref[0] != 0

    @pl.when(bounded)
    def _():
        lax.fori_loop(0, nq // TILES_PER_ITER, unshifted_tiles, 0)

    @pl.when(jnp.logical_not(bounded))
    def _():
        lax.fori_loop(0, nq, shifted_tile, 0)


def _diff_attn(dq, dk, dv, bounded, lamv, bias5, subln_g, lam_init, batch, seq):
    seq_block = pl.BlockSpec((seq, LANES), lambda b, h: (b, h))
    return pl.pallas_call(
        functools.partial(_diff_attn_kernel, lam_init),
        grid=(batch, DIFF_HEADS),
        in_specs=[
            pl.BlockSpec(memory_space=pltpu.SMEM),
            pl.BlockSpec((8, LANES), lambda b, h: (0, 0)),
            seq_block, seq_block, seq_block,
            pl.BlockSpec((None, 5, TQ, TK), lambda b, h: (h, 0, 0, 0)),
            pl.BlockSpec((1, LANES), lambda b, h: (0, 0)),
        ],
        out_specs=seq_block,
        out_shape=jax.ShapeDtypeStruct((batch * seq, DIFF_W), BF16),
        scratch_shapes=[pltpu.VMEM((2 * TQ, seq), F32), pltpu.VMEM((2 * TQ, seq), BF16),
                        pltpu.VMEM((2 * TQ, seq), BF16), pltpu.VMEM((seq, 2 * LANES), BF16)],
        compiler_params=_compiler_params(("parallel", "parallel")),
        name="diff_attn",
    )(bounded, lamv, dq, dk, dv, bias5, subln_g)


WIN_KEYS = 3 * BLOCK
BLOCKS_PER_ITER = 4


def _win_attn_kernel(bounded_ref, sink_ref, q_ref, k_ref, v_ref, bias_ref, o_ref, s_scr, p_scr, v_ones_scr):
    seq = k_ref.shape[0]
    nb = seq // BLOCK
    group_rows = WIN_GROUP * BLOCK
    lane = lax.broadcasted_iota(jnp.int32, (BLOCK, LANES), 1)
    v_ones_scr[:, :LANES] = v_ref[...]
    v_ones_scr[:, LANES:] = jnp.ones(v_ref.shape, BF16)

    def window(n):
        start = pl.multiple_of(jnp.clip(n * BLOCK - BLOCK, 0, seq - WIN_KEYS), BLOCK)
        variant = jnp.where(n == 0, 0, jnp.where(n == nb - 1, 2, 1))
        return start, variant

    def grouped_q(n, kv):
        q = q_ref[pl.ds(pl.multiple_of(n * BLOCK, BLOCK), BLOCK), :]
        zero = jnp.zeros((BLOCK, LANES), q.dtype)
        keep = (lane < HEAD_DIM) if kv == 0 else (lane >= HEAD_DIM)
        return jnp.concatenate([jnp.where(keep, q[:, j * LANES:(j + 1) * LANES], zero)
                                for j in range(WIN_GROUP)], axis=0)

    def group_logits(n, kv, start):
        return lax.dot_general(grouped_q(n, kv), k_ref[pl.ds(start, WIN_KEYS), :], (((1,), (1,)), ((), ())),
                               preferred_element_type=F32)

    def store(n, outs):
        rows = pl.ds(pl.multiple_of(n * BLOCK, BLOCK), BLOCK)
        for j in range(WIN_GROUP):
            blk = jnp.where(lane < HEAD_DIM, outs[0][j * BLOCK:(j + 1) * BLOCK], outs[1][j * BLOCK:(j + 1) * BLOCK])
            o_ref[rows, j * LANES:(j + 1) * LANES] = blk.astype(BF16)

    def unshifted_block(n):
        start, variant = window(n)
        outs = []
        for kv in range(WIN_KV_HEADS):
            grows = slice(kv * group_rows, (kv + 1) * group_rows)
            p = jnp.exp2(group_logits(n, kv, start) + bias_ref[variant, grows, :]).astype(BF16)
            res = jnp.dot(p, v_ones_scr[pl.ds(start, WIN_KEYS), :], preferred_element_type=F32)
            sink_w = jnp.concatenate(
                [jnp.broadcast_to(jnp.exp2(jnp.full((1, LANES), sink_ref[kv * WIN_GROUP + g], F32)), (BLOCK, LANES))
                 for g in range(WIN_GROUP)], axis=0)
            outs.append(res[:, :LANES] * (1.0 / (res[:, LANES:] + sink_w)))
        store(n, outs)

    def unshifted_blocks(i, carry):
        for u in range(BLOCKS_PER_ITER):
            unshifted_block(i * BLOCKS_PER_ITER + u)
        return carry

    def shifted_block(n, carry):
        start, variant = window(n)
        outs = []
        for kv in range(WIN_KV_HEADS):
            s_scr[...] = group_logits(n, kv, start)
            sink_terms = []
            for r in range(0, group_rows, STRIP):
                rows = slice(r, r + STRIP)
                grows = slice(kv * group_rows + r, kv * group_rows + r + STRIP)
                t = s_scr[rows, :] + bias_ref[variant, grows, :]
                sink = sink_ref[(kv * group_rows + r) // BLOCK]
                row_max = jnp.max(t, axis=-1, keepdims=True)
                p_scr[rows, :] = jnp.exp2(t - jnp.maximum(row_max, sink)).astype(BF16)
                m_wide = jnp.maximum(jnp.broadcast_to(row_max, (STRIP, LANES)), sink)
                sink_terms.append(jnp.exp2(sink - m_wide))
            res = jnp.dot(p_scr[...], v_ones_scr[pl.ds(start, WIN_KEYS), :], preferred_element_type=F32)
            l = res[:, LANES:] + jnp.concatenate(sink_terms, axis=0)
            outs.append(res[:, :LANES] * (1.0 / l))
        store(n, outs)
        return carry

    bounded = bounded_ref[0] != 0

    @pl.when(bounded)
    def _():
        lax.fori_loop(0, nb // BLOCKS_PER_ITER, unshifted_blocks, 0)

    @pl.when(jnp.logical_not(bounded))
    def _():
        lax.fori_loop(0, nb, shifted_block, 0)


def _win_attn(wq, wk, wv, bounded, bias3, sink, batch, seq):
    rows = WIN_HEADS * BLOCK
    smem = pl.BlockSpec(memory_space=pltpu.SMEM)
    seq_block = lambda w: pl.BlockSpec((seq, w), lambda b: (b, 0))
    return pl.pallas_call(
        _win_attn_kernel,
        grid=(batch,),
        in_specs=[smem, smem, seq_block(WIN_Q_W), seq_block(WIN_KV_W), seq_block(WIN_KV_W),
                  _resident((3, rows, WIN_KEYS))],
        out_specs=seq_block(WIN_Q_W),
        out_shape=jax.ShapeDtypeStruct((batch * seq, WIN_Q_W), BF16),
        scratch_shapes=[pltpu.VMEM((rows // 2, WIN_KEYS), F32), pltpu.VMEM((rows // 2, WIN_KEYS), BF16),
                        pltpu.VMEM((seq, 2 * LANES), BF16)],
        compiler_params=_compiler_params(("parallel",)),
        name="win_attn",
    )(bounded, sink, wq, wk, wv, bias3)


def _out_proj_kernel(x_ref, oa_ref, ob_ref, wa_ref, wb_ref, g_ref, x1_ref, h_ref):
    x1 = (x_ref[...]
          + jnp.dot(oa_ref[...], wa_ref[...], preferred_element_type=F32)
          + jnp.dot(ob_ref[...], wb_ref[...], preferred_element_type=F32))
    x1_ref[...] = x1
    ms = jnp.mean(x1 * x1, axis=-1, keepdims=True)
    h_ref[...] = (x1 * lax.rsqrt(ms + EPS) * g_ref[...]).astype(BF16)


def _out_proj(x2d, oa, ob, wa, wb, g_ffn):
    t = x2d.shape[0]
    tile = lambda w: pl.BlockSpec((TM_PROJ, w), lambda i: (i, 0))
    return pl.pallas_call(
        _out_proj_kernel,
        grid=(t // TM_PROJ,),
        in_specs=[tile(D_MODEL), tile(DIFF_W), tile(WIN_Q_W),
                  _resident((DIFF_W, D_MODEL)), _resident((WIN_Q_W, D_MODEL)), _resident((1, D_MODEL))],
        out_specs=[tile(D_MODEL), tile(D_MODEL)],
        out_shape=[jax.ShapeDtypeStruct((t, D_MODEL), F32), jax.ShapeDtypeStruct((t, D_MODEL), BF16)],
        compiler_params=_compiler_params(("parallel",)),
        name="out_proj",
    )(x2d, oa, ob, wa, wb, g_ffn)


def _ffn_kernel(x1_ref, h_ref, hp_ref, hn_ref, wg_ref, wu_ref, wd_ref, cw_ref, o_ref, act_scr):
    i = pl.program_id(1)
    last = pl.num_programs(1) - 1
    h = h_ref[...]
    zero = jnp.zeros((HALO, D_MODEL), BF16)
    h_ext = jnp.concatenate([jnp.where(i == 0, zero, hp_ref[...]), h,
                             jnp.where(i == last, zero, hn_ref[...])], axis=0)
    ext_rows = TM_FFN + 2 * HALO
    tile = slice(HALO, HALO + TM_FFN)
    for c0 in range(0, D_FF, FF_CHUNK):
        cols = slice(c0, min(c0 + FF_CHUNK, D_FF))
        g_ext = jnp.dot(h_ext, wg_ref[:, cols], preferred_element_type=F32)
        g = g_ext[tile]
        g_dn = pltpu.roll(g_ext, 1, 0)[tile]
        g_up = pltpu.roll(g_ext, ext_rows - 1, 0)[tile]
        cw = cw_ref[:, cols]
        u = g_dn * cw[0:1] + g * cw[1:2] + g_up * cw[2:3] + cw[3:4]
        up = jnp.dot(h, wu_ref[:, cols], preferred_element_type=F32)
        act = (u * (1.0 / (1.0 + jnp.exp(-u)))) * up
        act_scr[:, cols] = act.astype(BF16)
    o_ref[...] = x1_ref[...] + jnp.dot(act_scr[...], wd_ref[...], preferred_element_type=F32)


def _ffn(x1, h2, wg, wu, wd, cw, batch, seq):
    nt = seq // TM_FFN
    hb = TM_FFN // HALO
    n_halo = batch * seq // HALO
    tile = lambda: pl.BlockSpec((TM_FFN, D_MODEL), lambda b, i: (b * nt + i, 0))
    return pl.pallas_call(
        _ffn_kernel,
        grid=(batch, nt),
        in_specs=[
            tile(), tile(),
            pl.BlockSpec((HALO, D_MODEL), lambda b, i: (jnp.maximum((b * nt + i) * hb - 1, 0), 0)),
            pl.BlockSpec((HALO, D_MODEL), lambda b, i: (jnp.minimum((b * nt + i + 1) * hb, n_halo - 1), 0)),
            _resident((D_MODEL, D_FF)),
            _resident((D_MODEL, D_FF)),
            _resident((D_FF, D_MODEL)),
            _resident((4, D_FF)),
        ],
        out_specs=tile(),
        out_shape=jax.ShapeDtypeStruct((batch * seq, D_MODEL), F32),
        scratch_shapes=[pltpu.VMEM((TM_FFN, D_FF), BF16)],
        compiler_params=_compiler_params(("parallel", "arbitrary")),
        name="ffn",
    )(x1, h2, h2, h2, wg, wu, wd, cw)


def _skew(vec, rows):
    h, n = vec.shape
    w = jnp.pad(vec, ((0, 0), (0, 1)))
    return jnp.tile(w, (1, rows))[:, :rows * n].reshape(h, rows, n)


def _bias_tables(rel_bias):
    table = rel_bias.astype(F32).T
    n = 4 * TQ - 1
    rel = jnp.arange(n) - (2 * TQ - 1)
    vec = table[:DIFF_HEADS][:, _rel_bucket(rel)]
    near = _skew(vec, TQ)[:, :, TQ - 1:TQ - 1 + 3 * TK]
    near = near.reshape(DIFF_HEADS, TQ, 3, TK).transpose(0, 2, 1, 3)
    far = lambda r: jnp.broadcast_to(table[:DIFF_HEADS, _rel_bucket(jnp.int32(r))][:, None, None, None],
                                     (DIFF_HEADS, 1, TQ, TK))
    diff = jnp.concatenate([far(-2 * TK), near, far(2 * TK)], axis=1)
    hi, lo = jnp.max(table[:DIFF_HEADS], axis=1), jnp.min(table[:DIFF_HEADS], axis=1)
    diff = (diff - (0.5 * (hi + lo))[:, None, None, None]) * LOG2E
    diff_half_range = jnp.max(0.5 * (hi - lo)) * LOG2E
    n = 6 * BLOCK - 1
    rel = jnp.arange(n) - (3 * BLOCK - 1)
    vec = jnp.where(jnp.abs(rel) <= WINDOW, table[DIFF_HEADS:][:, _rel_bucket(rel)], NEG_INF)
    band = _skew(vec, BLOCK)[:, :, BLOCK - 1:BLOCK - 1 + 5 * BLOCK]
    win = jnp.stack([band[:, :, (2 - v) * BLOCK:(2 - v) * BLOCK + WIN_KEYS] for v in range(3)])
    win = win.reshape(3, WIN_HEADS * BLOCK, WIN_KEYS) * LOG2E
    win_hi = jnp.max(table[DIFF_HEADS:], axis=1) * LOG2E
    win_lo = jnp.min(table[DIFF_HEADS:], axis=1) * LOG2E
    return dict(diff=diff, diff_half_range=diff_half_range, win=win, win_hi=win_hi, win_lo=win_lo)


def _qk_bound(q_gain, k_gain):
    return HEAD_DIM * NORM_SLACK * jnp.max(jnp.abs(q_gain)) * jnp.max(jnp.abs(k_gain))


def _is_bounded(half_range):
    return (half_range <= MAX_UNSHIFTED_LOGIT).astype(jnp.int32).reshape(1)


def _layer_params(l, tables, norm_attn_g, w_in, diff_q_norm_g, diff_k_norm_g, diff_lambda_q1, diff_lambda_k1,
                  diff_lambda_q2, diff_lambda_k2, diff_subln_g, win_q_norm_g, win_k_norm_g, win_sink,
                  w_out, norm_ffn_g, w_gate, w_up, conv_w, conv_b, w_down):
    scale = HEAD_DIM ** -0.5
    w = w_in[l]
    dq, dk, dv = w[:, :DIFF_W], w[:, DIFF_W:2 * DIFF_W], w[:, 2 * DIFF_W:3 * DIFF_W]
    wq = w[:, 3 * DIFF_W:3 * DIFF_W + WIN_Q_W].reshape(D_MODEL, WIN_KV_HEADS, WIN_GROUP, HEAD_DIM)
    wq = wq.transpose(0, 2, 1, 3).reshape(D_MODEL, WIN_Q_W)
    wk = w[:, 3 * DIFF_W + WIN_Q_W:3 * DIFF_W + WIN_Q_W + WIN_KV_W]
    wv = w[:, 3 * DIFF_W + WIN_Q_W + WIN_KV_W:]
    w_in_p = jnp.concatenate([dq, dk, wq, wk, dv, wv], axis=1).astype(BF16)
    gvec = jnp.concatenate([jnp.tile(diff_q_norm_g[l], 2 * DIFF_HEADS) * (scale * LOG2E),
                            jnp.tile(diff_k_norm_g[l], 2 * DIFF_HEADS),
                            jnp.tile(win_q_norm_g[l], WIN_HEADS) * (scale * LOG2E),
                            jnp.tile(win_k_norm_g[l], WIN_KV_HEADS)]).reshape(1, NORMED_W).astype(F32)
    lamv = jnp.zeros((8, LANES), F32)
    lamv = lamv.at[0:4, :HEAD_DIM].set(jnp.stack([diff_lambda_q1[l], diff_lambda_k1[l],
                                                  diff_lambda_q2[l], diff_lambda_k2[l]]).astype(F32))
    wo = w_out[l]
    wb = wo[DIFF_W:].reshape(WIN_KV_HEADS, WIN_GROUP, HEAD_DIM, D_MODEL)
    wb = wb.transpose(1, 0, 2, 3).reshape(WIN_Q_W, D_MODEL)
    cw = jnp.concatenate([conv_w[l], conv_b[l][None]], axis=0).astype(F32)
    return dict(
        g_attn=norm_attn_g[l].reshape(1, D_MODEL).astype(F32), w_in_p=w_in_p, gvec=gvec, lamv=lamv,
        subln_g=diff_subln_g[l].reshape(1, 2 * HEAD_DIM).astype(F32),
        wa=wo[:DIFF_W].astype(BF16), wb=wb.astype(BF16),
        g_ffn=norm_ffn_g[l].reshape(1, D_MODEL).astype(F32),
        wg=w_gate[l].astype(BF16), wu=w_up[l].astype(BF16), wd=w_down[l].astype(BF16), cw=cw,
        lam_init=0.8 - 0.6 * math.exp(-0.3 * l),
        diff_bounded=_is_bounded(_qk_bound(diff_q_norm_g[l].astype(F32) * (scale * LOG2E), diff_k_norm_g[l])
                                 + tables["diff_half_range"]),
        **_window_softmax_params(tables, win_q_norm_g[l].astype(F32) * (scale * LOG2E), win_k_norm_g[l],
                                 win_sink[l].astype(F32) * LOG2E),
    )


def _window_softmax_params(tables, q_gain, k_gain, sink):
    qk = _qk_bound(q_gain, k_gain)
    hi = jnp.maximum(tables["win_hi"] + qk, sink)
    lo = jnp.minimum(tables["win_lo"] - qk, sink)
    centre = 0.5 * (hi + lo)
    return dict(win_bias=tables["win"] - jnp.repeat(centre, BLOCK)[None, :, None], win_sink=sink - centre,
                win_bounded=_is_bounded(jnp.max(0.5 * (hi - lo))))


def _group_sum_matrix():
    idx = np.arange(MXU_DIM) // HEAD_DIM
    return jnp.asarray(idx[:, None] == idx[None, :], dtype=BF16)


def _encoder_layer(x, p, bias5, bd):
    batch, seq, _ = x.shape
    x2d = x.reshape(batch * seq, D_MODEL)
    dq, dk, wq, wk, dv, wv = _in_proj(x2d, p["g_attn"], p["w_in_p"], p["gvec"], bd)
    oa = _diff_attn(dq, dk, dv, p["diff_bounded"], p["lamv"], bias5, p["subln_g"], p["lam_init"], batch, seq)
    ob = _win_attn(wq, wk, wv, p["win_bounded"], p["win_bias"], p["win_sink"], batch, seq)
    x1, h2 = _out_proj(x2d, oa, ob, p["wa"], p["wb"], p["g_ffn"])
    y = _ffn(x1, h2, p["wg"], p["wu"], p["wd"], p["cw"], batch, seq)
    return y.reshape(batch, seq, D_MODEL)


def kernel(x_prompt, x_sample, norm_attn_g, w_in, diff_q_norm_g, diff_k_norm_g, diff_lambda_q1, diff_lambda_k1, diff_lambda_q2, diff_lambda_k2, diff_subln_g, win_q_norm_g, win_k_norm_g, win_sink, rel_bias, w_out, norm_ffn_g, w_gate, w_up, conv_w, conv_b, w_down):
    depth = w_in.shape[0]
    tables = _bias_tables(rel_bias)
    layers = [_layer_params(l, tables, norm_attn_g, w_in, diff_q_norm_g, diff_k_norm_g, diff_lambda_q1,
                            diff_lambda_k1, diff_lambda_q2, diff_lambda_k2, diff_subln_g, win_q_norm_g,
                            win_k_norm_g, win_sink, w_out, norm_ffn_g, w_gate, w_up, conv_w, conv_b, w_down)
              for l in range(depth)]
    bd = _group_sum_matrix()

    def run(x):
        for p in layers:
            x = _encoder_layer(x, p, tables["diff"], bd)
        return x

    return run(x_prompt), run(x_sample)
```

```python
import functools
import math

import jax
import jax.numpy as jnp
import numpy as np
from jax import lax
from jax.experimental import pallas as pl
from jax.experimental.pallas import tpu as pltpu

D_MODEL = 1024
SEQ = 2048
HEAD_DIM = 64
DIFF_HEADS = 4
WIN_HEADS = 8
WIN_KV_HEADS = 2
WIN_GROUP = WIN_HEADS // WIN_KV_HEADS
WINDOW = 128
BLOCK = 128
N_BUCKETS = 32
MAX_DISTANCE = 128
D_FF = 2816
EPS = 1e-6
NEG_INF = -1e30

DIFF_W = DIFF_HEADS * 2 * HEAD_DIM
WIN_Q_W = WIN_HEADS * HEAD_DIM
WIN_KV_W = WIN_KV_HEADS * HEAD_DIM
D_IN = 3 * DIFF_W + WIN_Q_W + 2 * WIN_KV_W
NORMED_W = 2 * DIFF_W + WIN_Q_W + WIN_KV_W

LANES = 128
MXU_DIM = 256
VMEM_LIMIT = 56 * 1024 * 1024

TM_PROJ = 1024
TQ = 256
TK = 256
TILES_PER_ITER = 8
STRIP = 16
LOG2E = math.log2(math.e)
MAX_UNSHIFTED_LOGIT = 100.0
NORM_SLACK = 1.02
TM_FFN = 1024
FF_CHUNK = 512
HALO = 16

BF16 = jnp.bfloat16
F32 = jnp.float32


def _rel_bucket(rel):
    nb = N_BUCKETS // 2
    max_exact = nb // 2
    ret = jnp.where(rel > 0, nb, 0)
    n = jnp.abs(rel)
    nf = jnp.maximum(n, 1).astype(F32)
    large = max_exact + (jnp.log(nf / max_exact) / math.log(MAX_DISTANCE / max_exact)
                         * (nb - max_exact)).astype(jnp.int32)
    large = jnp.minimum(large, nb - 1)
    return ret + jnp.where(n < max_exact, n, large)


def _compiler_params(semantics):
    return pltpu.CompilerParams(dimension_semantics=semantics, vmem_limit_bytes=VMEM_LIMIT)


def _resident(shape):
    nd = len(shape)
    return pl.BlockSpec(shape, lambda *_: (0,) * nd, pipeline_mode=pl.Buffered(1))


def _in_proj_kernel(x_ref, g_ref, w_ref, gv_ref, bd_ref,
                    dq_ref, dk_ref, wq_ref, wk_ref, dv_ref, wv_ref):
    x = x_ref[...]
    ms = jnp.mean(x * x, axis=-1, keepdims=True)
    h = (x * lax.rsqrt(ms + EPS) * g_ref[...]).astype(BF16)
    proj = jnp.dot(h, w_ref[...], preferred_element_type=F32)
    bd = bd_ref[...]

    def head_norm(c0, width):
        blk = proj[:, c0:c0 + width]
        sq = (blk * blk).astype(BF16)
        ss = jnp.dot(sq, bd[:width, :width], preferred_element_type=F32)
        return (blk * lax.rsqrt(ss * (1.0 / HEAD_DIM) + EPS) * gv_ref[:, c0:c0 + width]).astype(BF16)

    for ref, base in ((dq_ref, 0), (dk_ref, DIFF_W), (wq_ref, 2 * DIFF_W)):
        for c in range(0, DIFF_W, MXU_DIM):
            ref[:, c:c + MXU_DIM] = head_norm(base + c, MXU_DIM)
    wk_ref[...] = head_norm(2 * DIFF_W + WIN_Q_W, WIN_KV_W)
    dv_ref[...] = proj[:, NORMED_W:NORMED_W + DIFF_W].astype(BF16)
    wv_ref[...] = proj[:, NORMED_W + DIFF_W:].astype(BF16)


def _in_proj(x2d, g_attn, w_in_p, gvec, bd):
    t = x2d.shape[0]
    tile = lambda w: pl.BlockSpec((TM_PROJ, w), lambda i: (i, 0))
    widths = (DIFF_W, DIFF_W, WIN_Q_W, WIN_KV_W, DIFF_W, WIN_KV_W)
    return pl.pallas_call(
        _in_proj_kernel,
        grid=(t // TM_PROJ,),
        in_specs=[tile(D_MODEL), _resident((1, D_MODEL)), _resident((D_MODEL, D_IN)),
                  _resident((1, NORMED_W)), _resident((MXU_DIM, MXU_DIM))],
        out_specs=[tile(w) for w in widths],
        out_shape=[jax.ShapeDtypeStruct((t, w), BF16) for w in widths],
        compiler_params=_compiler_params(("parallel",)),
        name="in_proj",
    )(x2d, g_attn, w_in_p, gvec, bd)


def _diff_attn_kernel(lam_init, bounded_ref, lamv_ref, q_ref, k_ref, v_ref, bias_ref, g_ref, o_ref,
                      s_scr, p_scr, p_alt_scr, v_ones_scr):
    seq = k_ref.shape[0]
    nq, nk = seq // TQ, seq // TK
    lv = lamv_ref[...]
    lam = (jnp.exp(jnp.sum(lv[0:1] * lv[1:2], axis=-1, keepdims=True))
           - jnp.exp(jnp.sum(lv[2:3] * lv[3:4], axis=-1, keepdims=True)) + lam_init)
    lane = lax.broadcasted_iota(jnp.int32, (TQ, LANES), 1)
    v_ones_scr[:, :LANES] = v_ref[...]
    v_ones_scr[:, LANES:] = jnp.ones(v_ref.shape, BF16)

    def stacked_q(tile):
        q = q_ref[pl.ds(pl.multiple_of(tile * TQ, TQ), TQ), :]
        zero = jnp.zeros_like(q)
        return jnp.concatenate([jnp.where(lane < HEAD_DIM, q, zero), jnp.where(lane >= HEAD_DIM, q, zero)], axis=0)

    def key_logits(qs, kj):
        return lax.dot_general(qs, k_ref[kj * TK:(kj + 1) * TK, :], (((1,), (1,)), ((), ())),
                               preferred_element_type=F32)

    def finish(tile, p_ref):
        res = jnp.dot(p_ref[...], v_ones_scr[...], preferred_element_type=F32)
        out = res[:, :LANES] * (1.0 / res[:, LANES:])
        o = out[:TQ] - lam * out[TQ:]
        ms = jnp.mean(o * o, axis=-1, keepdims=True)
        o = o * lax.rsqrt(ms + EPS) * g_ref[...] * (1.0 - lam_init)
        o_ref[pl.ds(pl.multiple_of(tile * TQ, TQ), TQ), :] = o.astype(BF16)

    def bias_tiles(tile):
        return [jnp.clip(kj - tile, -2, 2) + 2 for kj in range(nk)]

    def unshifted_tile(tile, p_ref):
        qs, tiles = stacked_q(tile), bias_tiles(tile)
        for kj in range(nk):
            b = bias_ref[tiles[kj]]
            s = key_logits(qs, kj)
            p_ref[:TQ, kj * TK:(kj + 1) * TK] = jnp.exp2(s[:TQ] + b).astype(BF16)
            p_ref[TQ:, kj * TK:(kj + 1) * TK] = jnp.exp2(s[TQ:] + b).astype(BF16)
        finish(tile, p_ref)

    def unshifted_tiles(i, carry):
        for u in range(TILES_PER_ITER):
            unshifted_tile(i * TILES_PER_ITER + u, (p_scr, p_alt_scr)[u % 2])
        return carry

    def shifted_tile(tile, carry):
        qs, tiles = stacked_q(tile), bias_tiles(tile)
        for kj in range(nk):
            s_scr[:, kj * TK:(kj + 1) * TK] = key_logits(qs, kj)

        def strip(i, c):
            for mp in range(2):
                rows = pl.ds(pl.multiple_of(mp * TQ + i * STRIP, STRIP), STRIP)
                brows = pl.ds(pl.multiple_of(i * STRIP, STRIP), STRIP)
                t = jnp.concatenate([s_scr[rows, kj * TK:(kj + 1) * TK] + bias_ref[tiles[kj], brows, :]
                                     for kj in range(nk)], axis=1)
                p_scr[rows, :] = jnp.exp2(t - jnp.max(t, axis=-1, keepdims=True)).astype(BF16)
            return c

        lax.fori_loop(0, TQ // STRIP, strip, 0)
        finish(tile, p_scr)
        return carry

    bounded = bounded_ref[0] != 0

    @pl.when(bounded)
    def _():
        lax.fori_loop(0, nq // TILES_PER_ITER, unshifted_tiles, 0)

    @pl.when(jnp.logical_not(bounded))
    def _():
        lax.fori_loop(0, nq, shifted_tile, 0)


def _diff_attn(dq, dk, dv, bounded, lamv, bias5, subln_g, lam_init, batch, seq):
    seq_block = pl.BlockSpec((seq, LANES), lambda b, h: (b, h))
    return pl.pallas_call(
        functools.partial(_diff_attn_kernel, lam_init),
        grid=(batch, DIFF_HEADS),
        in_specs=[
            pl.BlockSpec(memory_space=pltpu.SMEM),
            pl.BlockSpec((8, LANES), lambda b, h: (0, 0)),
            seq_block, seq_block, seq_block,
            pl.BlockSpec((None, 5, TQ, TK), lambda b, h: (h, 0, 0, 0)),
            pl.BlockSpec((1, LANES), lambda b, h: (0, 0)),
        ],
        out_specs=seq_block,
        out_shape=jax.ShapeDtypeStruct((batch * seq, DIFF_W), BF16),
        scratch_shapes=[pltpu.VMEM((2 * TQ, seq), F32), pltpu.VMEM((2 * TQ, seq), BF16),
                        pltpu.VMEM((2 * TQ, seq), BF16), pltpu.VMEM((seq, 2 * LANES), BF16)],
        compiler_params=_compiler_params(("parallel", "parallel")),
        name="diff_attn",
    )(bounded, lamv, dq, dk, dv, bias5, subln_g)


WIN_KEYS = 3 * BLOCK
BLOCKS_PER_ITER = 4


def _win_attn_kernel(bounded_ref, sink_ref, q_ref, k_ref, v_ref, bias_ref, o_ref, s_scr, p_scr, v_ones_scr):
    seq = k_ref.shape[0]
    nb = seq // BLOCK
    group_rows = WIN_GROUP * BLOCK
    lane = lax.broadcasted_iota(jnp.int32, (BLOCK, LANES), 1)
    v_ones_scr[:, :LANES] = v_ref[...]
    v_ones_scr[:, LANES:] = jnp.ones(v_ref.shape, BF16)

    def window(n):
        start = pl.multiple_of(jnp.clip(n * BLOCK - BLOCK, 0, seq - WIN_KEYS), BLOCK)
        variant = jnp.where(n == 0, 0, jnp.where(n == nb - 1, 2, 1))
        return start, variant

    def grouped_q(n, kv):
        q = q_ref[pl.ds(pl.multiple_of(n * BLOCK, BLOCK), BLOCK), :]
        zero = jnp.zeros((BLOCK, LANES), q.dtype)
        keep = (lane < HEAD_DIM) if kv == 0 else (lane >= HEAD_DIM)
        return jnp.concatenate([jnp.where(keep, q[:, j * LANES:(j + 1) * LANES], zero)
                                for j in range(WIN_GROUP)], axis=0)

    def group_logits(n, kv, start):
        return lax.dot_general(grouped_q(n, kv), k_ref[pl.ds(start, WIN_KEYS), :], (((1,), (1,)), ((), ())),
                               preferred_element_type=F32)

    def store(n, outs):
        rows = pl.ds(pl.multiple_of(n * BLOCK, BLOCK), BLOCK)
        for j in range(WIN_GROUP):
            blk = jnp.where(lane < HEAD_DIM, outs[0][j * BLOCK:(j + 1) * BLOCK], outs[1][j * BLOCK:(j + 1) * BLOCK])
            o_ref[rows, j * LANES:(j + 1) * LANES] = blk.astype(BF16)

    def unshifted_block(n):
        start, variant = window(n)
        outs = []
        for kv in range(WIN_KV_HEADS):
            grows = slice(kv * group_rows, (kv + 1) * group_rows)
            p = jnp.exp2(group_logits(n, kv, start) + bias_ref[variant, grows, :]).astype(BF16)
            res = jnp.dot(p, v_ones_scr[pl.ds(start, WIN_KEYS), :], preferred_element_type=F32)
            sink_w = jnp.concatenate(
                [jnp.broadcast_to(jnp.exp2(jnp.full((1, LANES), sink_ref[kv * WIN_GROUP + g], F32)), (BLOCK, LANES))
                 for g in range(WIN_GROUP)], axis=0)
            outs.append(res[:, :LANES] * (1.0 / (res[:, LANES:] + sink_w)))
        store(n, outs)

    def unshifted_blocks(i, carry):
        for u in range(BLOCKS_PER_ITER):
            unshifted_block(i * BLOCKS_PER_ITER + u)
        return carry

    def shifted_block(n, carry):
        start, variant = window(n)
        outs = []
        for kv in range(WIN_KV_HEADS):
            s_scr[...] = group_logits(n, kv, start)
            sink_terms = []
            for r in range(0, group_rows, STRIP):
                rows = slice(r, r + STRIP)
                grows = slice(kv * group_rows + r, kv * group_rows + r + STRIP)
                t = s_scr[rows, :] + bias_ref[variant, grows, :]
                sink = sink_ref[(kv * group_rows + r) // BLOCK]
                row_max = jnp.max(t, axis=-1, keepdims=True)
                p_scr[rows, :] = jnp.exp2(t - jnp.maximum(row_max, sink)).astype(BF16)
                m_wide = jnp.maximum(jnp.broadcast_to(row_max, (STRIP, LANES)), sink)
                sink_terms.append(jnp.exp2(sink - m_wide))
            res = jnp.dot(p_scr[...], v_ones_scr[pl.ds(start, WIN_KEYS), :], preferred_element_type=F32)
            l = res[:, LANES:] + jnp.concatenate(sink_terms, axis=0)
            outs.append(res[:, :LANES] * (1.0 / l))
        store(n, outs)
        return carry

    bounded = bounded_ref[0] != 0

    @pl.when(bounded)
    def _():
        lax.fori_loop(0, nb // BLOCKS_PER_ITER, unshifted_blocks, 0)

    @pl.when(jnp.logical_not(bounded))
    def _():
        lax.fori_loop(0, nb, shifted_block, 0)


def _win_attn(wq, wk, wv, bounded, bias3, sink, batch, seq):
    rows = WIN_HEADS * BLOCK
    smem = pl.BlockSpec(memory_space=pltpu.SMEM)
    seq_block = lambda w: pl.BlockSpec((seq, w), lambda b: (b, 0))
    return pl.pallas_call(
        _win_attn_kernel,
        grid=(batch,),
        in_specs=[smem, smem, seq_block(WIN_Q_W), seq_block(WIN_KV_W), seq_block(WIN_KV_W),
                  _resident((3, rows, WIN_KEYS))],
        out_specs=seq_block(WIN_Q_W),
        out_shape=jax.ShapeDtypeStruct((batch * seq, WIN_Q_W), BF16),
        scratch_shapes=[pltpu.VMEM((rows // 2, WIN_KEYS), F32), pltpu.VMEM((rows // 2, WIN_KEYS), BF16),
                        pltpu.VMEM((seq, 2 * LANES), BF16)],
        compiler_params=_compiler_params(("parallel",)),
        name="win_attn",
    )(bounded, sink, wq, wk, wv, bias3)


def _mix_ffn_kernel(x_ref, xp_ref, xn_ref, oa_ref, oap_ref, oan_ref, ob_ref, obp_ref, obn_ref,
                    wa_ref, wb_ref, g_ref, wg_ref, wu_ref, wd_ref, cw_ref, o_ref, act_scr):
    i = pl.program_id(1)
    last = pl.num_programs(1) - 1
    ext_rows = TM_FFN + 2 * HALO
    tile = slice(HALO, HALO + TM_FFN)
    ext = lambda p, c, n: jnp.concatenate([p[...], c[...], n[...]], axis=0)
    x1 = (ext(xp_ref, x_ref, xn_ref)
          + jnp.dot(ext(oap_ref, oa_ref, oan_ref), wa_ref[...], preferred_element_type=F32)
          + jnp.dot(ext(obp_ref, ob_ref, obn_ref), wb_ref[...], preferred_element_type=F32))
    o_ref[...] = x1[tile]
    ms = jnp.mean(x1 * x1, axis=-1, keepdims=True)
    h_ext = (x1 * lax.rsqrt(ms + EPS) * g_ref[...]).astype(BF16)
    row = lax.broadcasted_iota(jnp.int32, (ext_rows, 1), 0)
    outside = ((row < HALO) & (i == 0)) | ((row >= HALO + TM_FFN) & (i == last))
    h_ext = jnp.where(outside, jnp.zeros_like(h_ext), h_ext)
    h = h_ext[tile]
    for c0 in range(0, D_FF, FF_CHUNK):
        cols = slice(c0, min(c0 + FF_CHUNK, D_FF))
        g_ext = jnp.dot(h_ext, wg_ref[:, cols], preferred_element_type=F32)
        g = g_ext[tile]
        g_dn = pltpu.roll(g_ext, 1, 0)[tile]
        g_up = pltpu.roll(g_ext, ext_rows - 1, 0)[tile]
        cw = cw_ref[:, cols]
        u = g_dn * cw[0:1] + g * cw[1:2] + g_up * cw[2:3] + cw[3:4]
        up = jnp.dot(h, wu_ref[:, cols], preferred_element_type=F32)
        act = (u * (1.0 / (1.0 + jnp.exp(-u)))) * up
        act_scr[:, cols] = act.astype(BF16)
    o_ref[...] = o_ref[...] + jnp.dot(act_scr[...], wd_ref[...], preferred_element_type=F32)


def _mix_ffn(x2d, oa, ob, wa, wb, g_ffn, wg, wu, wd, cw, batch, seq):
    nt = seq // TM_FFN
    hb = TM_FFN // HALO
    n_halo = batch * seq // HALO

    def with_halo(width):
        return [pl.BlockSpec((TM_FFN, width), lambda b, i: (b * nt + i, 0)),
                pl.BlockSpec((HALO, width), lambda b, i: (jnp.maximum((b * nt + i) * hb - 1, 0), 0)),
                pl.BlockSpec((HALO, width), lambda b, i: (jnp.minimum((b * nt + i + 1) * hb, n_halo - 1), 0))]

    return pl.pallas_call(
        _mix_ffn_kernel,
        grid=(batch, nt),
        in_specs=(with_halo(D_MODEL) + with_halo(DIFF_W) + with_halo(WIN_Q_W)
                  + [_resident((DIFF_W, D_MODEL)), _resident((WIN_Q_W, D_MODEL)), _resident((1, D_MODEL)),
                     _resident((D_MODEL, D_FF)), _resident((D_MODEL, D_FF)), _resident((D_FF, D_MODEL)),
                     _resident((4, D_FF))]),
        out_specs=pl.BlockSpec((TM_FFN, D_MODEL), lambda b, i: (b * nt + i, 0)),
        out_shape=jax.ShapeDtypeStruct((batch * seq, D_MODEL), F32),
        scratch_shapes=[pltpu.VMEM((TM_FFN, D_FF), BF16)],
        compiler_params=_compiler_params(("parallel", "arbitrary")),
        name="mix_ffn",
    )(x2d, x2d, x2d, oa, oa, oa, ob, ob, ob, wa, wb, g_ffn, wg, wu, wd, cw)


def _skew(vec, rows):
    h, n = vec.shape
    w = jnp.pad(vec, ((0, 0), (0, 1)))
    return jnp.tile(w, (1, rows))[:, :rows * n].reshape(h, rows, n)


def _bias_tables(rel_bias):
    table = rel_bias.astype(F32).T
    n = 4 * TQ - 1
    rel = jnp.arange(n) - (2 * TQ - 1)
    vec = table[:DIFF_HEADS][:, _rel_bucket(rel)]
    near = _skew(vec, TQ)[:, :, TQ - 1:TQ - 1 + 3 * TK]
    near = near.reshape(DIFF_HEADS, TQ, 3, TK).transpose(0, 2, 1, 3)
    far = lambda r: jnp.broadcast_to(table[:DIFF_HEADS, _rel_bucket(jnp.int32(r))][:, None, None, None],
                                     (DIFF_HEADS, 1, TQ, TK))
    diff = jnp.concatenate([far(-2 * TK), near, far(2 * TK)], axis=1)
    hi, lo = jnp.max(table[:DIFF_HEADS], axis=1), jnp.min(table[:DIFF_HEADS], axis=1)
    diff = (diff - (0.5 * (hi + lo))[:, None, None, None]) * LOG2E
    diff_half_range = jnp.max(0.5 * (hi - lo)) * LOG2E
    n = 6 * BLOCK - 1
    rel = jnp.arange(n) - (3 * BLOCK - 1)
    vec = jnp.where(jnp.abs(rel) <= WINDOW, table[DIFF_HEADS:][:, _rel_bucket(rel)], NEG_INF)
    band = _skew(vec, BLOCK)[:, :, BLOCK - 1:BLOCK - 1 + 5 * BLOCK]
    win = jnp.stack([band[:, :, (2 - v) * BLOCK:(2 - v) * BLOCK + WIN_KEYS] for v in range(3)])
    win = win.reshape(3, WIN_HEADS * BLOCK, WIN_KEYS) * LOG2E
    win_hi = jnp.max(table[DIFF_HEADS:], axis=1) * LOG2E
    win_lo = jnp.min(table[DIFF_HEADS:], axis=1) * LOG2E
    return dict(diff=diff, diff_half_range=diff_half_range, win=win, win_hi=win_hi, win_lo=win_lo)


def _qk_bound(q_gain, k_gain):
    return HEAD_DIM * NORM_SLACK * jnp.max(jnp.abs(q_gain)) * jnp.max(jnp.abs(k_gain))


def _is_bounded(half_range):
    return (half_range <= MAX_UNSHIFTED_LOGIT).astype(jnp.int32).reshape(1)


def _layer_params(l, tables, norm_attn_g, w_in, diff_q_norm_g, diff_k_norm_g, diff_lambda_q1, diff_lambda_k1,
                  diff_lambda_q2, diff_lambda_k2, diff_subln_g, win_q_norm_g, win_k_norm_g, win_sink,
                  w_out, norm_ffn_g, w_gate, w_up, conv_w, conv_b, w_down):
    scale = HEAD_DIM ** -0.5
    w = w_in[l]
    dq, dk, dv = w[:, :DIFF_W], w[:, DIFF_W:2 * DIFF_W], w[:, 2 * DIFF_W:3 * DIFF_W]
    wq = w[:, 3 * DIFF_W:3 * DIFF_W + WIN_Q_W].reshape(D_MODEL, WIN_KV_HEADS, WIN_GROUP, HEAD_DIM)
    wq = wq.transpose(0, 2, 1, 3).reshape(D_MODEL, WIN_Q_W)
    wk = w[:, 3 * DIFF_W + WIN_Q_W:3 * DIFF_W + WIN_Q_W + WIN_KV_W]
    wv = w[:, 3 * DIFF_W + WIN_Q_W + WIN_KV_W:]
    w_in_p = jnp.concatenate([dq, dk, wq, wk, dv, wv], axis=1).astype(BF16)
    gvec = jnp.concatenate([jnp.tile(diff_q_norm_g[l], 2 * DIFF_HEADS) * (scale * LOG2E),
                            jnp.tile(diff_k_norm_g[l], 2 * DIFF_HEADS),
                            jnp.tile(win_q_norm_g[l], WIN_HEADS) * (scale * LOG2E),
                            jnp.tile(win_k_norm_g[l], WIN_KV_HEADS)]).reshape(1, NORMED_W).astype(F32)
    lamv = jnp.zeros((8, LANES), F32)
    lamv = lamv.at[0:4, :HEAD_DIM].set(jnp.stack([diff_lambda_q1[l], diff_lambda_k1[l],
                                                  diff_lambda_q2[l], diff_lambda_k2[l]]).astype(F32))
    wo = w_out[l]
    wb = wo[DIFF_W:].reshape(WIN_KV_HEADS, WIN_GROUP, HEAD_DIM, D_MODEL)
    wb = wb.transpose(1, 0, 2, 3).reshape(WIN_Q_W, D_MODEL)
    cw = jnp.concatenate([conv_w[l], conv_b[l][None]], axis=0).astype(F32)
    return dict(
        g_attn=norm_attn_g[l].reshape(1, D_MODEL).astype(F32), w_in_p=w_in_p, gvec=gvec, lamv=lamv,
        subln_g=diff_subln_g[l].reshape(1, 2 * HEAD_DIM).astype(F32),
        wa=wo[:DIFF_W].astype(BF16), wb=wb.astype(BF16),
        g_ffn=norm_ffn_g[l].reshape(1, D_MODEL).astype(F32),
        wg=w_gate[l].astype(BF16), wu=w_up[l].astype(BF16), wd=w_down[l].astype(BF16), cw=cw,
        lam_init=0.8 - 0.6 * math.exp(-0.3 * l),
        diff_bounded=_is_bounded(_qk_bound(diff_q_norm_g[l].astype(F32) * (scale * LOG2E), diff_k_norm_g[l])
                                 + tables["diff_half_range"]),
        **_window_softmax_params(tables, win_q_norm_g[l].astype(F32) * (scale * LOG2E), win_k_norm_g[l],
                                 win_sink[l].astype(F32) * LOG2E),
    )


def _window_softmax_params(tables, q_gain, k_gain, sink):
    qk = _qk_bound(q_gain, k_gain)
    hi = jnp.maximum(tables["win_hi"] + qk, sink)
    lo = jnp.minimum(tables["win_lo"] - qk, sink)
    centre = 0.5 * (hi + lo)
    return dict(win_bias=tables["win"] - jnp.repeat(centre, BLOCK)[None, :, None], win_sink=sink - centre,
                win_bounded=_is_bounded(jnp.max(0.5 * (hi - lo))))


def _group_sum_matrix():
    idx = np.arange(MXU_DIM) // HEAD_DIM
    return jnp.asarray(idx[:, None] == idx[None, :], dtype=BF16)


def _encoder_layer(x, p, bias5, bd):
    batch, seq, _ = x.shape
    x2d = x.reshape(batch * seq, D_MODEL)
    dq, dk, wq, wk, dv, wv = _in_proj(x2d, p["g_attn"], p["w_in_p"], p["gvec"], bd)
    oa = _diff_attn(dq, dk, dv, p["diff_bounded"], p["lamv"], bias5, p["subln_g"], p["lam_init"], batch, seq)
    ob = _win_attn(wq, wk, wv, p["win_bounded"], p["win_bias"], p["win_sink"], batch, seq)
    y = _mix_ffn(x2d, oa, ob, p["wa"], p["wb"], p["g_ffn"], p["wg"], p["wu"], p["wd"], p["cw"], batch, seq)
    return y.reshape(batch, seq, D_MODEL)


def kernel(x_prompt, x_sample, norm_attn_g, w_in, diff_q_norm_g, diff_k_norm_g, diff_lambda_q1, diff_lambda_k1, diff_lambda_q2, diff_lambda_k2, diff_subln_g, win_q_norm_g, win_k_norm_g, win_sink, rel_bias, w_out, norm_ffn_g, w_gate, w_up, conv_w, conv_b, w_down):
    depth = w_in.shape[0]
    tables = _bias_tables(rel_bias)
    layers = [_layer_params(l, tables, norm_attn_g, w_in, diff_q_norm_g, diff_k_norm_g, diff_lambda_q1,
                            diff_lambda_k1, diff_lambda_q2, diff_lambda_k2, diff_subln_g, win_q_norm_g,
                            win_k_norm_g, win_sink, w_out, norm_ffn_g, w_gate, w_up, conv_w, conv_b, w_down)
              for l in range(depth)]
    bd = _group_sum_matrix()

    def run(x):
        for p in layers:
            x = _encoder_layer(x, p, tables["diff"], bd)
        return x

    return run(x_prompt), run(x_sample)
```

```python
import functools
import math

import jax
import jax.numpy as jnp
import numpy as np
from jax import lax
from jax.experimental import pallas as pl
from jax.experimental.pallas import tpu as pltpu

D_MODEL = 1024
SEQ = 2048
HEAD_DIM = 64
DIFF_HEADS = 4
WIN_HEADS = 8
WIN_KV_HEADS = 2
WIN_GROUP = WIN_HEADS // WIN_KV_HEADS
WINDOW = 128
BLOCK = 128
N_BUCKETS = 32
MAX_DISTANCE = 128
D_FF = 2816
EPS = 1e-6
NEG_INF = -1e30

DIFF_W = DIFF_HEADS * 2 * HEAD_DIM
WIN_Q_W = WIN_HEADS * HEAD_DIM
WIN_KV_W = WIN_KV_HEADS * HEAD_DIM
D_IN = 3 * DIFF_W + WIN_Q_W + 2 * WIN_KV_W
NORMED_W = 2 * DIFF_W + WIN_Q_W + WIN_KV_W

LANES = 128
MXU_DIM = 256
VMEM_LIMIT = 56 * 1024 * 1024

TM_PROJ = 1024
TQ = 256
TK = 256
TILES_PER_ITER = 8
STRIP = 16
LOG2E = math.log2(math.e)
MAX_UNSHIFTED_LOGIT = 100.0
NORM_SLACK = 1.02
TM_FFN = 1024
FF_CHUNK = 512
HALO = 16

BF16 = jnp.bfloat16
F32 = jnp.float32


def _rel_bucket(rel):
    nb = N_BUCKETS // 2
    max_exact = nb // 2
    ret = jnp.where(rel > 0, nb, 0)
    n = jnp.abs(rel)
    nf = jnp.maximum(n, 1).astype(F32)
    large = max_exact + (jnp.log(nf / max_exact) / math.log(MAX_DISTANCE / max_exact)
                         * (nb - max_exact)).astype(jnp.int32)
    large = jnp.minimum(large, nb - 1)
    return ret + jnp.where(n < max_exact, n, large)


def _compiler_params(semantics):
    return pltpu.CompilerParams(dimension_semantics=semantics, vmem_limit_bytes=VMEM_LIMIT)


def _resident(shape):
    nd = len(shape)
    return pl.BlockSpec(shape, lambda *_: (0,) * nd, pipeline_mode=pl.Buffered(1))


def _in_proj_kernel(x_ref, g_ref, w_ref, gv_ref, bd_ref,
                    dq_ref, dk_ref, wq_ref, wk_ref, dv_ref, wv_ref):
    x = x_ref[...]
    ms = jnp.mean(x * x, axis=-1, keepdims=True)
    h = (x * lax.rsqrt(ms + EPS) * g_ref[...]).astype(BF16)
    proj = jnp.dot(h, w_ref[...], preferred_element_type=F32)
    bd = bd_ref[...]

    def head_norm(c0, width):
        blk = proj[:, c0:c0 + width]
        sq = (blk * blk).astype(BF16)
        ss = jnp.dot(sq, bd[:width, :width], preferred_element_type=F32)
        return (blk * lax.rsqrt(ss * (1.0 / HEAD_DIM) + EPS) * gv_ref[:, c0:c0 + width]).astype(BF16)

    for ref, base in ((dq_ref, 0), (dk_ref, DIFF_W), (wq_ref, 2 * DIFF_W)):
        for c in range(0, DIFF_W, MXU_DIM):
            ref[:, c:c + MXU_DIM] = head_norm(base + c, MXU_DIM)
    wk_ref[...] = head_norm(2 * DIFF_W + WIN_Q_W, WIN_KV_W)
    dv_ref[...] = proj[:, NORMED_W:NORMED_W + DIFF_W].astype(BF16)
    wv_ref[...] = proj[:, NORMED_W + DIFF_W:].astype(BF16)


def _in_proj(x2d, g_attn, w_in_p, gvec, bd):
    t = x2d.shape[0]
    tile = lambda w: pl.BlockSpec((TM_PROJ, w), lambda i: (i, 0))
    widths = (DIFF_W, DIFF_W, WIN_Q_W, WIN_KV_W, DIFF_W, WIN_KV_W)
    return pl.pallas_call(
        _in_proj_kernel,
        grid=(t // TM_PROJ,),
        in_specs=[tile(D_MODEL), _resident((1, D_MODEL)), _resident((D_MODEL, D_IN)),
                  _resident((1, NORMED_W)), _resident((MXU_DIM, MXU_DIM))],
        out_specs=[tile(w) for w in widths],
        out_shape=[jax.ShapeDtypeStruct((t, w), BF16) for w in widths],
        compiler_params=_compiler_params(("parallel",)),
        name="in_proj",
    )(x2d, g_attn, w_in_p, gvec, bd)


def _diff_attn_kernel(lam_init, bounded_ref, lamv_ref, q_ref, k_ref, v_ref, bias_ref, bias_t_ref, g_ref,
                      g_col_ref, o_ref, s_scr, p_scr, v_ones_scr, vt_scr, pt_scr, pt_alt_scr, at_scr, at_alt_scr):
    seq = k_ref.shape[0]
    nq, nk = seq // TQ, seq // TK
    lv = lamv_ref[...]
    lam = (jnp.exp(jnp.sum(lv[0:1] * lv[1:2], axis=-1, keepdims=True))
           - jnp.exp(jnp.sum(lv[2:3] * lv[3:4], axis=-1, keepdims=True)) + lam_init)
    lane = lax.broadcasted_iota(jnp.int32, (TQ, LANES), 1)
    reload_offset = jnp.maximum(bounded_ref[0], 1) - 1

    def stacked_q(tile):
        q = q_ref[pl.ds(pl.multiple_of(tile * TQ, TQ), TQ), :]
        zero = jnp.zeros_like(q)
        return jnp.concatenate([jnp.where(lane < HEAD_DIM, q, zero), jnp.where(lane >= HEAD_DIM, q, zero)], axis=0)

    def key_logits(qs, kj):
        return lax.dot_general(qs, k_ref[kj * TK:(kj + 1) * TK, :], (((1,), (1,)), ((), ())),
                               preferred_element_type=F32)

    def finish(tile, p_ref):
        res = jnp.dot(p_ref[...], v_ones_scr[...], preferred_element_type=F32)
        out = res[:, :LANES] * (1.0 / res[:, LANES:])
        o = out[:TQ] - lam * out[TQ:]
        ms = jnp.mean(o * o, axis=-1, keepdims=True)
        o = o * lax.rsqrt(ms + EPS) * g_ref[...] * (1.0 - lam_init)
        o_ref[pl.ds(pl.multiple_of(tile * TQ, TQ), TQ), :] = o.astype(BF16)

    def bias_tiles(tile):
        return [jnp.clip(kj - tile, -2, 2) + 2 for kj in range(nk)]

    def unshifted_tile(tile, pt_ref, at_ref):
        qs, tiles = stacked_q(tile), bias_tiles(tile)
        key_sums = jnp.zeros((8, 2 * TQ), F32)
        for kj in range(nk):
            rows = slice(kj * TK, (kj + 1) * TK)
            bt = bias_t_ref[tiles[kj]]
            st = lax.dot_general(k_ref[rows, :], qs, (((1,), (1,)), ((), ())),
                                 preferred_element_type=F32)
            pt = jnp.exp2(st + jnp.concatenate([bt, bt], axis=1))
            pt_ref[rows, :] = pt
            key_sums = key_sums + jnp.sum(pt.reshape(TK // 8, 8, 2 * TQ), axis=0)
        r = 1.0 / jnp.sum(key_sums, axis=0, keepdims=True)
        r1, r2 = r[:, :TQ], lam * r[:, TQ:]
        for kj in range(nk):
            rows = slice(kj * TK, (kj + 1) * TK)
            back = pl.ds(pl.multiple_of(kj * TK + reload_offset, TK), TK)
            at_ref[rows, :] = (pt_ref[back, :TQ] * r1 - pt_ref[back, TQ:] * r2).astype(BF16)
        ot = jnp.dot(vt_scr[...], at_ref[...], preferred_element_type=F32)
        ms = jnp.mean(ot * ot, axis=0, keepdims=True)
        gcol = jnp.concatenate([g_col_ref[...]] * (TQ // LANES), axis=1)
        ot = ot * lax.rsqrt(ms + EPS) * gcol * (1.0 - lam_init)
        o_ref[pl.ds(pl.multiple_of(tile * TQ, TQ), TQ), :] = ot.T.astype(BF16)

    def unshifted_tiles(i, carry):
        for u in range(TILES_PER_ITER):
            unshifted_tile(i * TILES_PER_ITER + u, (pt_scr, pt_alt_scr)[u % 2], (at_scr, at_alt_scr)[u % 2])
        return carry

    def shifted_tile(tile, carry):
        qs, tiles = stacked_q(tile), bias_tiles(tile)
        for kj in range(nk):
            s_scr[:, kj * TK:(kj + 1) * TK] = key_logits(qs, kj)

        def strip(i, c):
            for mp in range(2):
                rows = pl.ds(pl.multiple_of(mp * TQ + i * STRIP, STRIP), STRIP)
                brows = pl.ds(pl.multiple_of(i * STRIP, STRIP), STRIP)
                t = jnp.concatenate([s_scr[rows, kj * TK:(kj + 1) * TK] + bias_ref[tiles[kj], brows, :]
                                     for kj in range(nk)], axis=1)
                p_scr[rows, :] = jnp.exp2(t - jnp.max(t, axis=-1, keepdims=True)).astype(BF16)
            return c

        lax.fori_loop(0, TQ // STRIP, strip, 0)
        finish(tile, p_scr)
        return carry

    bounded = bounded_ref[0] != 0

    @pl.when(bounded)
    def _():
        vt_scr[...] = v_ref[...].astype(F32).T.astype(BF16)
        lax.fori_loop(0, nq // TILES_PER_ITER, unshifted_tiles, 0)

    @pl.when(jnp.logical_not(bounded))
    def _():
        v_ones_scr[:, :LANES] = v_ref[...]
        v_ones_scr[:, LANES:] = jnp.ones(v_ref.shape, BF16)
        lax.fori_loop(0, nq, shifted_tile, 0)


def _diff_attn(dq, dk, dv, bounded, lamv, bias5, subln_g, lam_init, batch, seq):
    seq_block = pl.BlockSpec((seq, LANES), lambda b, h: (b, h))
    bias_block = pl.BlockSpec((None, 5, TQ, TK), lambda b, h: (h, 0, 0, 0))
    g_col = jnp.broadcast_to(subln_g.reshape(LANES, 1), (LANES, LANES))
    return pl.pallas_call(
        functools.partial(_diff_attn_kernel, lam_init),
        grid=(batch, DIFF_HEADS),
        in_specs=[
            pl.BlockSpec(memory_space=pltpu.SMEM),
            pl.BlockSpec((8, LANES), lambda b, h: (0, 0)),
            seq_block, seq_block, seq_block,
            bias_block, bias_block,
            pl.BlockSpec((1, LANES), lambda b, h: (0, 0)),
            pl.BlockSpec((LANES, LANES), lambda b, h: (0, 0)),
        ],
        out_specs=seq_block,
        out_shape=jax.ShapeDtypeStruct((batch * seq, DIFF_W), BF16),
        scratch_shapes=[pltpu.VMEM((2 * TQ, seq), F32), pltpu.VMEM((2 * TQ, seq), BF16),
                        pltpu.VMEM((seq, 2 * LANES), BF16), pltpu.VMEM((LANES, seq), BF16),
                        pltpu.VMEM((seq, 2 * TQ), F32), pltpu.VMEM((seq, 2 * TQ), F32),
                        pltpu.VMEM((seq, TQ), BF16), pltpu.VMEM((seq, TQ), BF16)],
        compiler_params=_compiler_params(("parallel", "parallel")),
        name="diff_attn",
    )(bounded, lamv, dq, dk, dv, bias5, jnp.swapaxes(bias5, 2, 3), subln_g, g_col)


WIN_KEYS = 3 * BLOCK
BLOCKS_PER_ITER = 4


def _win_attn_kernel(bounded_ref, sink_ref, q_ref, k_ref, v_ref, bias_ref, o_ref, s_scr, p_scr, v_ones_scr):
    seq = k_ref.shape[0]
    nb = seq // BLOCK
    group_rows = WIN_GROUP * BLOCK
    lane = lax.broadcasted_iota(jnp.int32, (BLOCK, LANES), 1)
    v_ones_scr[:, :LANES] = v_ref[...]
    v_ones_scr[:, LANES:] = jnp.ones(v_ref.shape, BF16)

    def window(n):
        start = pl.multiple_of(jnp.clip(n * BLOCK - BLOCK, 0, seq - WIN_KEYS), BLOCK)
        variant = jnp.where(n == 0, 0, jnp.where(n == nb - 1, 2, 1))
        return start, variant

    def grouped_q(n, kv):
        q = q_ref[pl.ds(pl.multiple_of(n * BLOCK, BLOCK), BLOCK), :]
        zero = jnp.zeros((BLOCK, LANES), q.dtype)
        keep = (lane < HEAD_DIM) if kv == 0 else (lane >= HEAD_DIM)
        return jnp.concatenate([jnp.where(keep, q[:, j * LANES:(j + 1) * LANES], zero)
                                for j in range(WIN_GROUP)], axis=0)

    def group_logits(n, kv, start):
        return lax.dot_general(grouped_q(n, kv), k_ref[pl.ds(start, WIN_KEYS), :], (((1,), (1,)), ((), ())),
                               preferred_element_type=F32)

    def store(n, outs):
        rows = pl.ds(pl.multiple_of(n * BLOCK, BLOCK), BLOCK)
        for j in range(WIN_GROUP):
            blk = jnp.where(lane < HEAD_DIM, outs[0][j * BLOCK:(j + 1) * BLOCK], outs[1][j * BLOCK:(j + 1) * BLOCK])
            o_ref[rows, j * LANES:(j + 1) * LANES] = blk.astype(BF16)

    def unshifted_block(n):
        start, variant = window(n)
        outs = []
        for kv in range(WIN_KV_HEADS):
            grows = slice(kv * group_rows, (kv + 1) * group_rows)
            p = jnp.exp2(group_logits(n, kv, start) + bias_ref[variant, grows, :]).astype(BF16)
            res = jnp.dot(p, v_ones_scr[pl.ds(start, WIN_KEYS), :], preferred_element_type=F32)
            sink_w = jnp.concatenate(
                [jnp.broadcast_to(jnp.exp2(jnp.full((1, LANES), sink_ref[kv * WIN_GROUP + g], F32)), (BLOCK, LANES))
                 for g in range(WIN_GROUP)], axis=0)
            outs.append(res[:, :LANES] * (1.0 / (res[:, LANES:] + sink_w)))
        store(n, outs)

    def unshifted_blocks(i, carry):
        for u in range(BLOCKS_PER_ITER):
            unshifted_block(i * BLOCKS_PER_ITER + u)
        return carry

    def shifted_block(n, carry):
        start, variant = window(n)
        outs = []
        for kv in range(WIN_KV_HEADS):
            s_scr[...] = group_logits(n, kv, start)
            sink_terms = []
            for r in range(0, group_rows, STRIP):
                rows = slice(r, r + STRIP)
                grows = slice(kv * group_rows + r, kv * group_rows + r + STRIP)
                t = s_scr[rows, :] + bias_ref[variant, grows, :]
                sink = sink_ref[(kv * group_rows + r) // BLOCK]
                row_max = jnp.max(t, axis=-1, keepdims=True)
                p_scr[rows, :] = jnp.exp2(t - jnp.maximum(row_max, sink)).astype(BF16)
                m_wide = jnp.maximum(jnp.broadcast_to(row_max, (STRIP, LANES)), sink)
                sink_terms.append(jnp.exp2(sink - m_wide))
            res = jnp.dot(p_scr[...], v_ones_scr[pl.ds(start, WIN_KEYS), :], preferred_element_type=F32)
            l = res[:, LANES:] + jnp.concatenate(sink_terms, axis=0)
            outs.append(res[:, :LANES] * (1.0 / l))
        store(n, outs)
        return carry

    bounded = bounded_ref[0] != 0

    @pl.when(bounded)
    def _():
        lax.fori_loop(0, nb // BLOCKS_PER_ITER, unshifted_blocks, 0)

    @pl.when(jnp.logical_not(bounded))
    def _():
        lax.fori_loop(0, nb, shifted_block, 0)


def _win_attn(wq, wk, wv, bounded, bias3, sink, batch, seq):
    rows = WIN_HEADS * BLOCK
    smem = pl.BlockSpec(memory_space=pltpu.SMEM)
    seq_block = lambda w: pl.BlockSpec((seq, w), lambda b: (b, 0))
    return pl.pallas_call(
        _win_attn_kernel,
        grid=(batch,),
        in_specs=[smem, smem, seq_block(WIN_Q_W), seq_block(WIN_KV_W), seq_block(WIN_KV_W),
                  _resident((3, rows, WIN_KEYS))],
        out_specs=seq_block(WIN_Q_W),
        out_shape=jax.ShapeDtypeStruct((batch * seq, WIN_Q_W), BF16),
        scratch_shapes=[pltpu.VMEM((rows // 2, WIN_KEYS), F32), pltpu.VMEM((rows // 2, WIN_KEYS), BF16),
                        pltpu.VMEM((seq, 2 * LANES), BF16)],
        compiler_params=_compiler_params(("parallel",)),
        name="win_attn",
    )(bounded, sink, wq, wk, wv, bias3)


def _mix_ffn_kernel(x_ref, xp_ref, xn_ref, oa_ref, oap_ref, oan_ref, ob_ref, obp_ref, obn_ref,
                    wa_ref, wb_ref, g_ref, wg_ref, wu_ref, wd_ref, cw_ref, o_ref, act_scr):
    i = pl.program_id(1)
    last = pl.num_programs(1) - 1
    ext_rows = TM_FFN + 2 * HALO
    tile = slice(HALO, HALO + TM_FFN)
    ext = lambda p, c, n: jnp.concatenate([p[...], c[...], n[...]], axis=0)
    x1 = (ext(xp_ref, x_ref, xn_ref)
          + jnp.dot(ext(oap_ref, oa_ref, oan_ref), wa_ref[...], preferred_element_type=F32)
          + jnp.dot(ext(obp_ref, ob_ref, obn_ref), wb_ref[...], preferred_element_type=F32))
    o_ref[...] = x1[tile]
    ms = jnp.mean(x1 * x1, axis=-1, keepdims=True)
    h_ext = (x1 * lax.rsqrt(ms + EPS) * g_ref[...]).astype(BF16)
    row = lax.broadcasted_iota(jnp.int32, (ext_rows, 1), 0)
    outside = ((row < HALO) & (i == 0)) | ((row >= HALO + TM_FFN) & (i == last))
    h_ext = jnp.where(outside, jnp.zeros_like(h_ext), h_ext)
    h = h_ext[tile]
    for c0 in range(0, D_FF, FF_CHUNK):
        cols = slice(c0, min(c0 + FF_CHUNK, D_FF))
        g_ext = jnp.dot(h_ext, wg_ref[:, cols], preferred_element_type=F32)
        g = g_ext[tile]
        g_dn = pltpu.roll(g_ext, 1, 0)[tile]
        g_up = pltpu.roll(g_ext, ext_rows - 1, 0)[tile]
        cw = cw_ref[:, cols]
        u = g_dn * cw[0:1] + g * cw[1:2] + g_up * cw[2:3] + cw[3:4]
        up = jnp.dot(h, wu_ref[:, cols], preferred_element_type=F32)
        act = (u * (1.0 / (1.0 + jnp.exp(-u)))) * up
        act_scr[:, cols] = act.astype(BF16)
    o_ref[...] = o_ref[...] + jnp.dot(act_scr[...], wd_ref[...], preferred_element_type=F32)


def _mix_ffn(x2d, oa, ob, wa, wb, g_ffn, wg, wu, wd, cw, batch, seq):
    nt = seq // TM_FFN
    hb = TM_FFN // HALO
    n_halo = batch * seq // HALO

    def with_halo(width):
        return [pl.BlockSpec((TM_FFN, width), lambda b, i: (b * nt + i, 0)),
                pl.BlockSpec((HALO, width), lambda b, i: (jnp.maximum((b * nt + i) * hb - 1, 0), 0)),
                pl.BlockSpec((HALO, width), lambda b, i: (jnp.minimum((b * nt + i + 1) * hb, n_halo - 1), 0))]

    return pl.pallas_call(
        _mix_ffn_kernel,
        grid=(batch, nt),
        in_specs=(with_halo(D_MODEL) + with_halo(DIFF_W) + with_halo(WIN_Q_W)
                  + [_resident((DIFF_W, D_MODEL)), _resident((WIN_Q_W, D_MODEL)), _resident((1, D_MODEL)),
                     _resident((D_MODEL, D_FF)), _resident((D_MODEL, D_FF)), _resident((D_FF, D_MODEL)),
                     _resident((4, D_FF))]),
        out_specs=pl.BlockSpec((TM_FFN, D_MODEL), lambda b, i: (b * nt + i, 0)),
        out_shape=jax.ShapeDtypeStruct((batch * seq, D_MODEL), F32),
        scratch_shapes=[pltpu.VMEM((TM_FFN, D_FF), BF16)],
        compiler_params=_compiler_params(("parallel", "arbitrary")),
        name="mix_ffn",
    )(x2d, x2d, x2d, oa, oa, oa, ob, ob, ob, wa, wb, g_ffn, wg, wu, wd, cw)


def _skew(vec, rows):
    h, n = vec.shape
    w = jnp.pad(vec, ((0, 0), (0, 1)))
    return jnp.tile(w, (1, rows))[:, :rows * n].reshape(h, rows, n)


def _bias_tables(rel_bias):
    table = rel_bias.astype(F32).T
    n = 4 * TQ - 1
    rel = jnp.arange(n) - (2 * TQ - 1)
    vec = table[:DIFF_HEADS][:, _rel_bucket(rel)]
    near = _skew(vec, TQ)[:, :, TQ - 1:TQ - 1 + 3 * TK]
    near = near.reshape(DIFF_HEADS, TQ, 3, TK).transpose(0, 2, 1, 3)
    far = lambda r: jnp.broadcast_to(table[:DIFF_HEADS, _rel_bucket(jnp.int32(r))][:, None, None, None],
                                     (DIFF_HEADS, 1, TQ, TK))
    diff = jnp.concatenate([far(-2 * TK), near, far(2 * TK)], axis=1)
    hi, lo = jnp.max(table[:DIFF_HEADS], axis=1), jnp.min(table[:DIFF_HEADS], axis=1)
    diff = (diff - (0.5 * (hi + lo))[:, None, None, None]) * LOG2E
    diff_half_range = jnp.max(0.5 * (hi - lo)) * LOG2E
    n = 6 * BLOCK - 1
    rel = jnp.arange(n) - (3 * BLOCK - 1)
    vec = jnp.where(jnp.abs(rel) <= WINDOW, table[DIFF_HEADS:][:, _rel_bucket(rel)], NEG_INF)
    band = _skew(vec, BLOCK)[:, :, BLOCK - 1:BLOCK - 1 + 5 * BLOCK]
    win = jnp.stack([band[:, :, (2 - v) * BLOCK:(2 - v) * BLOCK + WIN_KEYS] for v in range(3)])
    win = win.reshape(3, WIN_HEADS * BLOCK, WIN_KEYS) * LOG2E
    win_hi = jnp.max(table[DIFF_HEADS:], axis=1) * LOG2E
    win_lo = jnp.min(table[DIFF_HEADS:], axis=1) * LOG2E
    return dict(diff=diff, diff_half_range=diff_half_range, win=win, win_hi=win_hi, win_lo=win_lo)


def _qk_bound(q_gain, k_gain):
    return HEAD_DIM * NORM_SLACK * jnp.max(jnp.abs(q_gain)) * jnp.max(jnp.abs(k_gain))


def _is_bounded(half_range):
    return (half_range <= MAX_UNSHIFTED_LOGIT).astype(jnp.int32).reshape(1)


def _layer_params(l, tables, norm_attn_g, w_in, diff_q_norm_g, diff_k_norm_g, diff_lambda_q1, diff_lambda_k1,
                  diff_lambda_q2, diff_lambda_k2, diff_subln_g, win_q_norm_g, win_k_norm_g, win_sink,
                  w_out, norm_ffn_g, w_gate, w_up, conv_w, conv_b, w_down):
    scale = HEAD_DIM ** -0.5
    w = w_in[l]
    dq, dk, dv = w[:, :DIFF_W], w[:, DIFF_W:2 * DIFF_W], w[:, 2 * DIFF_W:3 * DIFF_W]
    wq = w[:, 3 * DIFF_W:3 * DIFF_W + WIN_Q_W].reshape(D_MODEL, WIN_KV_HEADS, WIN_GROUP, HEAD_DIM)
    wq = wq.transpose(0, 2, 1, 3).reshape(D_MODEL, WIN_Q_W)
    wk = w[:, 3 * DIFF_W + WIN_Q_W:3 * DIFF_W + WIN_Q_W + WIN_KV_W]
    wv = w[:, 3 * DIFF_W + WIN_Q_W + WIN_KV_W:]
    w_in_p = jnp.concatenate([dq, dk, wq, wk, dv, wv], axis=1).astype(BF16)
    gvec = jnp.concatenate([jnp.tile(diff_q_norm_g[l], 2 * DIFF_HEADS) * (scale * LOG2E),
                            jnp.tile(diff_k_norm_g[l], 2 * DIFF_HEADS),
                            jnp.tile(win_q_norm_g[l], WIN_HEADS) * (scale * LOG2E),
                            jnp.tile(win_k_norm_g[l], WIN_KV_HEADS)]).reshape(1, NORMED_W).astype(F32)
    lamv = jnp.zeros((8, LANES), F32)
    lamv = lamv.at[0:4, :HEAD_DIM].set(jnp.stack([diff_lambda_q1[l], diff_lambda_k1[l],
                                                  diff_lambda_q2[l], diff_lambda_k2[l]]).astype(F32))
    wo = w_out[l]
    wb = wo[DIFF_W:].reshape(WIN_KV_HEADS, WIN_GROUP, HEAD_DIM, D_MODEL)
    wb = wb.transpose(1, 0, 2, 3).reshape(WIN_Q_W, D_MODEL)
    cw = jnp.concatenate([conv_w[l], conv_b[l][None]], axis=0).astype(F32)
    return dict(
        g_attn=norm_attn_g[l].reshape(1, D_MODEL).astype(F32), w_in_p=w_in_p, gvec=gvec, lamv=lamv,
        subln_g=diff_subln_g[l].reshape(1, 2 * HEAD_DIM).astype(F32),
        wa=wo[:DIFF_W].astype(BF16), wb=wb.astype(BF16),
        g_ffn=norm_ffn_g[l].reshape(1, D_MODEL).astype(F32),
        wg=w_gate[l].astype(BF16), wu=w_up[l].astype(BF16), wd=w_down[l].astype(BF16), cw=cw,
        lam_init=0.8 - 0.6 * math.exp(-0.3 * l),
        diff_bounded=_is_bounded(_qk_bound(diff_q_norm_g[l].astype(F32) * (scale * LOG2E), diff_k_norm_g[l])
                                 + tables["diff_half_range"]),
        **_window_softmax_params(tables, win_q_norm_g[l].astype(F32) * (scale * LOG2E), win_k_norm_g[l],
                                 win_sink[l].astype(F32) * LOG2E),
    )


def _window_softmax_params(tables, q_gain, k_gain, sink):
    qk = _qk_bound(q_gain, k_gain)
    hi = jnp.maximum(tables["win_hi"] + qk, sink)
    lo = jnp.minimum(tables["win_lo"] - qk, sink)
    centre = 0.5 * (hi + lo)
    return dict(win_bias=tables["win"] - jnp.repeat(centre, BLOCK)[None, :, None], win_sink=sink - centre,
                win_bounded=_is_bounded(jnp.max(0.5 * (hi - lo))))


def _group_sum_matrix():
    idx = np.arange(MXU_DIM) // HEAD_DIM
    return jnp.asarray(idx[:, None] == idx[None, :], dtype=BF16)


def _encoder_layer(x, p, bias5, bd):
    batch, seq, _ = x.shape
    x2d = x.reshape(batch * seq, D_MODEL)
    dq, dk, wq, wk, dv, wv = _in_proj(x2d, p["g_attn"], p["w_in_p"], p["gvec"], bd)
    oa = _diff_attn(dq, dk, dv, p["diff_bounded"], p["lamv"], bias5, p["subln_g"], p["lam_init"], batch, seq)
    ob = _win_attn(wq, wk, wv, p["win_bounded"], p["win_bias"], p["win_sink"], batch, seq)
    y = _mix_ffn(x2d, oa, ob, p["wa"], p["wb"], p["g_ffn"], p["wg"], p["wu"], p["wd"], p["cw"], batch, seq)
    return y.reshape(batch, seq, D_MODEL)


def kernel(x_prompt, x_sample, norm_attn_g, w_in, diff_q_norm_g, diff_k_norm_g, diff_lambda_q1, diff_lambda_k1, diff_lambda_q2, diff_lambda_k2, diff_subln_g, win_q_norm_g, win_k_norm_g, win_sink, rel_bias, w_out, norm_ffn_g, w_gate, w_up, conv_w, conv_b, w_down):
    depth = w_in.shape[0]
    tables = _bias_tables(rel_bias)
    layers = [_layer_params(l, tables, norm_attn_g, w_in, diff_q_norm_g, diff_k_norm_g, diff_lambda_q1,
                            diff_lambda_k1, diff_lambda_q2, diff_lambda_k2, diff_subln_g, win_q_norm_g,
                            win_k_norm_g, win_sink, w_out, norm_ffn_g, w_gate, w_up, conv_w, conv_b, w_down)
              for l in range(depth)]
    bd = _group_sum_matrix()

    def run(x):
        for p in layers:
            x = _encoder_layer(x, p, tables["diff"], bd)
        return x

    return run(x_prompt), run(x_sample)
```

```python
import functools
import math

import jax
import jax.numpy as jnp
from jax import lax
from jax.experimental import pallas as pl
from jax.experimental.pallas import tpu as pltpu

D_MODEL = 1024
SEQ = 2048
HEAD_DIM = 64
DIFF_HEADS = 4
WIN_HEADS = 8
WIN_KV_HEADS = 2
WIN_GROUP = WIN_HEADS // WIN_KV_HEADS
WINDOW = 128
BLOCK = 128
N_BUCKETS = 32
MAX_DISTANCE = 128
D_FF = 2816
EPS = 1e-6
NEG_INF = -1e30

DIFF_W = DIFF_HEADS * 2 * HEAD_DIM
WIN_Q_W = WIN_HEADS * HEAD_DIM
WIN_KV_W = WIN_KV_HEADS * HEAD_DIM
D_IN = 3 * DIFF_W + WIN_Q_W + 2 * WIN_KV_W
NORMED_W = 2 * DIFF_W + WIN_Q_W + WIN_KV_W

LANES = 128
VMEM_LIMIT = 56 * 1024 * 1024

TM_PROJ = 1024
TQ = 256
TK = 256
TILES_PER_ITER = 8
STRIP = 16
LOG2E = math.log2(math.e)
MAX_UNSHIFTED_LOGIT = 100.0
NORM_SLACK = 1.02
TM_FFN = 1024
FF_CHUNK = 512
HALO = 16

BF16 = jnp.bfloat16
F32 = jnp.float32


def _rel_bucket(rel):
    nb = N_BUCKETS // 2
    max_exact = nb // 2
    ret = jnp.where(rel > 0, nb, 0)
    n = jnp.abs(rel)
    nf = jnp.maximum(n, 1).astype(F32)
    large = max_exact + (jnp.log(nf / max_exact) / math.log(MAX_DISTANCE / max_exact)
                         * (nb - max_exact)).astype(jnp.int32)
    large = jnp.minimum(large, nb - 1)
    return ret + jnp.where(n < max_exact, n, large)


def _compiler_params(semantics):
    return pltpu.CompilerParams(dimension_semantics=semantics, vmem_limit_bytes=VMEM_LIMIT)


def _resident(shape):
    nd = len(shape)
    return pl.BlockSpec(shape, lambda *_: (0,) * nd, pipeline_mode=pl.Buffered(1))


def _in_proj_kernel(x_ref, g_ref, w_ref, gv_ref, dq_ref, dk_ref, wq_ref, wk_ref, dv_ref, wv_ref):
    x = x_ref[...]
    ms = jnp.mean(x * x, axis=-1, keepdims=True)
    h = (x * lax.rsqrt(ms + EPS) * g_ref[...]).astype(BF16)
    proj = jnp.dot(h, w_ref[...], preferred_element_type=F32)
    first_head = lax.broadcasted_iota(jnp.int32, (TM_PROJ, LANES), 1) < HEAD_DIM

    def head_norm(c0):
        blk = proj[:, c0:c0 + LANES]
        sq = blk * blk
        lo = jnp.sum(jnp.where(first_head, sq, 0.0), axis=-1, keepdims=True)
        hi = jnp.sum(sq, axis=-1, keepdims=True) - lo
        ss = jnp.where(first_head, lo, hi)
        return (blk * lax.rsqrt(ss * (1.0 / HEAD_DIM) + EPS) * gv_ref[:, c0:c0 + LANES]).astype(BF16)

    for ref, base, width in ((dq_ref, 0, DIFF_W), (dk_ref, DIFF_W, DIFF_W), (wq_ref, 2 * DIFF_W, WIN_Q_W),
                             (wk_ref, 2 * DIFF_W + WIN_Q_W, WIN_KV_W)):
        for c in range(0, width, LANES):
            ref[:, c:c + LANES] = head_norm(base + c)
    dv_ref[...] = proj[:, NORMED_W:NORMED_W + DIFF_W].astype(BF16)
    wv_ref[...] = proj[:, NORMED_W + DIFF_W:].astype(BF16)


def _in_proj(x2d, g_attn, w_in_p, gvec):
    t = x2d.shape[0]
    tile = lambda w: pl.BlockSpec((TM_PROJ, w), lambda i: (i, 0))
    widths = (DIFF_W, DIFF_W, WIN_Q_W, WIN_KV_W, DIFF_W, WIN_KV_W)
    return pl.pallas_call(
        _in_proj_kernel,
        grid=(t // TM_PROJ,),
        in_specs=[tile(D_MODEL), _resident((1, D_MODEL)), _resident((D_MODEL, D_IN)),
                  _resident((1, NORMED_W))],
        out_specs=[tile(w) for w in widths],
        out_shape=[jax.ShapeDtypeStruct((t, w), BF16) for w in widths],
        compiler_params=_compiler_params(("parallel",)),
        name="in_proj",
    )(x2d, g_attn, w_in_p, gvec)


def _diff_attn_kernel(lam_init, bounded_ref, lamv_ref, q_ref, k_ref, v_ref, bias_ref, bias_t_ref, g_ref,
                      g_col_ref, o_ref, s_scr, p_scr, v_ones_scr, vt_scr, pt_scr, pt_alt_scr, at_scr, at_alt_scr):
    seq = k_ref.shape[0]
    nq, nk = seq // TQ, seq // TK
    lv = lamv_ref[...]
    lam = (jnp.exp(jnp.sum(lv[0:1] * lv[1:2], axis=-1, keepdims=True))
           - jnp.exp(jnp.sum(lv[2:3] * lv[3:4], axis=-1, keepdims=True)) + lam_init)
    lane = lax.broadcasted_iota(jnp.int32, (TQ, LANES), 1)
    reload_offset = jnp.maximum(bounded_ref[0], 1) - 1

    def stacked_q(tile):
        q = q_ref[pl.ds(pl.multiple_of(tile * TQ, TQ), TQ), :]
        zero = jnp.zeros_like(q)
        return jnp.concatenate([jnp.where(lane < HEAD_DIM, q, zero), jnp.where(lane >= HEAD_DIM, q, zero)], axis=0)

    def key_logits(qs, kj):
        return lax.dot_general(qs, k_ref[kj * TK:(kj + 1) * TK, :], (((1,), (1,)), ((), ())),
                               preferred_element_type=F32)

    def finish(tile, p_ref):
        res = jnp.dot(p_ref[...], v_ones_scr[...], preferred_element_type=F32)
        out = res[:, :LANES] * (1.0 / res[:, LANES:])
        o = out[:TQ] - lam * out[TQ:]
        ms = jnp.mean(o * o, axis=-1, keepdims=True)
        o = o * lax.rsqrt(ms + EPS) * g_ref[...] * (1.0 - lam_init)
        o_ref[pl.ds(pl.multiple_of(tile * TQ, TQ), TQ), :] = o.astype(BF16)

    def bias_tiles(tile):
        return [jnp.clip(kj - tile, -2, 2) + 2 for kj in range(nk)]

    def unshifted_tile(tile, pt_ref, at_ref):
        qs, tiles = stacked_q(tile), bias_tiles(tile)
        key_sums = jnp.zeros((8, 2 * TQ), F32)
        for kj in range(nk):
            rows = slice(kj * TK, (kj + 1) * TK)
            bt = bias_t_ref[tiles[kj]]
            st = lax.dot_general(k_ref[rows, :], qs, (((1,), (1,)), ((), ())),
                                 preferred_element_type=F32)
            pt = jnp.exp2(st + jnp.concatenate([bt, bt], axis=1))
            pt_ref[rows, :] = pt
            key_sums = key_sums + jnp.sum(pt.reshape(TK // 8, 8, 2 * TQ), axis=0)
        r = 1.0 / jnp.sum(key_sums, axis=0, keepdims=True)
        r1, r2 = r[:, :TQ], lam * r[:, TQ:]
        for kj in range(nk):
            rows = slice(kj * TK, (kj + 1) * TK)
            back = pl.ds(pl.multiple_of(kj * TK + reload_offset, TK), TK)
            at_ref[rows, :] = (pt_ref[back, :TQ] * r1 - pt_ref[back, TQ:] * r2).astype(BF16)
        ot = jnp.dot(vt_scr[...], at_ref[...], preferred_element_type=F32)
        ms = jnp.mean(ot * ot, axis=0, keepdims=True)
        gcol = jnp.concatenate([g_col_ref[...]] * (TQ // LANES), axis=1)
        ot = ot * lax.rsqrt(ms + EPS) * gcol * (1.0 - lam_init)
        o_ref[pl.ds(pl.multiple_of(tile * TQ, TQ), TQ), :] = ot.T.astype(BF16)

    def unshifted_tiles(i, carry):
        for u in range(TILES_PER_ITER):
            unshifted_tile(i * TILES_PER_ITER + u, (pt_scr, pt_alt_scr)[u % 2], (at_scr, at_alt_scr)[u % 2])
        return carry

    def shifted_tile(tile, carry):
        qs, tiles = stacked_q(tile), bias_tiles(tile)
        for kj in range(nk):
            s_scr[:, kj * TK:(kj + 1) * TK] = key_logits(qs, kj)

        def strip(i, c):
            for mp in range(2):
                rows = pl.ds(pl.multiple_of(mp * TQ + i * STRIP, STRIP), STRIP)
                brows = pl.ds(pl.multiple_of(i * STRIP, STRIP), STRIP)
                t = jnp.concatenate([s_scr[rows, kj * TK:(kj + 1) * TK] + bias_ref[tiles[kj], brows, :]
                                     for kj in range(nk)], axis=1)
                p_scr[rows, :] = jnp.exp2(t - jnp.max(t, axis=-1, keepdims=True)).astype(BF16)
            return c

        lax.fori_loop(0, TQ // STRIP, strip, 0)
        finish(tile, p_scr)
        return carry

    bounded = bounded_ref[0] != 0

    @pl.when(bounded)
    def _():
        vt_scr[...] = v_ref[...].astype(F32).T.astype(BF16)
        lax.fori_loop(0, nq // TILES_PER_ITER, unshifted_tiles, 0)

    @pl.when(jnp.logical_not(bounded))
    def _():
        v_ones_scr[:, :LANES] = v_ref[...]
        v_ones_scr[:, LANES:] = jnp.ones(v_ref.shape, BF16)
        lax.fori_loop(0, nq, shifted_tile, 0)


def _diff_attn(dq, dk, dv, bounded, lamv, bias5, subln_g, lam_init, batch, seq):
    seq_block = pl.BlockSpec((seq, LANES), lambda b, h: (b, h))
    bias_block = pl.BlockSpec((None, 5, TQ, TK), lambda b, h: (h, 0, 0, 0))
    g_col = jnp.broadcast_to(subln_g.reshape(LANES, 1), (LANES, LANES))
    return pl.pallas_call(
        functools.partial(_diff_attn_kernel, lam_init),
        grid=(batch, DIFF_HEADS),
        in_specs=[
            pl.BlockSpec(memory_space=pltpu.SMEM),
            pl.BlockSpec((8, LANES), lambda b, h: (0, 0)),
            seq_block, seq_block, seq_block,
            bias_block, bias_block,
            pl.BlockSpec((1, LANES), lambda b, h: (0, 0)),
            pl.BlockSpec((LANES, LANES), lambda b, h: (0, 0)),
        ],
        out_specs=seq_block,
        out_shape=jax.ShapeDtypeStruct((batch * seq, DIFF_W), BF16),
        scratch_shapes=[pltpu.VMEM((2 * TQ, seq), F32), pltpu.VMEM((2 * TQ, seq), BF16),
                        pltpu.VMEM((seq, 2 * LANES), BF16), pltpu.VMEM((LANES, seq), BF16),
                        pltpu.VMEM((seq, 2 * TQ), F32), pltpu.VMEM((seq, 2 * TQ), F32),
                        pltpu.VMEM((seq, TQ), BF16), pltpu.VMEM((seq, TQ), BF16)],
        compiler_params=_compiler_params(("parallel", "parallel")),
        name="diff_attn",
    )(bounded, lamv, dq, dk, dv, bias5, jnp.swapaxes(bias5, 2, 3), subln_g, g_col)


WIN_KEYS = 3 * BLOCK
BLOCKS_PER_ITER = 4


def _win_attn_kernel(bounded_ref, sink_ref, q_ref, k_ref, v_ref, bias_ref, o_ref, s_scr, p_scr, v_ones_scr):
    seq = k_ref.shape[0]
    nb = seq // BLOCK
    group_rows = WIN_GROUP * BLOCK
    lane = lax.broadcasted_iota(jnp.int32, (BLOCK, LANES), 1)
    v_ones_scr[:, :LANES] = v_ref[...]
    v_ones_scr[:, LANES:] = jnp.ones(v_ref.shape, BF16)

    def window(n):
        start = pl.multiple_of(jnp.clip(n * BLOCK - BLOCK, 0, seq - WIN_KEYS), BLOCK)
        variant = jnp.where(n == 0, 0, jnp.where(n == nb - 1, 2, 1))
        return start, variant

    def grouped_q(n, kv):
        q = q_ref[pl.ds(pl.multiple_of(n * BLOCK, BLOCK), BLOCK), :]
        zero = jnp.zeros((BLOCK, LANES), q.dtype)
        keep = (lane < HEAD_DIM) if kv == 0 else (lane >= HEAD_DIM)
        return jnp.concatenate([jnp.where(keep, q[:, j * LANES:(j + 1) * LANES], zero)
                                for j in range(WIN_GROUP)], axis=0)

    def group_logits(n, kv, start):
        return lax.dot_general(grouped_q(n, kv), k_ref[pl.ds(start, WIN_KEYS), :], (((1,), (1,)), ((), ())),
                               preferred_element_type=F32)

    def store(n, outs):
        rows = pl.ds(pl.multiple_of(n * BLOCK, BLOCK), BLOCK)
        for j in range(WIN_GROUP):
            blk = jnp.where(lane < HEAD_DIM, outs[0][j * BLOCK:(j + 1) * BLOCK], outs[1][j * BLOCK:(j + 1) * BLOCK])
            o_ref[rows, j * LANES:(j + 1) * LANES] = blk.astype(BF16)

    def unshifted_block(n):
        start, variant = window(n)
        outs = []
        for kv in range(WIN_KV_HEADS):
            grows = slice(kv * group_rows, (kv + 1) * group_rows)
            p = jnp.exp2(group_logits(n, kv, start) + bias_ref[variant, grows, :]).astype(BF16)
            res = jnp.dot(p, v_ones_scr[pl.ds(start, WIN_KEYS), :], preferred_element_type=F32)
            sink_w = jnp.concatenate(
                [jnp.broadcast_to(jnp.exp2(jnp.full((1, LANES), sink_ref[kv * WIN_GROUP + g], F32)), (BLOCK, LANES))
                 for g in range(WIN_GROUP)], axis=0)
            outs.append(res[:, :LANES] * (1.0 / (res[:, LANES:] + sink_w)))
        store(n, outs)

    def unshifted_blocks(i, carry):
        for u in range(BLOCKS_PER_ITER):
            unshifted_block(i * BLOCKS_PER_ITER + u)
        return carry

    def shifted_block(n, carry):
        start, variant = window(n)
        outs = []
        for kv in range(WIN_KV_HEADS):
            s_scr[...] = group_logits(n, kv, start)
            sink_terms = []
            for r in range(0, group_rows, STRIP):
                rows = slice(r, r + STRIP)
                grows = slice(kv * group_rows + r, kv * group_rows + r + STRIP)
                t = s_scr[rows, :] + bias_ref[variant, grows, :]
                sink = sink_ref[(kv * group_rows + r) // BLOCK]
                row_max = jnp.max(t, axis=-1, keepdims=True)
                p_scr[rows, :] = jnp.exp2(t - jnp.maximum(row_max, sink)).astype(BF16)
                m_wide = jnp.maximum(jnp.broadcast_to(row_max, (STRIP, LANES)), sink)
                sink_terms.append(jnp.exp2(sink - m_wide))
            res = jnp.dot(p_scr[...], v_ones_scr[pl.ds(start, WIN_KEYS), :], preferred_element_type=F32)
            l = res[:, LANES:] + jnp.concatenate(sink_terms, axis=0)
            outs.append(res[:, :LANES] * (1.0 / l))
        store(n, outs)
        return carry

    bounded = bounded_ref[0] != 0

    @pl.when(bounded)
    def _():
        lax.fori_loop(0, nb // BLOCKS_PER_ITER, unshifted_blocks, 0)

    @pl.when(jnp.logical_not(bounded))
    def _():
        lax.fori_loop(0, nb, shifted_block, 0)


def _win_attn(wq, wk, wv, bounded, bias3, sink, batch, seq):
    rows = WIN_HEADS * BLOCK
    smem = pl.BlockSpec(memory_space=pltpu.SMEM)
    seq_block = lambda w: pl.BlockSpec((seq, w), lambda b: (b, 0))
    return pl.pallas_call(
        _win_attn_kernel,
        grid=(batch,),
        in_specs=[smem, smem, seq_block(WIN_Q_W), seq_block(WIN_KV_W), seq_block(WIN_KV_W),
                  _resident((3, rows, WIN_KEYS))],
        out_specs=seq_block(WIN_Q_W),
        out_shape=jax.ShapeDtypeStruct((batch * seq, WIN_Q_W), BF16),
        scratch_shapes=[pltpu.VMEM((rows // 2, WIN_KEYS), F32), pltpu.VMEM((rows // 2, WIN_KEYS), BF16),
                        pltpu.VMEM((seq, 2 * LANES), BF16)],
        compiler_params=_compiler_params(("parallel",)),
        name="win_attn",
    )(bounded, sink, wq, wk, wv, bias3)


def _mix_ffn_kernel(x_ref, xp_ref, xn_ref, oa_ref, oap_ref, oan_ref, ob_ref, obp_ref, obn_ref,
                    wa_ref, wb_ref, g_ref, wg_ref, wu_ref, wd_ref, cw_ref, o_ref, act_scr):
    i = pl.program_id(1)
    last = pl.num_programs(1) - 1
    ext_rows = TM_FFN + 2 * HALO
    tile = slice(HALO, HALO + TM_FFN)
    ext = lambda p, c, n: jnp.concatenate([p[...], c[...], n[...]], axis=0)
    x1 = (ext(xp_ref, x_ref, xn_ref)
          + jnp.dot(ext(oap_ref, oa_ref, oan_ref), wa_ref[...], preferred_element_type=F32)
          + jnp.dot(ext(obp_ref, ob_ref, obn_ref), wb_ref[...], preferred_element_type=F32))
    o_ref[...] = x1[tile]
    ms = jnp.mean(x1 * x1, axis=-1, keepdims=True)
    h_ext = (x1 * lax.rsqrt(ms + EPS) * g_ref[...]).astype(BF16)
    row = lax.broadcasted_iota(jnp.int32, (ext_rows, 1), 0)
    outside = ((row < HALO) & (i == 0)) | ((row >= HALO + TM_FFN) & (i == last))
    h_ext = jnp.where(outside, jnp.zeros_like(h_ext), h_ext)
    h = h_ext[tile]
    for c0 in range(0, D_FF, FF_CHUNK):
        cols = slice(c0, min(c0 + FF_CHUNK, D_FF))
        g_ext = jnp.dot(h_ext, wg_ref[:, cols], preferred_element_type=F32)
        g = g_ext[tile]
        g_dn = pltpu.roll(g_ext, 1, 0)[tile]
        g_up = pltpu.roll(g_ext, ext_rows - 1, 0)[tile]
        cw = cw_ref[:, cols]
        u = g_dn * cw[0:1] + g * cw[1:2] + g_up * cw[2:3] + cw[3:4]
        up = jnp.dot(h, wu_ref[:, cols], preferred_element_type=F32)
        act = (u * (1.0 / (1.0 + jnp.exp(-u)))) * up
        act_scr[:, cols] = act.astype(BF16)
    o_ref[...] = o_ref[...] + jnp.dot(act_scr[...], wd_ref[...], preferred_element_type=F32)


def _mix_ffn(x2d, oa, ob, wa, wb, g_ffn, wg, wu, wd, cw, batch, seq):
    nt = seq // TM_FFN
    hb = TM_FFN // HALO
    n_halo = batch * seq // HALO

    def with_halo(width):
        return [pl.BlockSpec((TM_FFN, width), lambda b, i: (b * nt + i, 0)),
                pl.BlockSpec((HALO, width), lambda b, i: (jnp.maximum((b * nt + i) * hb - 1, 0), 0)),
                pl.BlockSpec((HALO, width), lambda b, i: (jnp.minimum((b * nt + i + 1) * hb, n_halo - 1), 0))]

    return pl.pallas_call(
        _mix_ffn_kernel,
        grid=(batch, nt),
        in_specs=(with_halo(D_MODEL) + with_halo(DIFF_W) + with_halo(WIN_Q_W)
                  + [_resident((DIFF_W, D_MODEL)), _resident((WIN_Q_W, D_MODEL)), _resident((1, D_MODEL)),
                     _resident((D_MODEL, D_FF)), _resident((D_MODEL, D_FF)), _resident((D_FF, D_MODEL)),
                     _resident((4, D_FF))]),
        out_specs=pl.BlockSpec((TM_FFN, D_MODEL), lambda b, i: (b * nt + i, 0)),
        out_shape=jax.ShapeDtypeStruct((batch * seq, D_MODEL), F32),
        scratch_shapes=[pltpu.VMEM((TM_FFN, D_FF), BF16)],
        compiler_params=_compiler_params(("parallel", "arbitrary")),
        name="mix_ffn",
    )(x2d, x2d, x2d, oa, oa, oa, ob, ob, ob, wa, wb, g_ffn, wg, wu, wd, cw)


def _skew(vec, rows):
    h, n = vec.shape
    w = jnp.pad(vec, ((0, 0), (0, 1)))
    return jnp.tile(w, (1, rows))[:, :rows * n].reshape(h, rows, n)


def _bias_tables(rel_bias):
    table = rel_bias.astype(F32).T
    n = 4 * TQ - 1
    rel = jnp.arange(n) - (2 * TQ - 1)
    vec = table[:DIFF_HEADS][:, _rel_bucket(rel)]
    near = _skew(vec, TQ)[:, :, TQ - 1:TQ - 1 + 3 * TK]
    near = near.reshape(DIFF_HEADS, TQ, 3, TK).transpose(0, 2, 1, 3)
    far = lambda r: jnp.broadcast_to(table[:DIFF_HEADS, _rel_bucket(jnp.int32(r))][:, None, None, None],
                                     (DIFF_HEADS, 1, TQ, TK))
    diff = jnp.concatenate([far(-2 * TK), near, far(2 * TK)], axis=1)
    hi, lo = jnp.max(table[:DIFF_HEADS], axis=1), jnp.min(table[:DIFF_HEADS], axis=1)
    diff = (diff - (0.5 * (hi + lo))[:, None, None, None]) * LOG2E
    diff_half_range = jnp.max(0.5 * (hi - lo)) * LOG2E
    n = 6 * BLOCK - 1
    rel = jnp.arange(n) - (3 * BLOCK - 1)
    vec = jnp.where(jnp.abs(rel) <= WINDOW, table[DIFF_HEADS:][:, _rel_bucket(rel)], NEG_INF)
    band = _skew(vec, BLOCK)[:, :, BLOCK - 1:BLOCK - 1 + 5 * BLOCK]
    win = jnp.stack([band[:, :, (2 - v) * BLOCK:(2 - v) * BLOCK + WIN_KEYS] for v in range(3)])
    win = win.reshape(3, WIN_HEADS * BLOCK, WIN_KEYS) * LOG2E
    win_hi = jnp.max(table[DIFF_HEADS:], axis=1) * LOG2E
    win_lo = jnp.min(table[DIFF_HEADS:], axis=1) * LOG2E
    return dict(diff=diff, diff_half_range=diff_half_range, win=win, win_hi=win_hi, win_lo=win_lo)


def _qk_bound(q_gain, k_gain):
    return HEAD_DIM * NORM_SLACK * jnp.max(jnp.abs(q_gain)) * jnp.max(jnp.abs(k_gain))


def _is_bounded(half_range):
    return (half_range <= MAX_UNSHIFTED_LOGIT).astype(jnp.int32).reshape(1)


def _layer_params(l, tables, norm_attn_g, w_in, diff_q_norm_g, diff_k_norm_g, diff_lambda_q1, diff_lambda_k1,
                  diff_lambda_q2, diff_lambda_k2, diff_subln_g, win_q_norm_g, win_k_norm_g, win_sink,
                  w_out, norm_ffn_g, w_gate, w_up, conv_w, conv_b, w_down):
    scale = HEAD_DIM ** -0.5
    w = w_in[l]
    dq, dk, dv = w[:, :DIFF_W], w[:, DIFF_W:2 * DIFF_W], w[:, 2 * DIFF_W:3 * DIFF_W]
    wq = w[:, 3 * DIFF_W:3 * DIFF_W + WIN_Q_W].reshape(D_MODEL, WIN_KV_HEADS, WIN_GROUP, HEAD_DIM)
    wq = wq.transpose(0, 2, 1, 3).reshape(D_MODEL, WIN_Q_W)
    wk = w[:, 3 * DIFF_W + WIN_Q_W:3 * DIFF_W + WIN_Q_W + WIN_KV_W]
    wv = w[:, 3 * DIFF_W + WIN_Q_W + WIN_KV_W:]
    w_in_p = jnp.concatenate([dq, dk, wq, wk, dv, wv], axis=1).astype(BF16)
    gvec = jnp.concatenate([jnp.tile(diff_q_norm_g[l], 2 * DIFF_HEADS) * (scale * LOG2E),
                            jnp.tile(diff_k_norm_g[l], 2 * DIFF_HEADS),
                            jnp.tile(win_q_norm_g[l], WIN_HEADS) * (scale * LOG2E),
                            jnp.tile(win_k_norm_g[l], WIN_KV_HEADS)]).reshape(1, NORMED_W).astype(F32)
    lamv = jnp.zeros((8, LANES), F32)
    lamv = lamv.at[0:4, :HEAD_DIM].set(jnp.stack([diff_lambda_q1[l], diff_lambda_k1[l],
                                                  diff_lambda_q2[l], diff_lambda_k2[l]]).astype(F32))
    wo = w_out[l]
    wb = wo[DIFF_W:].reshape(WIN_KV_HEADS, WIN_GROUP, HEAD_DIM, D_MODEL)
    wb = wb.transpose(1, 0, 2, 3).reshape(WIN_Q_W, D_MODEL)
    cw = jnp.concatenate([conv_w[l], conv_b[l][None]], axis=0).astype(F32)
    return dict(
        g_attn=norm_attn_g[l].reshape(1, D_MODEL).astype(F32), w_in_p=w_in_p, gvec=gvec, lamv=lamv,
        subln_g=diff_subln_g[l].reshape(1, 2 * HEAD_DIM).astype(F32),
        wa=wo[:DIFF_W].astype(BF16), wb=wb.astype(BF16),
        g_ffn=norm_ffn_g[l].reshape(1, D_MODEL).astype(F32),
        wg=w_gate[l].astype(BF16), wu=w_up[l].astype(BF16), wd=w_down[l].astype(BF16), cw=cw,
        lam_init=0.8 - 0.6 * math.exp(-0.3 * l),
        diff_bounded=_is_bounded(_qk_bound(diff_q_norm_g[l].astype(F32) * (scale * LOG2E), diff_k_norm_g[l])
                                 + tables["diff_half_range"]),
        **_window_softmax_params(tables, win_q_norm_g[l].astype(F32) * (scale * LOG2E), win_k_norm_g[l],
                                 win_sink[l].astype(F32) * LOG2E),
    )


def _window_softmax_params(tables, q_gain, k_gain, sink):
    qk = _qk_bound(q_gain, k_gain)
    hi = jnp.maximum(tables["win_hi"] + qk, sink)
    lo = jnp.minimum(tables["win_lo"] - qk, sink)
    centre = 0.5 * (hi + lo)
    return dict(win_bias=tables["win"] - jnp.repeat(centre, BLOCK)[None, :, None], win_sink=sink - centre,
                win_bounded=_is_bounded(jnp.max(0.5 * (hi - lo))))


def _encoder_layer(x, p, bias5):
    batch, seq, _ = x.shape
    x2d = x.reshape(batch * seq, D_MODEL)
    dq, dk, wq, wk, dv, wv = _in_proj(x2d, p["g_attn"], p["w_in_p"], p["gvec"])
    oa = _diff_attn(dq, dk, dv, p["diff_bounded"], p["lamv"], bias5, p["subln_g"], p["lam_init"], batch, seq)
    ob = _win_attn(wq, wk, wv, p["win_bounded"], p["win_bias"], p["win_sink"], batch, seq)
    y = _mix_ffn(x2d, oa, ob, p["wa"], p["wb"], p["g_ffn"], p["wg"], p["wu"], p["wd"], p["cw"], batch, seq)
    return y.reshape(batch, seq, D_MODEL)


def kernel(x_prompt, x_sample, norm_attn_g, w_in, diff_q_norm_g, diff_k_norm_g, diff_lambda_q1, diff_lambda_k1, diff_lambda_q2, diff_lambda_k2, diff_subln_g, win_q_norm_g, win_k_norm_g, win_sink, rel_bias, w_out, norm_ffn_g, w_gate, w_up, conv_w, conv_b, w_down):
    depth = w_in.shape[0]
    tables = _bias_tables(rel_bias)
    layers = [_layer_params(l, tables, norm_attn_g, w_in, diff_q_norm_g, diff_k_norm_g, diff_lambda_q1,
                            diff_lambda_k1, diff_lambda_q2, diff_lambda_k2, diff_subln_g, win_q_norm_g,
                            win_k_norm_g, win_sink, w_out, norm_ffn_g, w_gate, w_up, conv_w, conv_b, w_down)
              for l in range(depth)]

    def run(x):
        for p in layers:
            x = _encoder_layer(x, p, tables["diff"])
        return x

    return run(x_prompt), run(x_sample)
```

```python
import functools
import math

import jax
import jax.numpy as jnp
from jax import lax
from jax.experimental import pallas as pl
from jax.experimental.pallas import tpu as pltpu

D_MODEL = 1024
SEQ = 2048
HEAD_DIM = 64
DIFF_HEADS = 4
WIN_HEADS = 8
WIN_KV_HEADS = 2
WIN_GROUP = WIN_HEADS // WIN_KV_HEADS
WINDOW = 128
BLOCK = 128
N_BUCKETS = 32
MAX_DISTANCE = 128
D_FF = 2816
EPS = 1e-6
NEG_INF = -1e30

DIFF_W = DIFF_HEADS * 2 * HEAD_DIM
WIN_Q_W = WIN_HEADS * HEAD_DIM
WIN_KV_W = WIN_KV_HEADS * HEAD_DIM
D_IN = 3 * DIFF_W + WIN_Q_W + 2 * WIN_KV_W
NORMED_W = 2 * DIFF_W + WIN_Q_W + WIN_KV_W

LANES = 128
VMEM_LIMIT = 56 * 1024 * 1024

TM_PROJ = 1024
TQ = 256
TK = 256
STRIP = 16
LOG2E = math.log2(math.e)
MAX_UNSHIFTED_LOGIT = 100.0
NORM_SLACK = 1.02
TM_FFN = 1024
FF_CHUNK = 512
HALO = 16

BF16 = jnp.bfloat16
F32 = jnp.float32


def _rel_bucket(rel):
    nb = N_BUCKETS // 2
    max_exact = nb // 2
    ret = jnp.where(rel > 0, nb, 0)
    n = jnp.abs(rel)
    nf = jnp.maximum(n, 1).astype(F32)
    large = max_exact + (jnp.log(nf / max_exact) / math.log(MAX_DISTANCE / max_exact)
                         * (nb - max_exact)).astype(jnp.int32)
    large = jnp.minimum(large, nb - 1)
    return ret + jnp.where(n < max_exact, n, large)


def _compiler_params(semantics):
    return pltpu.CompilerParams(dimension_semantics=semantics, vmem_limit_bytes=VMEM_LIMIT)


def _resident(shape):
    nd = len(shape)
    return pl.BlockSpec(shape, lambda *_: (0,) * nd, pipeline_mode=pl.Buffered(1))


def _in_proj_kernel(x_ref, g_ref, w_ref, gv_ref, dq_ref, dk_ref, wq_ref, wk_ref, dv_ref, wv_ref):
    x = x_ref[...]
    ms = jnp.mean(x * x, axis=-1, keepdims=True)
    h = (x * lax.rsqrt(ms + EPS) * g_ref[...]).astype(BF16)
    proj = jnp.dot(h, w_ref[...], preferred_element_type=F32)
    first_head = lax.broadcasted_iota(jnp.int32, (TM_PROJ, LANES), 1) < HEAD_DIM

    def head_norm(c0):
        blk = proj[:, c0:c0 + LANES]
        sq = blk * blk
        lo = jnp.sum(jnp.where(first_head, sq, 0.0), axis=-1, keepdims=True)
        hi = jnp.sum(sq, axis=-1, keepdims=True) - lo
        ss = jnp.where(first_head, lo, hi)
        return (blk * lax.rsqrt(ss * (1.0 / HEAD_DIM) + EPS) * gv_ref[:, c0:c0 + LANES]).astype(BF16)

    for ref, base, width in ((dq_ref, 0, DIFF_W), (dk_ref, DIFF_W, DIFF_W), (wq_ref, 2 * DIFF_W, WIN_Q_W),
                             (wk_ref, 2 * DIFF_W + WIN_Q_W, WIN_KV_W)):
        for c in range(0, width, LANES):
            ref[:, c:c + LANES] = head_norm(base + c)
    dv_ref[...] = proj[:, NORMED_W:NORMED_W + DIFF_W].astype(BF16)
    wv_ref[...] = proj[:, NORMED_W + DIFF_W:].astype(BF16)


def _in_proj(x2d, g_attn, w_in_p, gvec):
    t = x2d.shape[0]
    tile = lambda w: pl.BlockSpec((TM_PROJ, w), lambda i: (i, 0))
    widths = (DIFF_W, DIFF_W, WIN_Q_W, WIN_KV_W, DIFF_W, WIN_KV_W)
    return pl.pallas_call(
        _in_proj_kernel,
        grid=(t // TM_PROJ,),
        in_specs=[tile(D_MODEL), _resident((1, D_MODEL)), _resident((D_MODEL, D_IN)),
                  _resident((1, NORMED_W))],
        out_specs=[tile(w) for w in widths],
        out_shape=[jax.ShapeDtypeStruct((t, w), BF16) for w in widths],
        compiler_params=_compiler_params(("parallel",)),
        name="in_proj",
    )(x2d, g_attn, w_in_p, gvec)


def _diff_attn_kernel(lam_init, bounded_ref, lamv_ref, q_ref, k_ref, v_ref, bias_ref, bias_t_ref, g_ref,
                      g_col_ref, o_ref, s_scr, p_scr, v_ones_scr, vt_scr, pt_scr, pt_alt_scr, at_scr, at_alt_scr):
    seq = k_ref.shape[0]
    nq, nk = seq // TQ, seq // TK
    lv = lamv_ref[...]
    lam = (jnp.exp(jnp.sum(lv[0:1] * lv[1:2], axis=-1, keepdims=True))
           - jnp.exp(jnp.sum(lv[2:3] * lv[3:4], axis=-1, keepdims=True)) + lam_init)
    lane = lax.broadcasted_iota(jnp.int32, (TQ, LANES), 1)
    reload_offset = jnp.maximum(bounded_ref[0], 1) - 1

    def stacked_q(tile):
        start = tile * TQ if isinstance(tile, int) else pl.multiple_of(tile * TQ, TQ)
        q = q_ref[pl.ds(start, TQ), :]
        zero = jnp.zeros_like(q)
        return jnp.concatenate([jnp.where(lane < HEAD_DIM, q, zero), jnp.where(lane >= HEAD_DIM, q, zero)], axis=0)

    def key_logits(qs, kj):
        return lax.dot_general(qs, k_ref[kj * TK:(kj + 1) * TK, :], (((1,), (1,)), ((), ())),
                               preferred_element_type=F32)

    def finish(tile, p_ref):
        res = jnp.dot(p_ref[...], v_ones_scr[...], preferred_element_type=F32)
        out = res[:, :LANES] * (1.0 / res[:, LANES:])
        o = out[:TQ] - lam * out[TQ:]
        ms = jnp.mean(o * o, axis=-1, keepdims=True)
        o = o * lax.rsqrt(ms + EPS) * g_ref[...] * (1.0 - lam_init)
        o_ref[pl.ds(pl.multiple_of(tile * TQ, TQ), TQ), :] = o.astype(BF16)

    def bias_tiles(tile):
        return [jnp.clip(kj - tile, -2, 2) + 2 for kj in range(nk)]

    def bias_class(tile, kj):
        return min(max(kj - tile, -2), 2) + 2

    def far_factor(which):
        f = jnp.exp2(bias_t_ref[which, 0:1, :])
        return jnp.concatenate([f, f], axis=1)

    def unshifted_weights(tile, pt_ref):
        qs = stacked_q(tile)
        sums = {}
        for kj in range(nk):
            rows = slice(kj * TK, (kj + 1) * TK)
            cls = bias_class(tile, kj)
            st = lax.dot_general(k_ref[rows, :], qs, (((1,), (1,)), ((), ())),
                                 preferred_element_type=F32)
            if cls in (0, 4):
                pt = jnp.exp2(st)
            else:
                bt = bias_t_ref[cls]
                pt = jnp.exp2(st + jnp.concatenate([bt, bt], axis=1))
                cls = 2
            pt_ref[rows, :] = pt
            part = jnp.sum(pt.reshape(TK // 8, 8, 2 * TQ), axis=0)
            sums[cls] = sums[cls] + part if cls in sums else part
        return sums

    def unshifted_output(tile, sums, pt_ref, at_ref):
        factors = {2: None, 0: far_factor(0), 4: far_factor(4)}
        key_sums = sum(s if factors[c] is None else s * factors[c] for c, s in sums.items())
        r = 1.0 / jnp.sum(key_sums, axis=0, keepdims=True)
        r = jnp.concatenate([r[:, :TQ], lam * r[:, TQ:]], axis=1)
        scales = {c: r if f is None else r * f for c, f in factors.items()}
        for kj in range(nk):
            rows = slice(kj * TK, (kj + 1) * TK)
            cls = bias_class(tile, kj)
            scale = scales[cls if cls in (0, 4) else 2]
            back = pl.ds(pl.multiple_of(kj * TK + reload_offset, TK), TK)
            at_ref[rows, :] = (pt_ref[back, :TQ] * scale[:, :TQ] - pt_ref[back, TQ:] * scale[:, TQ:]).astype(BF16)
        ot = jnp.dot(vt_scr[...], at_ref[...], preferred_element_type=F32)
        ms = jnp.mean(ot * ot, axis=0, keepdims=True)
        gcol = jnp.concatenate([g_col_ref[...]] * (TQ // LANES), axis=1)
        ot = ot * lax.rsqrt(ms + EPS) * gcol * (1.0 - lam_init)
        o_ref[tile * TQ:(tile + 1) * TQ, :] = ot.T.astype(BF16)

    def unshifted_sequence():
        pt_bufs, at_bufs = (pt_scr, pt_alt_scr), (at_scr, at_alt_scr)
        sums = unshifted_weights(0, pt_bufs[0])
        for tile in range(nq):
            following = unshifted_weights(tile + 1, pt_bufs[(tile + 1) % 2]) if tile + 1 < nq else None
            unshifted_output(tile, sums, pt_bufs[tile % 2], at_bufs[tile % 2])
            sums = following

    def shifted_tile(tile, carry):
        qs, tiles = stacked_q(tile), bias_tiles(tile)
        for kj in range(nk):
            s_scr[:, kj * TK:(kj + 1) * TK] = key_logits(qs, kj)

        def strip(i, c):
            for mp in range(2):
                rows = pl.ds(pl.multiple_of(mp * TQ + i * STRIP, STRIP), STRIP)
                brows = pl.ds(pl.multiple_of(i * STRIP, STRIP), STRIP)
                t = jnp.concatenate([s_scr[rows, kj * TK:(kj + 1) * TK] + bias_ref[tiles[kj], brows, :]
                                     for kj in range(nk)], axis=1)
                p_scr[rows, :] = jnp.exp2(t - jnp.max(t, axis=-1, keepdims=True)).astype(BF16)
            return c

        lax.fori_loop(0, TQ // STRIP, strip, 0)
        finish(tile, p_scr)
        return carry

    bounded = bounded_ref[0] != 0

    @pl.when(bounded)
    def _():
        vt_scr[...] = v_ref[...].astype(F32).T.astype(BF16)
        unshifted_sequence()

    @pl.when(jnp.logical_not(bounded))
    def _():
        v_ones_scr[:, :LANES] = v_ref[...]
        v_ones_scr[:, LANES:] = jnp.ones(v_ref.shape, BF16)
        lax.fori_loop(0, nq, shifted_tile, 0)


def _diff_attn(dq, dk, dv, bounded, lamv, bias5, subln_g, lam_init, batch, seq):
    seq_block = pl.BlockSpec((seq, LANES), lambda b, h: (b, h))
    bias_block = pl.BlockSpec((None, 5, TQ, TK), lambda b, h: (h, 0, 0, 0))
    g_col = jnp.broadcast_to(subln_g.reshape(LANES, 1), (LANES, LANES))
    return pl.pallas_call(
        functools.partial(_diff_attn_kernel, lam_init),
        grid=(batch, DIFF_HEADS),
        in_specs=[
            pl.BlockSpec(memory_space=pltpu.SMEM),
            pl.BlockSpec((8, LANES), lambda b, h: (0, 0)),
            seq_block, seq_block, seq_block,
            bias_block, bias_block,
            pl.BlockSpec((1, LANES), lambda b, h: (0, 0)),
            pl.BlockSpec((LANES, LANES), lambda b, h: (0, 0)),
        ],
        out_specs=seq_block,
        out_shape=jax.ShapeDtypeStruct((batch * seq, DIFF_W), BF16),
        scratch_shapes=[pltpu.VMEM((2 * TQ, seq), F32), pltpu.VMEM((2 * TQ, seq), BF16),
                        pltpu.VMEM((seq, 2 * LANES), BF16), pltpu.VMEM((LANES, seq), BF16),
                        pltpu.VMEM((seq, 2 * TQ), F32), pltpu.VMEM((seq, 2 * TQ), F32),
                        pltpu.VMEM((seq, TQ), BF16), pltpu.VMEM((seq, TQ), BF16)],
        compiler_params=_compiler_params(("parallel", "parallel")),
        name="diff_attn",
    )(bounded, lamv, dq, dk, dv, bias5, jnp.swapaxes(bias5, 2, 3), subln_g, g_col)


WIN_KEYS = 3 * BLOCK
BLOCKS_PER_ITER = 4


def _win_attn_kernel(bounded_ref, sink_ref, q_ref, k_ref, v_ref, bias_ref, o_ref, s_scr, p_scr, v_ones_scr):
    seq = k_ref.shape[0]
    nb = seq // BLOCK
    group_rows = WIN_GROUP * BLOCK
    lane = lax.broadcasted_iota(jnp.int32, (BLOCK, LANES), 1)
    v_ones_scr[:, :LANES] = v_ref[...]
    v_ones_scr[:, LANES:] = jnp.ones(v_ref.shape, BF16)

    def window(n):
        start = pl.multiple_of(jnp.clip(n * BLOCK - BLOCK, 0, seq - WIN_KEYS), BLOCK)
        variant = jnp.where(n == 0, 0, jnp.where(n == nb - 1, 2, 1))
        return start, variant

    def grouped_q(n, kv):
        q = q_ref[pl.ds(pl.multiple_of(n * BLOCK, BLOCK), BLOCK), :]
        zero = jnp.zeros((BLOCK, LANES), q.dtype)
        keep = (lane < HEAD_DIM) if kv == 0 else (lane >= HEAD_DIM)
        return jnp.concatenate([jnp.where(keep, q[:, j * LANES:(j + 1) * LANES], zero)
                                for j in range(WIN_GROUP)], axis=0)

    def group_logits(n, kv, start):
        return lax.dot_general(grouped_q(n, kv), k_ref[pl.ds(start, WIN_KEYS), :], (((1,), (1,)), ((), ())),
                               preferred_element_type=F32)

    def store(n, outs):
        rows = pl.ds(pl.multiple_of(n * BLOCK, BLOCK), BLOCK)
        for j in range(WIN_GROUP):
            blk = jnp.where(lane < HEAD_DIM, outs[0][j * BLOCK:(j + 1) * BLOCK], outs[1][j * BLOCK:(j + 1) * BLOCK])
            o_ref[rows, j * LANES:(j + 1) * LANES] = blk.astype(BF16)

    def unshifted_block(n):
        start, variant = window(n)
        outs = []
        for kv in range(WIN_KV_HEADS):
            grows = slice(kv * group_rows, (kv + 1) * group_rows)
            p = jnp.exp2(group_logits(n, kv, start) + bias_ref[variant, grows, :]).astype(BF16)
            res = jnp.dot(p, v_ones_scr[pl.ds(start, WIN_KEYS), :], preferred_element_type=F32)
            sink_w = jnp.concatenate(
                [jnp.broadcast_to(jnp.exp2(jnp.full((1, LANES), sink_ref[kv * WIN_GROUP + g], F32)), (BLOCK, LANES))
                 for g in range(WIN_GROUP)], axis=0)
            outs.append(res[:, :LANES] * (1.0 / (res[:, LANES:] + sink_w)))
        store(n, outs)

    def unshifted_blocks(i, carry):
        for u in range(BLOCKS_PER_ITER):
            unshifted_block(i * BLOCKS_PER_ITER + u)
        return carry

    def shifted_block(n, carry):
        start, variant = window(n)
        outs = []
        for kv in range(WIN_KV_HEADS):
            s_scr[...] = group_logits(n, kv, start)
            sink_terms = []
            for r in range(0, group_rows, STRIP):
                rows = slice(r, r + STRIP)
                grows = slice(kv * group_rows + r, kv * group_rows + r + STRIP)
                t = s_scr[rows, :] + bias_ref[variant, grows, :]
                sink = sink_ref[(kv * group_rows + r) // BLOCK]
                row_max = jnp.max(t, axis=-1, keepdims=True)
                p_scr[rows, :] = jnp.exp2(t - jnp.maximum(row_max, sink)).astype(BF16)
                m_wide = jnp.maximum(jnp.broadcast_to(row_max, (STRIP, LANES)), sink)
                sink_terms.append(jnp.exp2(sink - m_wide))
            res = jnp.dot(p_scr[...], v_ones_scr[pl.ds(start, WIN_KEYS), :], preferred_element_type=F32)
            l = res[:, LANES:] + jnp.concatenate(sink_terms, axis=0)
            outs.append(res[:, :LANES] * (1.0 / l))
        store(n, outs)
        return carry

    bounded = bounded_ref[0] != 0

    @pl.when(bounded)
    def _():
        lax.fori_loop(0, nb // BLOCKS_PER_ITER, unshifted_blocks, 0)

    @pl.when(jnp.logical_not(bounded))
    def _():
        lax.fori_loop(0, nb, shifted_block, 0)


def _win_attn(wq, wk, wv, bounded, bias3, sink, batch, seq):
    rows = WIN_HEADS * BLOCK
    smem = pl.BlockSpec(memory_space=pltpu.SMEM)
    seq_block = lambda w: pl.BlockSpec((seq, w), lambda b: (b, 0))
    return pl.pallas_call(
        _win_attn_kernel,
        grid=(batch,),
        in_specs=[smem, smem, seq_block(WIN_Q_W), seq_block(WIN_KV_W), seq_block(WIN_KV_W),
                  _resident((3, rows, WIN_KEYS))],
        out_specs=seq_block(WIN_Q_W),
        out_shape=jax.ShapeDtypeStruct((batch * seq, WIN_Q_W), BF16),
        scratch_shapes=[pltpu.VMEM((rows // 2, WIN_KEYS), F32), pltpu.VMEM((rows // 2, WIN_KEYS), BF16),
                        pltpu.VMEM((seq, 2 * LANES), BF16)],
        compiler_params=_compiler_params(("parallel",)),
        name="win_attn",
    )(bounded, sink, wq, wk, wv, bias3)


def _mix_ffn_kernel(x_ref, xp_ref, xn_ref, oa_ref, oap_ref, oan_ref, ob_ref, obp_ref, obn_ref,
                    wa_ref, wb_ref, g_ref, wg_ref, wu_ref, wd_ref, cw_ref, o_ref, act_scr):
    i = pl.program_id(1)
    last = pl.num_programs(1) - 1
    ext_rows = TM_FFN + 2 * HALO
    tile = slice(HALO, HALO + TM_FFN)
    ext = lambda p, c, n: jnp.concatenate([p[...], c[...], n[...]], axis=0)
    x1 = (ext(xp_ref, x_ref, xn_ref)
          + jnp.dot(ext(oap_ref, oa_ref, oan_ref), wa_ref[...], preferred_element_type=F32)
          + jnp.dot(ext(obp_ref, ob_ref, obn_ref), wb_ref[...], preferred_element_type=F32))
    o_ref[...] = x1[tile]
    ms = jnp.mean(x1 * x1, axis=-1, keepdims=True)
    h_ext = (x1 * lax.rsqrt(ms + EPS) * g_ref[...]).astype(BF16)
    row = lax.broadcasted_iota(jnp.int32, (ext_rows, 1), 0)
    outside = ((row < HALO) & (i == 0)) | ((row >= HALO + TM_FFN) & (i == last))
    h_ext = jnp.where(outside, jnp.zeros_like(h_ext), h_ext)
    h = h_ext[tile]
    for c0 in range(0, D_FF, FF_CHUNK):
        cols = slice(c0, min(c0 + FF_CHUNK, D_FF))
        g_ext = jnp.dot(h_ext, wg_ref[:, cols], preferred_element_type=F32)
        g = g_ext[tile]
        g_dn = pltpu.roll(g_ext, 1, 0)[tile]
        g_up = pltpu.roll(g_ext, ext_rows - 1, 0)[tile]
        cw = cw_ref[:, cols]
        u = g_dn * cw[0:1] + g * cw[1:2] + g_up * cw[2:3] + cw[3:4]
        up = jnp.dot(h, wu_ref[:, cols], preferred_element_type=F32)
        act = (u * (1.0 / (1.0 + jnp.exp(-u)))) * up
        act_scr[:, cols] = act.astype(BF16)
    o_ref[...] = o_ref[...] + jnp.dot(act_scr[...], wd_ref[...], preferred_element_type=F32)


def _mix_ffn(x2d, oa, ob, wa, wb, g_ffn, wg, wu, wd, cw, batch, seq):
    nt = seq // TM_FFN
    hb = TM_FFN // HALO
    n_halo = batch * seq // HALO

    def with_halo(width):
        return [pl.BlockSpec((TM_FFN, width), lambda b, i: (b * nt + i, 0)),
                pl.BlockSpec((HALO, width), lambda b, i: (jnp.maximum((b * nt + i) * hb - 1, 0), 0)),
                pl.BlockSpec((HALO, width), lambda b, i: (jnp.minimum((b * nt + i + 1) * hb, n_halo - 1), 0))]

    return pl.pallas_call(
        _mix_ffn_kernel,
        grid=(batch, nt),
        in_specs=(with_halo(D_MODEL) + with_halo(DIFF_W) + with_halo(WIN_Q_W)
                  + [_resident((DIFF_W, D_MODEL)), _resident((WIN_Q_W, D_MODEL)), _resident((1, D_MODEL)),
                     _resident((D_MODEL, D_FF)), _resident((D_MODEL, D_FF)), _resident((D_FF, D_MODEL)),
                     _resident((4, D_FF))]),
        out_specs=pl.BlockSpec((TM_FFN, D_MODEL), lambda b, i: (b * nt + i, 0)),
        out_shape=jax.ShapeDtypeStruct((batch * seq, D_MODEL), F32),
        scratch_shapes=[pltpu.VMEM((TM_FFN, D_FF), BF16)],
        compiler_params=_compiler_params(("parallel", "arbitrary")),
        name="mix_ffn",
    )(x2d, x2d, x2d, oa, oa, oa, ob, ob, ob, wa, wb, g_ffn, wg, wu, wd, cw)


def _skew(vec, rows):
    h, n = vec.shape
    w = jnp.pad(vec, ((0, 0), (0, 1)))
    return jnp.tile(w, (1, rows))[:, :rows * n].reshape(h, rows, n)


def _bias_tables(rel_bias):
    table = rel_bias.astype(F32).T
    n = 4 * TQ - 1
    rel = jnp.arange(n) - (2 * TQ - 1)
    vec = table[:DIFF_HEADS][:, _rel_bucket(rel)]
    near = _skew(vec, TQ)[:, :, TQ - 1:TQ - 1 + 3 * TK]
    near = near.reshape(DIFF_HEADS, TQ, 3, TK).transpose(0, 2, 1, 3)
    far = lambda r: jnp.broadcast_to(table[:DIFF_HEADS, _rel_bucket(jnp.int32(r))][:, None, None, None],
                                     (DIFF_HEADS, 1, TQ, TK))
    diff = jnp.concatenate([far(-2 * TK), near, far(2 * TK)], axis=1)
    hi, lo = jnp.max(table[:DIFF_HEADS], axis=1), jnp.min(table[:DIFF_HEADS], axis=1)
    diff = (diff - (0.5 * (hi + lo))[:, None, None, None]) * LOG2E
    diff_half_range = jnp.max(0.5 * (hi - lo)) * LOG2E
    n = 6 * BLOCK - 1
    rel = jnp.arange(n) - (3 * BLOCK - 1)
    vec = jnp.where(jnp.abs(rel) <= WINDOW, table[DIFF_HEADS:][:, _rel_bucket(rel)], NEG_INF)
    band = _skew(vec, BLOCK)[:, :, BLOCK - 1:BLOCK - 1 + 5 * BLOCK]
    win = jnp.stack([band[:, :, (2 - v) * BLOCK:(2 - v) * BLOCK + WIN_KEYS] for v in range(3)])
    win = win.reshape(3, WIN_HEADS * BLOCK, WIN_KEYS) * LOG2E
    win_hi = jnp.max(table[DIFF_HEADS:], axis=1) * LOG2E
    win_lo = jnp.min(table[DIFF_HEADS:], axis=1) * LOG2E
    return dict(diff=diff, diff_half_range=diff_half_range, win=win, win_hi=win_hi, win_lo=win_lo)


def _qk_bound(q_gain, k_gain):
    return HEAD_DIM * NORM_SLACK * jnp.max(jnp.abs(q_gain)) * jnp.max(jnp.abs(k_gain))


def _is_bounded(half_range):
    return (half_range <= MAX_UNSHIFTED_LOGIT).astype(jnp.int32).reshape(1)


def _layer_params(l, tables, norm_attn_g, w_in, diff_q_norm_g, diff_k_norm_g, diff_lambda_q1, diff_lambda_k1,
                  diff_lambda_q2, diff_lambda_k2, diff_subln_g, win_q_norm_g, win_k_norm_g, win_sink,
                  w_out, norm_ffn_g, w_gate, w_up, conv_w, conv_b, w_down):
    scale = HEAD_DIM ** -0.5
    w = w_in[l]
    dq, dk, dv = w[:, :DIFF_W], w[:, DIFF_W:2 * DIFF_W], w[:, 2 * DIFF_W:3 * DIFF_W]
    wq = w[:, 3 * DIFF_W:3 * DIFF_W + WIN_Q_W].reshape(D_MODEL, WIN_KV_HEADS, WIN_GROUP, HEAD_DIM)
    wq = wq.transpose(0, 2, 1, 3).reshape(D_MODEL, WIN_Q_W)
    wk = w[:, 3 * DIFF_W + WIN_Q_W:3 * DIFF_W + WIN_Q_W + WIN_KV_W]
    wv = w[:, 3 * DIFF_W + WIN_Q_W + WIN_KV_W:]
    w_in_p = jnp.concatenate([dq, dk, wq, wk, dv, wv], axis=1).astype(BF16)
    gvec = jnp.concatenate([jnp.tile(diff_q_norm_g[l], 2 * DIFF_HEADS) * (scale * LOG2E),
                            jnp.tile(diff_k_norm_g[l], 2 * DIFF_HEADS),
                            jnp.tile(win_q_norm_g[l], WIN_HEADS) * (scale * LOG2E),
                            jnp.tile(win_k_norm_g[l], WIN_KV_HEADS)]).reshape(1, NORMED_W).astype(F32)
    lamv = jnp.zeros((8, LANES), F32)
    lamv = lamv.at[0:4, :HEAD_DIM].set(jnp.stack([diff_lambda_q1[l], diff_lambda_k1[l],
                                                  diff_lambda_q2[l], diff_lambda_k2[l]]).astype(F32))
    wo = w_out[l]
    wb = wo[DIFF_W:].reshape(WIN_KV_HEADS, WIN_GROUP, HEAD_DIM, D_MODEL)
    wb = wb.transpose(1, 0, 2, 3).reshape(WIN_Q_W, D_MODEL)
    cw = jnp.concatenate([conv_w[l], conv_b[l][None]], axis=0).astype(F32)
    return dict(
        g_attn=norm_attn_g[l].reshape(1, D_MODEL).astype(F32), w_in_p=w_in_p, gvec=gvec, lamv=lamv,
        subln_g=diff_subln_g[l].reshape(1, 2 * HEAD_DIM).astype(F32),
        wa=wo[:DIFF_W].astype(BF16), wb=wb.astype(BF16),
        g_ffn=norm_ffn_g[l].reshape(1, D_MODEL).astype(F32),
        wg=w_gate[l].astype(BF16), wu=w_up[l].astype(BF16), wd=w_down[l].astype(BF16), cw=cw,
        lam_init=0.8 - 0.6 * math.exp(-0.3 * l),
        diff_bounded=_is_bounded(_qk_bound(diff_q_norm_g[l].astype(F32) * (scale * LOG2E), diff_k_norm_g[l])
                                 + tables["diff_half_range"]),
        **_window_softmax_params(tables, win_q_norm_g[l].astype(F32) * (scale * LOG2E), win_k_norm_g[l],
                                 win_sink[l].astype(F32) * LOG2E),
    )


def _window_softmax_params(tables, q_gain, k_gain, sink):
    qk = _qk_bound(q_gain, k_gain)
    hi = jnp.maximum(tables["win_hi"] + qk, sink)
    lo = jnp.minimum(tables["win_lo"] - qk, sink)
    centre = 0.5 * (hi + lo)
    return dict(win_bias=tables["win"] - jnp.repeat(centre, BLOCK)[None, :, None], win_sink=sink - centre,
                win_bounded=_is_bounded(jnp.max(0.5 * (hi - lo))))


def _encoder_layer(x, p, bias5):
    batch, seq, _ = x.shape
    x2d = x.reshape(batch * seq, D_MODEL)
    dq, dk, wq, wk, dv, wv = _in_proj(x2d, p["g_attn"], p["w_in_p"], p["gvec"])
    oa = _diff_attn(dq, dk, dv, p["diff_bounded"], p["lamv"], bias5, p["subln_g"], p["lam_init"], batch, seq)
    ob = _win_attn(wq, wk, wv, p["win_bounded"], p["win_bias"], p["win_sink"], batch, seq)
    y = _mix_ffn(x2d, oa, ob, p["wa"], p["wb"], p["g_ffn"], p["wg"], p["wu"], p["wd"], p["cw"], batch, seq)
    return y.reshape(batch, seq, D_MODEL)


def kernel(x_prompt, x_sample, norm_attn_g, w_in, diff_q_norm_g, diff_k_norm_g, diff_lambda_q1, diff_lambda_k1, diff_lambda_q2, diff_lambda_k2, diff_subln_g, win_q_norm_g, win_k_norm_g, win_sink, rel_bias, w_out, norm_ffn_g, w_gate, w_up, conv_w, conv_b, w_down):
    depth = w_in.shape[0]
    tables = _bias_tables(rel_bias)
    layers = [_layer_params(l, tables, norm_attn_g, w_in, diff_q_norm_g, diff_k_norm_g, diff_lambda_q1,
                            diff_lambda_k1, diff_lambda_q2, diff_lambda_k2, diff_subln_g, win_q_norm_g,
                            win_k_norm_g, win_sink, w_out, norm_ffn_g, w_gate, w_up, conv_w, conv_b, w_down)
              for l in range(depth)]

    def run(x):
        for p in layers:
            x = _encoder_layer(x, p, tables["diff"])
        return x

    return run(x_prompt), run(x_sample)
```

```python
import functools
import math

import jax
import jax.numpy as jnp
from jax import lax
from jax.experimental import pallas as pl
from jax.experimental.pallas import tpu as pltpu

D_MODEL = 1024
SEQ = 2048
HEAD_DIM = 64
DIFF_HEADS = 4
WIN_HEADS = 8
WIN_KV_HEADS = 2
WIN_GROUP = WIN_HEADS // WIN_KV_HEADS
WINDOW = 128
BLOCK = 128
N_BUCKETS = 32
MAX_DISTANCE = 128
D_FF = 2816
EPS = 1e-6
NEG_INF = -1e30

DIFF_W = DIFF_HEADS * 2 * HEAD_DIM
WIN_Q_W = WIN_HEADS * HEAD_DIM
WIN_KV_W = WIN_KV_HEADS * HEAD_DIM
D_IN = 3 * DIFF_W + WIN_Q_W + 2 * WIN_KV_W
NORMED_W = 2 * DIFF_W + WIN_Q_W + WIN_KV_W

LANES = 128
VMEM_LIMIT = 56 * 1024 * 1024

TM_PROJ = 1024
TQ = 256
TK = 256
HEADS_PER_STEP = 1
STRIP = 16
LOG2E = math.log2(math.e)
MAX_UNSHIFTED_LOGIT = 100.0
NORM_SLACK = 1.02
TM_FFN = 1024
FF_CHUNK = 512
HALO = 16

BF16 = jnp.bfloat16
F32 = jnp.float32


def _rel_bucket(rel):
    nb = N_BUCKETS // 2
    max_exact = nb // 2
    ret = jnp.where(rel > 0, nb, 0)
    n = jnp.abs(rel)
    nf = jnp.maximum(n, 1).astype(F32)
    large = max_exact + (jnp.log(nf / max_exact) / math.log(MAX_DISTANCE / max_exact)
                         * (nb - max_exact)).astype(jnp.int32)
    large = jnp.minimum(large, nb - 1)
    return ret + jnp.where(n < max_exact, n, large)


def _compiler_params(semantics):
    return pltpu.CompilerParams(dimension_semantics=semantics, vmem_limit_bytes=VMEM_LIMIT)


def _resident(shape):
    nd = len(shape)
    return pl.BlockSpec(shape, lambda *_: (0,) * nd, pipeline_mode=pl.Buffered(1))


def _in_proj_kernel(x_ref, g_ref, w_ref, gv_ref, dq_ref, dk_ref, wq_ref, wk_ref, dv_ref, wv_ref):
    x = x_ref[...]
    ms = jnp.mean(x * x, axis=-1, keepdims=True)
    h = (x * lax.rsqrt(ms + EPS) * g_ref[...]).astype(BF16)
    proj = jnp.dot(h, w_ref[...], preferred_element_type=F32)
    first_head = lax.broadcasted_iota(jnp.int32, (TM_PROJ, LANES), 1) < HEAD_DIM

    def head_norm(c0):
        blk = proj[:, c0:c0 + LANES]
        sq = blk * blk
        lo = jnp.sum(jnp.where(first_head, sq, 0.0), axis=-1, keepdims=True)
        hi = jnp.sum(sq, axis=-1, keepdims=True) - lo
        ss = jnp.where(first_head, lo, hi)
        return (blk * lax.rsqrt(ss * (1.0 / HEAD_DIM) + EPS) * gv_ref[:, c0:c0 + LANES]).astype(BF16)

    for ref, base, width in ((dq_ref, 0, DIFF_W), (dk_ref, DIFF_W, DIFF_W), (wq_ref, 2 * DIFF_W, WIN_Q_W),
                             (wk_ref, 2 * DIFF_W + WIN_Q_W, WIN_KV_W)):
        for c in range(0, width, LANES):
            ref[:, c:c + LANES] = head_norm(base + c)
    dv_ref[...] = proj[:, NORMED_W:NORMED_W + DIFF_W].astype(BF16)
    wv_ref[...] = proj[:, NORMED_W + DIFF_W:].astype(BF16)


def _in_proj(x2d, g_attn, w_in_p, gvec):
    t = x2d.shape[0]
    tile = lambda w: pl.BlockSpec((TM_PROJ, w), lambda i: (i, 0))
    widths = (DIFF_W, DIFF_W, WIN_Q_W, WIN_KV_W, DIFF_W, WIN_KV_W)
    return pl.pallas_call(
        _in_proj_kernel,
        grid=(t // TM_PROJ,),
        in_specs=[tile(D_MODEL), _resident((1, D_MODEL)), _resident((D_MODEL, D_IN)),
                  _resident((1, NORMED_W))],
        out_specs=[tile(w) for w in widths],
        out_shape=[jax.ShapeDtypeStruct((t, w), BF16) for w in widths],
        compiler_params=_compiler_params(("parallel",)),
        name="in_proj",
    )(x2d, g_attn, w_in_p, gvec)


def _diff_attn_kernel(lam_init, bounded_ref, lamv_ref, q_ref, k_ref, v_ref, bias_ref, bias_t_ref, g_ref,
                      g_col_ref, o_ref, s_scr, p_scr, v_ones_scr, vt_scr, pt_scr, pt_alt_scr, at_scr, at_alt_scr):
    seq = k_ref.shape[0]
    nq, nk = seq // TQ, seq // TK
    lv = lamv_ref[...]
    lam = (jnp.exp(jnp.sum(lv[0:1] * lv[1:2], axis=-1, keepdims=True))
           - jnp.exp(jnp.sum(lv[2:3] * lv[3:4], axis=-1, keepdims=True)) + lam_init)
    lane = lax.broadcasted_iota(jnp.int32, (TQ, LANES), 1)
    reload_offset = jnp.maximum(bounded_ref[0], 1) - 1
    head_cols = [slice(hd * LANES, (hd + 1) * LANES) for hd in range(HEADS_PER_STEP)]

    def stacked_q(hd, tile):
        start = tile * TQ if isinstance(tile, int) else pl.multiple_of(tile * TQ, TQ)
        q = q_ref[pl.ds(start, TQ), head_cols[hd]]
        zero = jnp.zeros_like(q)
        return jnp.concatenate([jnp.where(lane < HEAD_DIM, q, zero), jnp.where(lane >= HEAD_DIM, q, zero)], axis=0)

    def bias_class(tile, kj):
        return min(max(kj - tile, -2), 2) + 2

    def far_factor(hd, which):
        f = jnp.exp2(bias_t_ref[hd, which, 0:1, :])
        return jnp.concatenate([f, f], axis=1)

    def unshifted_weights(hd, tile, pt_ref):
        qs = stacked_q(hd, tile)
        sums = {}
        for kj in range(nk):
            rows = slice(kj * TK, (kj + 1) * TK)
            cls = bias_class(tile, kj)
            st = lax.dot_general(k_ref[rows, head_cols[hd]], qs, (((1,), (1,)), ((), ())),
                                 preferred_element_type=F32)
            if cls in (0, 4):
                pt = jnp.exp2(st)
            else:
                bt = bias_t_ref[hd, cls]
                pt = jnp.exp2(st + jnp.concatenate([bt, bt], axis=1))
                cls = 2
            pt_ref[rows, :] = pt
            part = jnp.sum(pt.reshape(TK // 8, 8, 2 * TQ), axis=0)
            sums[cls] = sums[cls] + part if cls in sums else part
        return sums

    def unshifted_output(hd, tile, sums, pt_ref, at_ref):
        factors = {2: None, 0: far_factor(hd, 0), 4: far_factor(hd, 4)}
        key_sums = sum(s if factors[c] is None else s * factors[c] for c, s in sums.items())
        r = 1.0 / jnp.sum(key_sums, axis=0, keepdims=True)
        r = jnp.concatenate([r[:, :TQ], lam * r[:, TQ:]], axis=1)
        scales = {c: r if f is None else r * f for c, f in factors.items()}
        for kj in range(nk):
            rows = slice(kj * TK, (kj + 1) * TK)
            cls = bias_class(tile, kj)
            scale = scales[cls if cls in (0, 4) else 2]
            back = pl.ds(pl.multiple_of(kj * TK + reload_offset, TK), TK)
            at_ref[rows, :] = (pt_ref[back, :TQ] * scale[:, :TQ] - pt_ref[back, TQ:] * scale[:, TQ:]).astype(BF16)
        ot = jnp.dot(vt_scr[hd], at_ref[...], preferred_element_type=F32)
        ms = jnp.mean(ot * ot, axis=0, keepdims=True)
        gcol = jnp.concatenate([g_col_ref[...]] * (TQ // LANES), axis=1)
        ot = ot * lax.rsqrt(ms + EPS) * gcol * (1.0 - lam_init)
        o_ref[tile * TQ:(tile + 1) * TQ, head_cols[hd]] = ot.T.astype(BF16)

    def unshifted_sequence():
        pt_bufs, at_bufs = (pt_scr, pt_alt_scr), (at_scr, at_alt_scr)
        items = [(hd, tile) for hd in range(HEADS_PER_STEP) for tile in range(nq)]
        sums = unshifted_weights(*items[0], pt_bufs[0])
        for n, item in enumerate(items):
            following = unshifted_weights(*items[n + 1], pt_bufs[(n + 1) % 2]) if n + 1 < len(items) else None
            unshifted_output(*item, sums, pt_bufs[n % 2], at_bufs[n % 2])
            sums = following

    def shifted_head(hd):
        v_ones_scr[:, :LANES] = v_ref[:, head_cols[hd]]
        v_ones_scr[:, LANES:] = jnp.ones((seq, LANES), BF16)

        def shifted_tile(tile, carry):
            qs = stacked_q(hd, tile)
            tiles = [jnp.clip(kj - tile, -2, 2) + 2 for kj in range(nk)]
            for kj in range(nk):
                s_scr[:, kj * TK:(kj + 1) * TK] = lax.dot_general(
                    qs, k_ref[kj * TK:(kj + 1) * TK, head_cols[hd]], (((1,), (1,)), ((), ())),
                    preferred_element_type=F32)

            def strip(i, c):
                for mp in range(2):
                    rows = pl.ds(pl.multiple_of(mp * TQ + i * STRIP, STRIP), STRIP)
                    brows = pl.ds(pl.multiple_of(i * STRIP, STRIP), STRIP)
                    t = jnp.concatenate([s_scr[rows, kj * TK:(kj + 1) * TK] + bias_ref[hd, tiles[kj], brows, :]
                                         for kj in range(nk)], axis=1)
                    p_scr[rows, :] = jnp.exp2(t - jnp.max(t, axis=-1, keepdims=True)).astype(BF16)
                return c

            lax.fori_loop(0, TQ // STRIP, strip, 0)
            res = jnp.dot(p_scr[...], v_ones_scr[...], preferred_element_type=F32)
            out = res[:, :LANES] * (1.0 / res[:, LANES:])
            o = out[:TQ] - lam * out[TQ:]
            ms = jnp.mean(o * o, axis=-1, keepdims=True)
            o = o * lax.rsqrt(ms + EPS) * g_ref[...] * (1.0 - lam_init)
            o_ref[pl.ds(pl.multiple_of(tile * TQ, TQ), TQ), head_cols[hd]] = o.astype(BF16)
            return carry

        lax.fori_loop(0, nq, shifted_tile, 0)

    bounded = bounded_ref[0] != 0

    @pl.when(bounded)
    def _():
        for hd in range(HEADS_PER_STEP):
            vt_scr[hd] = v_ref[:, head_cols[hd]].astype(F32).T.astype(BF16)
        unshifted_sequence()

    @pl.when(jnp.logical_not(bounded))
    def _():
        for hd in range(HEADS_PER_STEP):
            shifted_head(hd)


def _diff_attn(dq, dk, dv, bounded, lamv, bias5, bias5_t, subln_g, lam_init, batch, seq):
    seq_block = pl.BlockSpec((seq, HEADS_PER_STEP * LANES), lambda b, h: (b, h))
    bias_block = pl.BlockSpec((HEADS_PER_STEP, 5, TQ, TK), lambda b, h: (h, 0, 0, 0))
    g_col = jnp.broadcast_to(subln_g.reshape(LANES, 1), (LANES, LANES))
    return pl.pallas_call(
        functools.partial(_diff_attn_kernel, lam_init),
        grid=(batch, DIFF_HEADS // HEADS_PER_STEP),
        in_specs=[
            pl.BlockSpec(memory_space=pltpu.SMEM),
            pl.BlockSpec((8, LANES), lambda b, h: (0, 0)),
            seq_block, seq_block, seq_block,
            bias_block, bias_block,
            pl.BlockSpec((1, LANES), lambda b, h: (0, 0)),
            pl.BlockSpec((LANES, LANES), lambda b, h: (0, 0)),
        ],
        out_specs=seq_block,
        out_shape=jax.ShapeDtypeStruct((batch * seq, DIFF_W), BF16),
        scratch_shapes=[pltpu.VMEM((2 * TQ, seq), F32), pltpu.VMEM((2 * TQ, seq), BF16),
                        pltpu.VMEM((seq, 2 * LANES), BF16), pltpu.VMEM((HEADS_PER_STEP, LANES, seq), BF16),
                        pltpu.VMEM((seq, 2 * TQ), F32), pltpu.VMEM((seq, 2 * TQ), F32),
                        pltpu.VMEM((seq, TQ), BF16), pltpu.VMEM((seq, TQ), BF16)],
        compiler_params=_compiler_params(("parallel", "parallel")),
        name="diff_attn",
    )(bounded, lamv, dq, dk, dv, bias5, bias5_t, subln_g, g_col)


WIN_KEYS = 3 * BLOCK
BLOCKS_PER_ITER = 4


def _win_attn_kernel(bounded_ref, sink_ref, q_ref, k_ref, v_ref, bias_ref, o_ref, s_scr, p_scr, v_ones_scr):
    seq = k_ref.shape[0]
    nb = seq // BLOCK
    group_rows = WIN_GROUP * BLOCK
    lane = lax.broadcasted_iota(jnp.int32, (BLOCK, LANES), 1)
    v_ones_scr[:, :LANES] = v_ref[...]
    v_ones_scr[:, LANES:] = jnp.ones(v_ref.shape, BF16)

    def window(n):
        start = pl.multiple_of(jnp.clip(n * BLOCK - BLOCK, 0, seq - WIN_KEYS), BLOCK)
        variant = jnp.where(n == 0, 0, jnp.where(n == nb - 1, 2, 1))
        return start, variant

    def grouped_q(n, kv):
        q = q_ref[pl.ds(pl.multiple_of(n * BLOCK, BLOCK), BLOCK), :]
        zero = jnp.zeros((BLOCK, LANES), q.dtype)
        keep = (lane < HEAD_DIM) if kv == 0 else (lane >= HEAD_DIM)
        return jnp.concatenate([jnp.where(keep, q[:, j * LANES:(j + 1) * LANES], zero)
                                for j in range(WIN_GROUP)], axis=0)

    def group_logits(n, kv, start):
        return lax.dot_general(grouped_q(n, kv), k_ref[pl.ds(start, WIN_KEYS), :], (((1,), (1,)), ((), ())),
                               preferred_element_type=F32)

    def store(n, outs):
        rows = pl.ds(pl.multiple_of(n * BLOCK, BLOCK), BLOCK)
        for j in range(WIN_GROUP):
            blk = jnp.where(lane < HEAD_DIM, outs[0][j * BLOCK:(j + 1) * BLOCK], outs[1][j * BLOCK:(j + 1) * BLOCK])
            o_ref[rows, j * LANES:(j + 1) * LANES] = blk.astype(BF16)

    def unshifted_weights(n, kv):
        start, variant = window(n)
        grows = slice(kv * group_rows, (kv + 1) * group_rows)
        return jnp.exp2(group_logits(n, kv, start) + bias_ref[variant, grows, :]).astype(BF16)

    def unshifted_output(n, kv, p):
        start, _ = window(n)
        res = jnp.dot(p, v_ones_scr[pl.ds(start, WIN_KEYS), :], preferred_element_type=F32)
        sink_w = jnp.concatenate(
            [jnp.broadcast_to(jnp.exp2(jnp.full((1, LANES), sink_ref[kv * WIN_GROUP + g], F32)), (BLOCK, LANES))
             for g in range(WIN_GROUP)], axis=0)
        return res[:, :LANES] * (1.0 / (res[:, LANES:] + sink_w))

    def unshifted_blocks(i, carry):
        items = [(i * BLOCKS_PER_ITER + u, kv) for u in range(BLOCKS_PER_ITER) for kv in range(WIN_KV_HEADS)]
        p, outs = unshifted_weights(*items[0]), []
        for m, (n, kv) in enumerate(items):
            following = unshifted_weights(*items[m + 1]) if m + 1 < len(items) else None
            outs.append(unshifted_output(n, kv, p))
            if kv == WIN_KV_HEADS - 1:
                store(n, outs)
                outs = []
            p = following
        return carry

    def shifted_block(n, carry):
        start, variant = window(n)
        outs = []
        for kv in range(WIN_KV_HEADS):
            s_scr[...] = group_logits(n, kv, start)
            sink_terms = []
            for r in range(0, group_rows, STRIP):
                rows = slice(r, r + STRIP)
                grows = slice(kv * group_rows + r, kv * group_rows + r + STRIP)
                t = s_scr[rows, :] + bias_ref[variant, grows, :]
                sink = sink_ref[(kv * group_rows + r) // BLOCK]
                row_max = jnp.max(t, axis=-1, keepdims=True)
                p_scr[rows, :] = jnp.exp2(t - jnp.maximum(row_max, sink)).astype(BF16)
                m_wide = jnp.maximum(jnp.broadcast_to(row_max, (STRIP, LANES)), sink)
                sink_terms.append(jnp.exp2(sink - m_wide))
            res = jnp.dot(p_scr[...], v_ones_scr[pl.ds(start, WIN_KEYS), :], preferred_element_type=F32)
            l = res[:, LANES:] + jnp.concatenate(sink_terms, axis=0)
            outs.append(res[:, :LANES] * (1.0 / l))
        store(n, outs)
        return carry

    bounded = bounded_ref[0] != 0

    @pl.when(bounded)
    def _():
        lax.fori_loop(0, nb // BLOCKS_PER_ITER, unshifted_blocks, 0)

    @pl.when(jnp.logical_not(bounded))
    def _():
        lax.fori_loop(0, nb, shifted_block, 0)


def _win_attn(wq, wk, wv, bounded, bias3, sink, batch, seq):
    rows = WIN_HEADS * BLOCK
    smem = pl.BlockSpec(memory_space=pltpu.SMEM)
    seq_block = lambda w: pl.BlockSpec((seq, w), lambda b: (b, 0))
    return pl.pallas_call(
        _win_attn_kernel,
        grid=(batch,),
        in_specs=[smem, smem, seq_block(WIN_Q_W), seq_block(WIN_KV_W), seq_block(WIN_KV_W),
                  _resident((3, rows, WIN_KEYS))],
        out_specs=seq_block(WIN_Q_W),
        out_shape=jax.ShapeDtypeStruct((batch * seq, WIN_Q_W), BF16),
        scratch_shapes=[pltpu.VMEM((rows // 2, WIN_KEYS), F32), pltpu.VMEM((rows // 2, WIN_KEYS), BF16),
                        pltpu.VMEM((seq, 2 * LANES), BF16)],
        compiler_params=_compiler_params(("parallel",)),
        name="win_attn",
    )(bounded, sink, wq, wk, wv, bias3)


def _mix_ffn_kernel(x_ref, xp_ref, xn_ref, oa_ref, oap_ref, oan_ref, ob_ref, obp_ref, obn_ref,
                    wa_ref, wb_ref, g_ref, wg_ref, wu_ref, wd_ref, cw_ref, o_ref, act_scr):
    i = pl.program_id(1)
    last = pl.num_programs(1) - 1
    ext_rows = TM_FFN + 2 * HALO
    tile = slice(HALO, HALO + TM_FFN)
    ext = lambda p, c, n: jnp.concatenate([p[...], c[...], n[...]], axis=0)
    x1 = (ext(xp_ref, x_ref, xn_ref)
          + jnp.dot(ext(oap_ref, oa_ref, oan_ref), wa_ref[...], preferred_element_type=F32)
          + jnp.dot(ext(obp_ref, ob_ref, obn_ref), wb_ref[...], preferred_element_type=F32))
    o_ref[...] = x1[tile]
    ms = jnp.mean(x1 * x1, axis=-1, keepdims=True)
    h_ext = (x1 * lax.rsqrt(ms + EPS) * g_ref[...]).astype(BF16)
    row = lax.broadcasted_iota(jnp.int32, (ext_rows, 1), 0)
    outside = ((row < HALO) & (i == 0)) | ((row >= HALO + TM_FFN) & (i == last))
    h_ext = jnp.where(outside, jnp.zeros_like(h_ext), h_ext)
    h = h_ext[tile]
    for c0 in range(0, D_FF, FF_CHUNK):
        cols = slice(c0, min(c0 + FF_CHUNK, D_FF))
        g_ext = jnp.dot(h_ext, wg_ref[:, cols], preferred_element_type=F32)
        g = g_ext[tile]
        g_dn = pltpu.roll(g_ext, 1, 0)[tile]
        g_up = pltpu.roll(g_ext, ext_rows - 1, 0)[tile]
        cw = cw_ref[:, cols]
        u = g_dn * cw[0:1] + g * cw[1:2] + g_up * cw[2:3] + cw[3:4]
        up = jnp.dot(h, wu_ref[:, cols], preferred_element_type=F32)
        act = (u * (1.0 / (1.0 + jnp.exp(-u)))) * up
        act_scr[:, cols] = act.astype(BF16)
    o_ref[...] = o_ref[...] + jnp.dot(act_scr[...], wd_ref[...], preferred_element_type=F32)


def _mix_ffn(x2d, oa, ob, wa, wb, g_ffn, wg, wu, wd, cw, batch, seq):
    nt = seq // TM_FFN
    hb = TM_FFN // HALO
    n_halo = batch * seq // HALO

    def with_halo(width):
        return [pl.BlockSpec((TM_FFN, width), lambda b, i: (b * nt + i, 0)),
                pl.BlockSpec((HALO, width), lambda b, i: (jnp.maximum((b * nt + i) * hb - 1, 0), 0)),
                pl.BlockSpec((HALO, width), lambda b, i: (jnp.minimum((b * nt + i + 1) * hb, n_halo - 1), 0))]

    return pl.pallas_call(
        _mix_ffn_kernel,
        grid=(batch, nt),
        in_specs=(with_halo(D_MODEL) + with_halo(DIFF_W) + with_halo(WIN_Q_W)
                  + [_resident((DIFF_W, D_MODEL)), _resident((WIN_Q_W, D_MODEL)), _resident((1, D_MODEL)),
                     _resident((D_MODEL, D_FF)), _resident((D_MODEL, D_FF)), _resident((D_FF, D_MODEL)),
                     _resident((4, D_FF))]),
        out_specs=pl.BlockSpec((TM_FFN, D_MODEL), lambda b, i: (b * nt + i, 0)),
        out_shape=jax.ShapeDtypeStruct((batch * seq, D_MODEL), F32),
        scratch_shapes=[pltpu.VMEM((TM_FFN, D_FF), BF16)],
        compiler_params=_compiler_params(("parallel", "arbitrary")),
        name="mix_ffn",
    )(x2d, x2d, x2d, oa, oa, oa, ob, ob, ob, wa, wb, g_ffn, wg, wu, wd, cw)


SKEW_W = 1024


def _bias_tables_kernel(far_ref, fwd_ref, bwd_ref, win_vec_ref, bias_ref, bias_t_ref, win_ref):
    def skew(row, rows):
        return pltpu.roll(jnp.broadcast_to(row, (rows, SKEW_W)), 0, 1, stride=1, stride_axis=0)

    for h in range(DIFF_HEADS):
        near = skew(fwd_ref[h:h + 1, :], TQ)
        near_t = skew(bwd_ref[h:h + 1, :], 3 * TK)
        for d in range(3):
            bias_ref[h, d + 1] = near[:, d * TK:(d + 1) * TK]
            bias_t_ref[h, d + 1] = near_t[d * TK:(d + 1) * TK, :TQ]
        for d in (0, 4):
            bias_ref[h, d] = jnp.full((TQ, TK), far_ref[h, d // 4], F32)
            bias_t_ref[h, d] = jnp.full((TK, TQ), far_ref[h, d // 4], F32)
    for h in range(WIN_HEADS):
        band = skew(win_vec_ref[h:h + 1, :], BLOCK)
        for v in range(3):
            win_ref[v, h * BLOCK:(h + 1) * BLOCK, :] = band[:, (2 - v) * BLOCK:(2 - v) * BLOCK + WIN_KEYS]


def _wrapped_offsets(last):
    idx = jnp.arange(SKEW_W)
    return jnp.where(idx <= last, idx, idx - SKEW_W)


def _bias_tables(rel_bias):
    table = rel_bias.astype(F32).T * LOG2E
    hi, lo = jnp.max(table[:DIFF_HEADS], axis=1), jnp.min(table[:DIFF_HEADS], axis=1)
    centred = table[:DIFF_HEADS] - (0.5 * (hi + lo))[:, None]
    fwd = centred[:, _rel_bucket(_wrapped_offsets(3 * TK - 1) - TK)]
    bwd = centred[:, _rel_bucket(-_wrapped_offsets(TQ - 1) - TK)]
    far = centred[:, _rel_bucket(jnp.array([-2 * TK, 2 * TK]))]
    rel = _wrapped_offsets(5 * BLOCK - 1) - 2 * BLOCK
    win_vec = jnp.where(jnp.abs(rel) <= WINDOW, table[DIFF_HEADS:][:, _rel_bucket(rel)], NEG_INF * LOG2E)
    whole = lambda shape: pl.BlockSpec(shape, lambda: (0,) * len(shape))
    tile_shape = (DIFF_HEADS, 5, TQ, TK)
    win_shape = (3, WIN_HEADS * BLOCK, WIN_KEYS)
    diff, diff_t, win = pl.pallas_call(
        _bias_tables_kernel,
        in_specs=[pl.BlockSpec(memory_space=pltpu.SMEM), whole((DIFF_HEADS, SKEW_W)), whole((DIFF_HEADS, SKEW_W)),
                  whole((WIN_HEADS, SKEW_W))],
        out_specs=[whole(tile_shape), whole(tile_shape), whole(win_shape)],
        out_shape=[jax.ShapeDtypeStruct(tile_shape, F32), jax.ShapeDtypeStruct(tile_shape, F32),
                   jax.ShapeDtypeStruct(win_shape, F32)],
        compiler_params=pltpu.CompilerParams(vmem_limit_bytes=VMEM_LIMIT),
        name="bias_tables",
    )(far, fwd, bwd, win_vec)
    return dict(diff=diff, diff_t=diff_t, diff_half_range=jnp.max(0.5 * (hi - lo)), win=win,
                win_hi=jnp.max(table[DIFF_HEADS:], axis=1), win_lo=jnp.min(table[DIFF_HEADS:], axis=1))


def _qk_bound(q_gain, k_gain):
    return HEAD_DIM * NORM_SLACK * jnp.max(jnp.abs(q_gain)) * jnp.max(jnp.abs(k_gain))


def _is_bounded(half_range):
    return (half_range <= MAX_UNSHIFTED_LOGIT).astype(jnp.int32).reshape(1)


def _layer_params(l, tables, norm_attn_g, w_in, diff_q_norm_g, diff_k_norm_g, diff_lambda_q1, diff_lambda_k1,
                  diff_lambda_q2, diff_lambda_k2, diff_subln_g, win_q_norm_g, win_k_norm_g, win_sink,
                  w_out, norm_ffn_g, w_gate, w_up, conv_w, conv_b, w_down):
    scale = HEAD_DIM ** -0.5
    w = w_in[l]
    dq, dk, dv = w[:, :DIFF_W], w[:, DIFF_W:2 * DIFF_W], w[:, 2 * DIFF_W:3 * DIFF_W]
    wq = w[:, 3 * DIFF_W:3 * DIFF_W + WIN_Q_W].reshape(D_MODEL, WIN_KV_HEADS, WIN_GROUP, HEAD_DIM)
    wq = wq.transpose(0, 2, 1, 3).reshape(D_MODEL, WIN_Q_W)
    wk = w[:, 3 * DIFF_W + WIN_Q_W:3 * DIFF_W + WIN_Q_W + WIN_KV_W]
    wv = w[:, 3 * DIFF_W + WIN_Q_W + WIN_KV_W:]
    w_in_p = jnp.concatenate([dq, dk, wq, wk, dv, wv], axis=1).astype(BF16)
    gvec = jnp.concatenate([jnp.tile(diff_q_norm_g[l], 2 * DIFF_HEADS) * (scale * LOG2E),
                            jnp.tile(diff_k_norm_g[l], 2 * DIFF_HEADS),
                            jnp.tile(win_q_norm_g[l], WIN_HEADS) * (scale * LOG2E),
                            jnp.tile(win_k_norm_g[l], WIN_KV_HEADS)]).reshape(1, NORMED_W).astype(F32)
    lamv = jnp.zeros((8, LANES), F32)
    lamv = lamv.at[0:4, :HEAD_DIM].set(jnp.stack([diff_lambda_q1[l], diff_lambda_k1[l],
                                                  diff_lambda_q2[l], diff_lambda_k2[l]]).astype(F32))
    wo = w_out[l]
    wb = wo[DIFF_W:].reshape(WIN_KV_HEADS, WIN_GROUP, HEAD_DIM, D_MODEL)
    wb = wb.transpose(1, 0, 2, 3).reshape(WIN_Q_W, D_MODEL)
    cw = jnp.concatenate([conv_w[l], conv_b[l][None]], axis=0).astype(F32)
    return dict(
        g_attn=norm_attn_g[l].reshape(1, D_MODEL).astype(F32), w_in_p=w_in_p, gvec=gvec, lamv=lamv,
        subln_g=diff_subln_g[l].reshape(1, 2 * HEAD_DIM).astype(F32),
        wa=wo[:DIFF_W].astype(BF16), wb=wb.astype(BF16),
        g_ffn=norm_ffn_g[l].reshape(1, D_MODEL).astype(F32),
        wg=w_gate[l].astype(BF16), wu=w_up[l].astype(BF16), wd=w_down[l].astype(BF16), cw=cw,
        lam_init=0.8 - 0.6 * math.exp(-0.3 * l),
        diff_bounded=_is_bounded(_qk_bound(diff_q_norm_g[l].astype(F32) * (scale * LOG2E), diff_k_norm_g[l])
                                 + tables["diff_half_range"]),
        **_window_softmax_params(tables, win_q_norm_g[l].astype(F32) * (scale * LOG2E), win_k_norm_g[l],
                                 win_sink[l].astype(F32) * LOG2E),
    )


def _window_softmax_params(tables, q_gain, k_gain, sink):
    qk = _qk_bound(q_gain, k_gain)
    hi = jnp.maximum(tables["win_hi"] + qk, sink)
    lo = jnp.minimum(tables["win_lo"] - qk, sink)
    centre = 0.5 * (hi + lo)
    return dict(win_bias=tables["win"] - jnp.repeat(centre, BLOCK)[None, :, None], win_sink=sink - centre,
                win_bounded=_is_bounded(jnp.max(0.5 * (hi - lo))))


def _encoder_layer(x, p, bias5, bias5_t):
    batch, seq, _ = x.shape
    x2d = x.reshape(batch * seq, D_MODEL)
    dq, dk, wq, wk, dv, wv = _in_proj(x2d, p["g_attn"], p["w_in_p"], p["gvec"])
    oa = _diff_attn(dq, dk, dv, p["diff_bounded"], p["lamv"], bias5, bias5_t, p["subln_g"], p["lam_init"],
                    batch, seq)
    ob = _win_attn(wq, wk, wv, p["win_bounded"], p["win_bias"], p["win_sink"], batch, seq)
    y = _mix_ffn(x2d, oa, ob, p["wa"], p["wb"], p["g_ffn"], p["wg"], p["wu"], p["wd"], p["cw"], batch, seq)
    return y.reshape(batch, seq, D_MODEL)


def kernel(x_prompt, x_sample, norm_attn_g, w_in, diff_q_norm_g, diff_k_norm_g, diff_lambda_q1, diff_lambda_k1, diff_lambda_q2, diff_lambda_k2, diff_subln_g, win_q_norm_g, win_k_norm_g, win_sink, rel_bias, w_out, norm_ffn_g, w_gate, w_up, conv_w, conv_b, w_down):
    depth = w_in.shape[0]
    tables = _bias_tables(rel_bias)
    layers = [_layer_params(l, tables, norm_attn_g, w_in, diff_q_norm_g, diff_k_norm_g, diff_lambda_q1,
                            diff_lambda_k1, diff_lambda_q2, diff_lambda_k2, diff_subln_g, win_q_norm_g,
                            win_k_norm_g, win_sink, w_out, norm_ffn_g, w_gate, w_up, conv_w, conv_b, w_down)
              for l in range(depth)]

    def run(x):
        for p in layers:
            x = _encoder_layer(x, p, tables["diff"], tables["diff_t"])
        return x

    return run(x_prompt), run(x_sample)
```

```python
import functools
import math

import jax
import jax.numpy as jnp
from jax import lax
from jax.experimental import pallas as pl
from jax.experimental.pallas import tpu as pltpu

D_MODEL = 1024
HEAD_DIM = 64
DIFF_HEADS = 4
WIN_HEADS = 8
WIN_KV_HEADS = 2
WIN_GROUP = WIN_HEADS // WIN_KV_HEADS
WINDOW = 128
BLOCK = 128
N_BUCKETS = 32
MAX_DISTANCE = 128
D_FF = 2816
EPS = 1e-6
NEG_INF = -1e30

DIFF_W = DIFF_HEADS * 2 * HEAD_DIM
WIN_Q_W = WIN_HEADS * HEAD_DIM
WIN_KV_W = WIN_KV_HEADS * HEAD_DIM
D_IN = 3 * DIFF_W + WIN_Q_W + 2 * WIN_KV_W
NORMED_W = 2 * DIFF_W + WIN_Q_W + WIN_KV_W

LANES = 128
VMEM_LIMIT = 56 * 1024 * 1024

TM_PROJ = 1024
TQ = 256
TK = 256
HEADS_PER_STEP = 1
STRIP = 16
LOG2E = math.log2(math.e)
MAX_UNSHIFTED_LOGIT = 100.0
NORM_SLACK = 1.02
TM_FFN = 1024
FF_CHUNK = 512
HALO = 16

BF16 = jnp.bfloat16
F32 = jnp.float32


def _rel_bucket(rel):
    nb = N_BUCKETS // 2
    max_exact = nb // 2
    ret = jnp.where(rel > 0, nb, 0)
    n = jnp.abs(rel)
    nf = jnp.maximum(n, 1).astype(F32)
    large = max_exact + (jnp.log(nf / max_exact) / math.log(MAX_DISTANCE / max_exact)
                         * (nb - max_exact)).astype(jnp.int32)
    large = jnp.minimum(large, nb - 1)
    return ret + jnp.where(n < max_exact, n, large)


def _compiler_params(semantics):
    return pltpu.CompilerParams(dimension_semantics=semantics, vmem_limit_bytes=VMEM_LIMIT)


def _resident(shape):
    nd = len(shape)
    return pl.BlockSpec(shape, lambda *_: (0,) * nd, pipeline_mode=pl.Buffered(1))


def _in_proj_kernel(x_ref, g_ref, w_ref, gv_ref, dq_ref, dk_ref, wq_ref, wk_ref, dv_ref, wv_ref):
    x = x_ref[...]
    ms = jnp.mean(x * x, axis=-1, keepdims=True)
    h = (x * lax.rsqrt(ms + EPS) * g_ref[...]).astype(BF16)
    proj = jnp.dot(h, w_ref[...], preferred_element_type=F32)
    first_head = lax.broadcasted_iota(jnp.int32, (TM_PROJ, LANES), 1) < HEAD_DIM

    def head_norm(c0):
        blk = proj[:, c0:c0 + LANES]
        sq = blk * blk
        lo = jnp.sum(jnp.where(first_head, sq, 0.0), axis=-1, keepdims=True)
        hi = jnp.sum(sq, axis=-1, keepdims=True) - lo
        ss = jnp.where(first_head, lo, hi)
        return (blk * lax.rsqrt(ss * (1.0 / HEAD_DIM) + EPS) * gv_ref[:, c0:c0 + LANES]).astype(BF16)

    for ref, base, width in ((dq_ref, 0, DIFF_W), (dk_ref, DIFF_W, DIFF_W), (wq_ref, 2 * DIFF_W, WIN_Q_W),
                             (wk_ref, 2 * DIFF_W + WIN_Q_W, WIN_KV_W)):
        for c in range(0, width, LANES):
            ref[:, c:c + LANES] = head_norm(base + c)
    dv_ref[...] = proj[:, NORMED_W:NORMED_W + DIFF_W].astype(BF16)
    wv_ref[...] = proj[:, NORMED_W + DIFF_W:].astype(BF16)


def _in_proj(x2d, g_attn, w_in_p, gvec):
    t = x2d.shape[0]
    tile = lambda w: pl.BlockSpec((TM_PROJ, w), lambda i: (i, 0))
    widths = (DIFF_W, DIFF_W, WIN_Q_W, WIN_KV_W, DIFF_W, WIN_KV_W)
    return pl.pallas_call(
        _in_proj_kernel,
        grid=(t // TM_PROJ,),
        in_specs=[tile(D_MODEL), _resident((1, D_MODEL)), _resident((D_MODEL, D_IN)),
                  _resident((1, NORMED_W))],
        out_specs=[tile(w) for w in widths],
        out_shape=[jax.ShapeDtypeStruct((t, w), BF16) for w in widths],
        compiler_params=_compiler_params(("parallel",)),
        name="in_proj",
    )(x2d, g_attn, w_in_p, gvec)


def _diff_attn_kernel(lam_init, bounded_ref, lamv_ref, q_ref, k_ref, v_ref, bias_ref, bias_t_ref, g_ref,
                      g_col_ref, o_ref, s_scr, p_scr, v_ones_scr, vt_scr, pt_scr, pt_alt_scr, at_scr, at_alt_scr):
    seq = k_ref.shape[0]
    nq, nk = seq // TQ, seq // TK
    lv = lamv_ref[...]
    lam = (jnp.exp(jnp.sum(lv[0:1] * lv[1:2], axis=-1, keepdims=True))
           - jnp.exp(jnp.sum(lv[2:3] * lv[3:4], axis=-1, keepdims=True)) + lam_init)
    lane = lax.broadcasted_iota(jnp.int32, (TQ, LANES), 1)
    reload_offset = jnp.maximum(bounded_ref[0], 1) - 1
    head_cols = [slice(hd * LANES, (hd + 1) * LANES) for hd in range(HEADS_PER_STEP)]

    def stacked_q(hd, tile):
        start = tile * TQ if isinstance(tile, int) else pl.multiple_of(tile * TQ, TQ)
        q = q_ref[pl.ds(start, TQ), head_cols[hd]]
        zero = jnp.zeros_like(q)
        return jnp.concatenate([jnp.where(lane < HEAD_DIM, q, zero), jnp.where(lane >= HEAD_DIM, q, zero)], axis=0)

    def bias_class(tile, kj):
        return min(max(kj - tile, -2), 2) + 2

    def far_factor(hd, which):
        f = jnp.exp2(bias_t_ref[hd, which, 0:1, :])
        return jnp.concatenate([f, f], axis=1)

    def unshifted_weights(hd, tile, pt_ref):
        qs = stacked_q(hd, tile)
        sums = {}
        for kj in range(nk):
            rows = slice(kj * TK, (kj + 1) * TK)
            cls = bias_class(tile, kj)
            st = lax.dot_general(k_ref[rows, head_cols[hd]], qs, (((1,), (1,)), ((), ())),
                                 preferred_element_type=F32)
            if cls in (0, 4):
                pt = jnp.exp2(st)
            else:
                bt = bias_t_ref[hd, cls]
                pt = jnp.exp2(st + jnp.concatenate([bt, bt], axis=1))
                cls = 2
            pt_ref[rows, :] = pt
            part = jnp.sum(pt.reshape(TK // 8, 8, 2 * TQ), axis=0)
            sums[cls] = sums[cls] + part if cls in sums else part
        return sums

    def unshifted_output(hd, tile, sums, pt_ref, at_ref):
        factors = {2: None, 0: far_factor(hd, 0), 4: far_factor(hd, 4)}
        key_sums = sum(s if factors[c] is None else s * factors[c] for c, s in sums.items())
        r = 1.0 / jnp.sum(key_sums, axis=0, keepdims=True)
        r = jnp.concatenate([r[:, :TQ], lam * r[:, TQ:]], axis=1)
        scales = {c: r if f is None else r * f for c, f in factors.items()}
        for kj in range(nk):
            rows = slice(kj * TK, (kj + 1) * TK)
            cls = bias_class(tile, kj)
            scale = scales[cls if cls in (0, 4) else 2]
            back = pl.ds(pl.multiple_of(kj * TK + reload_offset, TK), TK)
            at_ref[rows, :] = (pt_ref[back, :TQ] * scale[:, :TQ] - pt_ref[back, TQ:] * scale[:, TQ:]).astype(BF16)
        ot = jnp.dot(vt_scr[hd], at_ref[...], preferred_element_type=F32)
        ms = jnp.mean(ot * ot, axis=0, keepdims=True)
        gcol = jnp.concatenate([g_col_ref[...]] * (TQ // LANES), axis=1)
        ot = ot * lax.rsqrt(ms + EPS) * gcol * (1.0 - lam_init)
        o_ref[tile * TQ:(tile + 1) * TQ, head_cols[hd]] = ot.T.astype(BF16)

    def unshifted_sequence():
        pt_bufs, at_bufs = (pt_scr, pt_alt_scr), (at_scr, at_alt_scr)
        items = [(hd, tile) for hd in range(HEADS_PER_STEP) for tile in range(nq)]
        sums = unshifted_weights(*items[0], pt_bufs[0])
        for n, item in enumerate(items):
            following = unshifted_weights(*items[n + 1], pt_bufs[(n + 1) % 2]) if n + 1 < len(items) else None
            unshifted_output(*item, sums, pt_bufs[n % 2], at_bufs[n % 2])
            sums = following

    def shifted_head(hd):
        v_ones_scr[:, :LANES] = v_ref[:, head_cols[hd]]
        v_ones_scr[:, LANES:] = jnp.ones((seq, LANES), BF16)

        def shifted_tile(tile, carry):
            qs = stacked_q(hd, tile)
            tiles = [jnp.clip(kj - tile, -2, 2) + 2 for kj in range(nk)]
            for kj in range(nk):
                s_scr[:, kj * TK:(kj + 1) * TK] = lax.dot_general(
                    qs, k_ref[kj * TK:(kj + 1) * TK, head_cols[hd]], (((1,), (1,)), ((), ())),
                    preferred_element_type=F32)

            def strip(i, c):
                for mp in range(2):
                    rows = pl.ds(pl.multiple_of(mp * TQ + i * STRIP, STRIP), STRIP)
                    brows = pl.ds(pl.multiple_of(i * STRIP, STRIP), STRIP)
                    t = jnp.concatenate([s_scr[rows, kj * TK:(kj + 1) * TK] + bias_ref[hd, tiles[kj], brows, :]
                                         for kj in range(nk)], axis=1)
                    p_scr[rows, :] = jnp.exp2(t - jnp.max(t, axis=-1, keepdims=True)).astype(BF16)
                return c

            lax.fori_loop(0, TQ // STRIP, strip, 0)
            res = jnp.dot(p_scr[...], v_ones_scr[...], preferred_element_type=F32)
            out = res[:, :LANES] * (1.0 / res[:, LANES:])
            o = out[:TQ] - lam * out[TQ:]
            ms = jnp.mean(o * o, axis=-1, keepdims=True)
            o = o * lax.rsqrt(ms + EPS) * g_ref[...] * (1.0 - lam_init)
            o_ref[pl.ds(pl.multiple_of(tile * TQ, TQ), TQ), head_cols[hd]] = o.astype(BF16)
            return carry

        lax.fori_loop(0, nq, shifted_tile, 0)

    bounded = bounded_ref[0] != 0

    @pl.when(bounded)
    def _():
        for hd in range(HEADS_PER_STEP):
            vt_scr[hd] = v_ref[:, head_cols[hd]].astype(F32).T.astype(BF16)
        unshifted_sequence()

    @pl.when(jnp.logical_not(bounded))
    def _():
        for hd in range(HEADS_PER_STEP):
            shifted_head(hd)


def _diff_attn(dq, dk, dv, bounded, lamv, bias5, bias5_t, subln_g, lam_init, batch, seq):
    seq_block = pl.BlockSpec((seq, HEADS_PER_STEP * LANES), lambda b, h: (b, h))
    bias_block = pl.BlockSpec((HEADS_PER_STEP, 5, TQ, TK), lambda b, h: (h, 0, 0, 0))
    g_col = jnp.broadcast_to(subln_g.reshape(LANES, 1), (LANES, LANES))
    return pl.pallas_call(
        functools.partial(_diff_attn_kernel, lam_init),
        grid=(batch, DIFF_HEADS // HEADS_PER_STEP),
        in_specs=[
            pl.BlockSpec(memory_space=pltpu.SMEM),
            pl.BlockSpec((8, LANES), lambda b, h: (0, 0)),
            seq_block, seq_block, seq_block,
            bias_block, bias_block,
            pl.BlockSpec((1, LANES), lambda b, h: (0, 0)),
            pl.BlockSpec((LANES, LANES), lambda b, h: (0, 0)),
        ],
        out_specs=seq_block,
        out_shape=jax.ShapeDtypeStruct((batch * seq, DIFF_W), BF16),
        scratch_shapes=[pltpu.VMEM((2 * TQ, seq), F32), pltpu.VMEM((2 * TQ, seq), BF16),
                        pltpu.VMEM((seq, 2 * LANES), BF16), pltpu.VMEM((HEADS_PER_STEP, LANES, seq), BF16),
                        pltpu.VMEM((seq, 2 * TQ), F32), pltpu.VMEM((seq, 2 * TQ), F32),
                        pltpu.VMEM((seq, TQ), BF16), pltpu.VMEM((seq, TQ), BF16)],
        compiler_params=_compiler_params(("parallel", "parallel")),
        name="diff_attn",
    )(bounded, lamv, dq, dk, dv, bias5, bias5_t, subln_g, g_col)


WIN_KEYS = 3 * BLOCK
BLOCKS_PER_ITER = 4


def _win_attn_kernel(bounded_ref, sink_ref, q_ref, k_ref, v_ref, bias_ref, o_ref, s_scr, p_scr, v_ones_scr):
    seq = k_ref.shape[0]
    nb = seq // BLOCK
    group_rows = WIN_GROUP * BLOCK
    lane = lax.broadcasted_iota(jnp.int32, (BLOCK, LANES), 1)
    v_ones_scr[:, :LANES] = v_ref[...]
    v_ones_scr[:, LANES:] = jnp.ones(v_ref.shape, BF16)

    def window(n):
        start = pl.multiple_of(jnp.clip(n * BLOCK - BLOCK, 0, seq - WIN_KEYS), BLOCK)
        variant = jnp.where(n == 0, 0, jnp.where(n == nb - 1, 2, 1))
        return start, variant

    def grouped_q(n, kv):
        q = q_ref[pl.ds(pl.multiple_of(n * BLOCK, BLOCK), BLOCK), :]
        zero = jnp.zeros((BLOCK, LANES), q.dtype)
        keep = (lane < HEAD_DIM) if kv == 0 else (lane >= HEAD_DIM)
        return jnp.concatenate([jnp.where(keep, q[:, j * LANES:(j + 1) * LANES], zero)
                                for j in range(WIN_GROUP)], axis=0)

    def group_logits(n, kv, start):
        return lax.dot_general(grouped_q(n, kv), k_ref[pl.ds(start, WIN_KEYS), :], (((1,), (1,)), ((), ())),
                               preferred_element_type=F32)

    def store(n, outs):
        rows = pl.ds(pl.multiple_of(n * BLOCK, BLOCK), BLOCK)
        for j in range(WIN_GROUP):
            blk = jnp.where(lane < HEAD_DIM, outs[0][j * BLOCK:(j + 1) * BLOCK], outs[1][j * BLOCK:(j + 1) * BLOCK])
            o_ref[rows, j * LANES:(j + 1) * LANES] = blk.astype(BF16)

    def unshifted_weights(n, kv):
        start, variant = window(n)
        grows = slice(kv * group_rows, (kv + 1) * group_rows)
        return jnp.exp2(group_logits(n, kv, start) + bias_ref[variant, grows, :]).astype(BF16)

    def unshifted_output(n, kv, p):
        start, _ = window(n)
        res = jnp.dot(p, v_ones_scr[pl.ds(start, WIN_KEYS), :], preferred_element_type=F32)
        sink_w = jnp.concatenate(
            [jnp.broadcast_to(jnp.exp2(jnp.full((1, LANES), sink_ref[kv * WIN_GROUP + g], F32)), (BLOCK, LANES))
             for g in range(WIN_GROUP)], axis=0)
        return res[:, :LANES] * (1.0 / (res[:, LANES:] + sink_w))

    def unshifted_blocks(i, carry):
        items = [(i * BLOCKS_PER_ITER + u, kv) for u in range(BLOCKS_PER_ITER) for kv in range(WIN_KV_HEADS)]
        p, outs = unshifted_weights(*items[0]), []
        for m, (n, kv) in enumerate(items):
            following = unshifted_weights(*items[m + 1]) if m + 1 < len(items) else None
            outs.append(unshifted_output(n, kv, p))
            if kv == WIN_KV_HEADS - 1:
                store(n, outs)
                outs = []
            p = following
        return carry

    def shifted_block(n, carry):
        start, variant = window(n)
        outs = []
        for kv in range(WIN_KV_HEADS):
            s_scr[...] = group_logits(n, kv, start)
            sink_terms = []
            for r in range(0, group_rows, STRIP):
                rows = slice(r, r + STRIP)
                grows = slice(kv * group_rows + r, kv * group_rows + r + STRIP)
                t = s_scr[rows, :] + bias_ref[variant, grows, :]
                sink = sink_ref[(kv * group_rows + r) // BLOCK]
                row_max = jnp.max(t, axis=-1, keepdims=True)
                p_scr[rows, :] = jnp.exp2(t - jnp.maximum(row_max, sink)).astype(BF16)
                m_wide = jnp.maximum(jnp.broadcast_to(row_max, (STRIP, LANES)), sink)
                sink_terms.append(jnp.exp2(sink - m_wide))
            res = jnp.dot(p_scr[...], v_ones_scr[pl.ds(start, WIN_KEYS), :], preferred_element_type=F32)
            l = res[:, LANES:] + jnp.concatenate(sink_terms, axis=0)
            outs.append(res[:, :LANES] * (1.0 / l))
        store(n, outs)
        return carry

    bounded = bounded_ref[0] != 0

    @pl.when(bounded)
    def _():
        lax.fori_loop(0, nb // BLOCKS_PER_ITER, unshifted_blocks, 0)

    @pl.when(jnp.logical_not(bounded))
    def _():
        lax.fori_loop(0, nb, shifted_block, 0)


def _win_attn(wq, wk, wv, bounded, bias3, sink, batch, seq):
    rows = WIN_HEADS * BLOCK
    smem = pl.BlockSpec(memory_space=pltpu.SMEM)
    seq_block = lambda w: pl.BlockSpec((seq, w), lambda b: (b, 0))
    return pl.pallas_call(
        _win_attn_kernel,
        grid=(batch,),
        in_specs=[smem, smem, seq_block(WIN_Q_W), seq_block(WIN_KV_W), seq_block(WIN_KV_W),
                  _resident((3, rows, WIN_KEYS))],
        out_specs=seq_block(WIN_Q_W),
        out_shape=jax.ShapeDtypeStruct((batch * seq, WIN_Q_W), BF16),
        scratch_shapes=[pltpu.VMEM((rows // 2, WIN_KEYS), F32), pltpu.VMEM((rows // 2, WIN_KEYS), BF16),
                        pltpu.VMEM((seq, 2 * LANES), BF16)],
        compiler_params=_compiler_params(("parallel",)),
        name="win_attn",
    )(bounded, sink, wq, wk, wv, bias3)


def _mix_ffn_kernel(x_ref, xp_ref, xn_ref, oa_ref, oap_ref, oan_ref, ob_ref, obp_ref, obn_ref,
                    wa_ref, wb_ref, g_ref, wg_ref, wu_ref, wd_ref, cw_ref, o_ref, act_scr):
    i = pl.program_id(1)
    last = pl.num_programs(1) - 1
    ext_rows = TM_FFN + 2 * HALO
    tile = slice(HALO, HALO + TM_FFN)
    ext = lambda p, c, n: jnp.concatenate([p[...], c[...], n[...]], axis=0)
    x1 = (ext(xp_ref, x_ref, xn_ref)
          + jnp.dot(ext(oap_ref, oa_ref, oan_ref), wa_ref[...], preferred_element_type=F32)
          + jnp.dot(ext(obp_ref, ob_ref, obn_ref), wb_ref[...], preferred_element_type=F32))
    o_ref[...] = x1[tile]
    ms = jnp.mean(x1 * x1, axis=-1, keepdims=True)
    h_ext = (x1 * lax.rsqrt(ms + EPS) * g_ref[...]).astype(BF16)
    row = lax.broadcasted_iota(jnp.int32, (ext_rows, 1), 0)
    outside = ((row < HALO) & (i == 0)) | ((row >= HALO + TM_FFN) & (i == last))
    h_ext = jnp.where(outside, jnp.zeros_like(h_ext), h_ext)
    h = h_ext[tile]
    for c0 in range(0, D_FF, FF_CHUNK):
        cols = slice(c0, min(c0 + FF_CHUNK, D_FF))
        g_ext = jnp.dot(h_ext, wg_ref[:, cols], preferred_element_type=F32)
        g = g_ext[tile]
        g_dn = pltpu.roll(g_ext, 1, 0)[tile]
        g_up = pltpu.roll(g_ext, ext_rows - 1, 0)[tile]
        cw = cw_ref[:, cols]
        u = g_dn * cw[0:1] + g * cw[1:2] + g_up * cw[2:3] + cw[3:4]
        up = jnp.dot(h, wu_ref[:, cols], preferred_element_type=F32)
        act = (u * (1.0 / (1.0 + jnp.exp(-u)))) * up
        act_scr[:, cols] = act.astype(BF16)
    o_ref[...] = o_ref[...] + jnp.dot(act_scr[...], wd_ref[...], preferred_element_type=F32)


def _mix_ffn(x2d, oa, ob, wa, wb, g_ffn, wg, wu, wd, cw, batch, seq):
    nt = seq // TM_FFN
    hb = TM_FFN // HALO
    n_halo = batch * seq // HALO

    def with_halo(width):
        return [pl.BlockSpec((TM_FFN, width), lambda b, i: (b * nt + i, 0)),
                pl.BlockSpec((HALO, width), lambda b, i: (jnp.maximum((b * nt + i) * hb - 1, 0), 0)),
                pl.BlockSpec((HALO, width), lambda b, i: (jnp.minimum((b * nt + i + 1) * hb, n_halo - 1), 0))]

    return pl.pallas_call(
        _mix_ffn_kernel,
        grid=(batch, nt),
        in_specs=(with_halo(D_MODEL) + with_halo(DIFF_W) + with_halo(WIN_Q_W)
                  + [_resident((DIFF_W, D_MODEL)), _resident((WIN_Q_W, D_MODEL)), _resident((1, D_MODEL)),
                     _resident((D_MODEL, D_FF)), _resident((D_MODEL, D_FF)), _resident((D_FF, D_MODEL)),
                     _resident((4, D_FF))]),
        out_specs=pl.BlockSpec((TM_FFN, D_MODEL), lambda b, i: (b * nt + i, 0)),
        out_shape=jax.ShapeDtypeStruct((batch * seq, D_MODEL), F32),
        scratch_shapes=[pltpu.VMEM((TM_FFN, D_FF), BF16)],
        compiler_params=_compiler_params(("parallel", "arbitrary")),
        name="mix_ffn",
    )(x2d, x2d, x2d, oa, oa, oa, ob, ob, ob, wa, wb, g_ffn, wg, wu, wd, cw)


SKEW_W = 1024


def _bias_tables_kernel(far_ref, fwd_ref, bwd_ref, win_vec_ref, bias_ref, bias_t_ref, win_ref):
    def skew(row, rows):
        return pltpu.roll(jnp.broadcast_to(row, (rows, SKEW_W)), 0, 1, stride=1, stride_axis=0)

    for h in range(DIFF_HEADS):
        near = skew(fwd_ref[h:h + 1, :], TQ)
        near_t = skew(bwd_ref[h:h + 1, :], 3 * TK)
        for d in range(3):
            bias_ref[h, d + 1] = near[:, d * TK:(d + 1) * TK]
            bias_t_ref[h, d + 1] = near_t[d * TK:(d + 1) * TK, :TQ]
        for d in (0, 4):
            bias_ref[h, d] = jnp.full((TQ, TK), far_ref[h, d // 4], F32)
            bias_t_ref[h, d] = jnp.full((TK, TQ), far_ref[h, d // 4], F32)
    for h in range(WIN_HEADS):
        band = skew(win_vec_ref[h:h + 1, :], BLOCK)
        for v in range(3):
            win_ref[v, h * BLOCK:(h + 1) * BLOCK, :] = band[:, (2 - v) * BLOCK:(2 - v) * BLOCK + WIN_KEYS]


def _wrapped_offsets(last):
    idx = jnp.arange(SKEW_W)
    return jnp.where(idx <= last, idx, idx - SKEW_W)


def _bias_ranges(rel_bias):
    table = rel_bias.astype(F32).T * LOG2E
    return dict(table=table, hi=jnp.max(table, axis=1), lo=jnp.min(table, axis=1))


def _bias_tables(ranges, win_centre):
    centre = jnp.concatenate([0.5 * (ranges["hi"] + ranges["lo"])[:DIFF_HEADS], win_centre])
    centred = ranges["table"] - centre[:, None]
    rel_fwd = _wrapped_offsets(3 * TK - 1) - TK
    rel_bwd = -_wrapped_offsets(TQ - 1) - TK
    rel_win = _wrapped_offsets(5 * BLOCK - 1) - 2 * BLOCK
    rel_far = jnp.array([-2 * TK, 2 * TK])
    looked_up = centred[:, _rel_bucket(jnp.concatenate([rel_fwd, rel_bwd, rel_win, rel_far]))]
    fwd, bwd = looked_up[:DIFF_HEADS, :SKEW_W], looked_up[:DIFF_HEADS, SKEW_W:2 * SKEW_W]
    far = looked_up[:DIFF_HEADS, 3 * SKEW_W:]
    win_vec = jnp.where(jnp.abs(rel_win) <= WINDOW, looked_up[DIFF_HEADS:, 2 * SKEW_W:3 * SKEW_W], NEG_INF)
    whole = lambda shape: pl.BlockSpec(shape, lambda: (0,) * len(shape))
    tile_shape = (DIFF_HEADS, 5, TQ, TK)
    win_shape = (3, WIN_HEADS * BLOCK, WIN_KEYS)
    diff, diff_t, win = pl.pallas_call(
        _bias_tables_kernel,
        in_specs=[pl.BlockSpec(memory_space=pltpu.SMEM), whole((DIFF_HEADS, SKEW_W)), whole((DIFF_HEADS, SKEW_W)),
                  whole((WIN_HEADS, SKEW_W))],
        out_specs=[whole(tile_shape), whole(tile_shape), whole(win_shape)],
        out_shape=[jax.ShapeDtypeStruct(tile_shape, F32), jax.ShapeDtypeStruct(tile_shape, F32),
                   jax.ShapeDtypeStruct(win_shape, F32)],
        compiler_params=pltpu.CompilerParams(vmem_limit_bytes=VMEM_LIMIT),
        name="bias_tables",
    )(far, fwd, bwd, win_vec)
    return dict(bias5=diff, bias5_t=diff_t, win_bias=win)


def _qk_bound(q_gain, k_gain):
    return HEAD_DIM * NORM_SLACK * jnp.max(jnp.abs(q_gain)) * jnp.max(jnp.abs(k_gain))


def _is_bounded(half_range):
    return (half_range <= MAX_UNSHIFTED_LOGIT).astype(jnp.int32).reshape(1)


def _layer_params(l, ranges, norm_attn_g, w_in, diff_q_norm_g, diff_k_norm_g, diff_lambda_q1, diff_lambda_k1,
                  diff_lambda_q2, diff_lambda_k2, diff_subln_g, win_q_norm_g, win_k_norm_g, win_sink,
                  w_out, norm_ffn_g, w_gate, w_up, conv_w, conv_b, w_down):
    scale = HEAD_DIM ** -0.5
    w = w_in[l].astype(BF16)
    dq, dk, dv = w[:, :DIFF_W], w[:, DIFF_W:2 * DIFF_W], w[:, 2 * DIFF_W:3 * DIFF_W]
    wq = w[:, 3 * DIFF_W:3 * DIFF_W + WIN_Q_W].reshape(D_MODEL, WIN_KV_HEADS, WIN_GROUP, HEAD_DIM)
    wq = wq.transpose(0, 2, 1, 3).reshape(D_MODEL, WIN_Q_W)
    wk = w[:, 3 * DIFF_W + WIN_Q_W:3 * DIFF_W + WIN_Q_W + WIN_KV_W]
    wv = w[:, 3 * DIFF_W + WIN_Q_W + WIN_KV_W:]
    w_in_p = jnp.concatenate([dq, dk, wq, wk, dv, wv], axis=1)
    gvec = jnp.concatenate([jnp.tile(diff_q_norm_g[l], 2 * DIFF_HEADS) * (scale * LOG2E),
                            jnp.tile(diff_k_norm_g[l], 2 * DIFF_HEADS),
                            jnp.tile(win_q_norm_g[l], WIN_HEADS) * (scale * LOG2E),
                            jnp.tile(win_k_norm_g[l], WIN_KV_HEADS)]).reshape(1, NORMED_W).astype(F32)
    lamv = jnp.zeros((8, LANES), F32)
    lamv = lamv.at[0:4, :HEAD_DIM].set(jnp.stack([diff_lambda_q1[l], diff_lambda_k1[l],
                                                  diff_lambda_q2[l], diff_lambda_k2[l]]).astype(F32))
    wo = w_out[l]
    wb = wo[DIFF_W:].reshape(WIN_KV_HEADS, WIN_GROUP, HEAD_DIM, D_MODEL)
    wb = wb.transpose(1, 0, 2, 3).reshape(WIN_Q_W, D_MODEL)
    cw = jnp.concatenate([conv_w[l], conv_b[l][None]], axis=0).astype(F32)
    return dict(
        g_attn=norm_attn_g[l].reshape(1, D_MODEL).astype(F32), w_in_p=w_in_p, gvec=gvec, lamv=lamv,
        subln_g=diff_subln_g[l].reshape(1, 2 * HEAD_DIM).astype(F32),
        wa=wo[:DIFF_W].astype(BF16), wb=wb.astype(BF16),
        g_ffn=norm_ffn_g[l].reshape(1, D_MODEL).astype(F32),
        wg=w_gate[l].astype(BF16), wu=w_up[l].astype(BF16), wd=w_down[l].astype(BF16), cw=cw,
        lam_init=0.8 - 0.6 * math.exp(-0.3 * l),
        diff_bounded=_is_bounded(_qk_bound(diff_q_norm_g[l].astype(F32) * (scale * LOG2E), diff_k_norm_g[l])
                                 + jnp.max(0.5 * (ranges["hi"] - ranges["lo"])[:DIFF_HEADS])),
        **_window_softmax_params(ranges, win_q_norm_g[l].astype(F32) * (scale * LOG2E), win_k_norm_g[l],
                                 win_sink[l].astype(F32) * LOG2E),
    )


def _window_softmax_params(ranges, q_gain, k_gain, sink):
    qk = _qk_bound(q_gain, k_gain)
    hi = jnp.maximum(ranges["hi"][DIFF_HEADS:] + qk, sink)
    lo = jnp.minimum(ranges["lo"][DIFF_HEADS:] - qk, sink)
    centre = 0.5 * (hi + lo)
    return dict(win_sink=sink - centre, win_bounded=_is_bounded(jnp.max(0.5 * (hi - lo))),
                **_bias_tables(ranges, centre))


def _encoder_layer(x, p):
    batch, seq, _ = x.shape
    x2d = x.reshape(batch * seq, D_MODEL)
    dq, dk, wq, wk, dv, wv = _in_proj(x2d, p["g_attn"], p["w_in_p"], p["gvec"])
    oa = _diff_attn(dq, dk, dv, p["diff_bounded"], p["lamv"], p["bias5"], p["bias5_t"], p["subln_g"], p["lam_init"],
                    batch, seq)
    ob = _win_attn(wq, wk, wv, p["win_bounded"], p["win_bias"], p["win_sink"], batch, seq)
    y = _mix_ffn(x2d, oa, ob, p["wa"], p["wb"], p["g_ffn"], p["wg"], p["wu"], p["wd"], p["cw"], batch, seq)
    return y.reshape(batch, seq, D_MODEL)


def kernel(x_prompt, x_sample, norm_attn_g, w_in, diff_q_norm_g, diff_k_norm_g, diff_lambda_q1, diff_lambda_k1, diff_lambda_q2, diff_lambda_k2, diff_subln_g, win_q_norm_g, win_k_norm_g, win_sink, rel_bias, w_out, norm_ffn_g, w_gate, w_up, conv_w, conv_b, w_down):
    depth = w_in.shape[0]
    ranges = _bias_ranges(rel_bias)
    layers = [_layer_params(l, ranges, norm_attn_g, w_in, diff_q_norm_g, diff_k_norm_g, diff_lambda_q1,
                            diff_lambda_k1, diff_lambda_q2, diff_lambda_k2, diff_subln_g, win_q_norm_g,
                            win_k_norm_g, win_sink, w_out, norm_ffn_g, w_gate, w_up, conv_w, conv_b, w_down)
              for l in range(depth)]

    def run(x):
        for p in layers:
            x = _encoder_layer(x, p)
        return x

    return run(x_prompt), run(x_sample)
```

```python
import functools
import math

import jax
import jax.numpy as jnp
from jax import lax
from jax.experimental import pallas as pl
from jax.experimental.pallas import tpu as pltpu

D_MODEL = 1024
HEAD_DIM = 64
DIFF_HEADS = 4
WIN_HEADS = 8
WIN_KV_HEADS = 2
WIN_GROUP = WIN_HEADS // WIN_KV_HEADS
WINDOW = 128
BLOCK = 128
N_BUCKETS = 32
MAX_DISTANCE = 128
D_FF = 2816
EPS = 1e-6
NEG_INF = -1e30

DIFF_W = DIFF_HEADS * 2 * HEAD_DIM
WIN_Q_W = WIN_HEADS * HEAD_DIM
WIN_KV_W = WIN_KV_HEADS * HEAD_DIM
D_IN = 3 * DIFF_W + WIN_Q_W + 2 * WIN_KV_W
NORMED_W = 2 * DIFF_W + WIN_Q_W + WIN_KV_W

LANES = 128
VMEM_LIMIT = 56 * 1024 * 1024

TM_PROJ = 1024
TQ = 256
TK = 256
HEADS_PER_STEP = 1
STRIP = 16
LOG2E = math.log2(math.e)
MAX_UNSHIFTED_LOGIT = 100.0
NORM_SLACK = 1.02
TM_FFN = 1024
FF_CHUNK = 1024
HALO = 16

BF16 = jnp.bfloat16
F32 = jnp.float32


def _rel_bucket(rel):
    nb = N_BUCKETS // 2
    max_exact = nb // 2
    ret = jnp.where(rel > 0, nb, 0)
    n = jnp.abs(rel)
    nf = jnp.maximum(n, 1).astype(F32)
    large = max_exact + (jnp.log(nf / max_exact) / math.log(MAX_DISTANCE / max_exact)
                         * (nb - max_exact)).astype(jnp.int32)
    large = jnp.minimum(large, nb - 1)
    return ret + jnp.where(n < max_exact, n, large)


def _compiler_params(semantics):
    return pltpu.CompilerParams(dimension_semantics=semantics, vmem_limit_bytes=VMEM_LIMIT)


def _resident(shape):
    nd = len(shape)
    return pl.BlockSpec(shape, lambda *_: (0,) * nd, pipeline_mode=pl.Buffered(1))


def _in_proj_kernel(x_ref, g_ref, w_ref, gv_ref, dq_ref, dk_ref, wq_ref, wk_ref, dv_ref, wv_ref):
    x = x_ref[...]
    ms = jnp.mean(x * x, axis=-1, keepdims=True)
    h = (x * lax.rsqrt(ms + EPS) * g_ref[...]).astype(BF16)
    proj = jnp.dot(h, w_ref[...], preferred_element_type=F32)
    first_head = lax.broadcasted_iota(jnp.int32, (TM_PROJ, LANES), 1) < HEAD_DIM

    def head_norm(c0):
        blk = proj[:, c0:c0 + LANES]
        sq = blk * blk
        lo = jnp.sum(jnp.where(first_head, sq, 0.0), axis=-1, keepdims=True)
        hi = jnp.sum(sq, axis=-1, keepdims=True) - lo
        ss = jnp.where(first_head, lo, hi)
        return (blk * lax.rsqrt(ss * (1.0 / HEAD_DIM) + EPS) * gv_ref[:, c0:c0 + LANES]).astype(BF16)

    for ref, base, width in ((dq_ref, 0, DIFF_W), (dk_ref, DIFF_W, DIFF_W), (wq_ref, 2 * DIFF_W, WIN_Q_W),
                             (wk_ref, 2 * DIFF_W + WIN_Q_W, WIN_KV_W)):
        for c in range(0, width, LANES):
            ref[:, c:c + LANES] = head_norm(base + c)
    dv_ref[...] = proj[:, NORMED_W:NORMED_W + DIFF_W].astype(BF16)
    wv_ref[...] = proj[:, NORMED_W + DIFF_W:].astype(BF16)


def _in_proj(x2d, g_attn, w_in_p, gvec):
    t = x2d.shape[0]
    tile = lambda w: pl.BlockSpec((TM_PROJ, w), lambda i: (i, 0))
    widths = (DIFF_W, DIFF_W, WIN_Q_W, WIN_KV_W, DIFF_W, WIN_KV_W)
    return pl.pallas_call(
        _in_proj_kernel,
        grid=(t // TM_PROJ,),
        in_specs=[tile(D_MODEL), _resident((1, D_MODEL)), _resident((D_MODEL, D_IN)),
                  _resident((1, NORMED_W))],
        out_specs=[tile(w) for w in widths],
        out_shape=[jax.ShapeDtypeStruct((t, w), BF16) for w in widths],
        compiler_params=_compiler_params(("parallel",)),
        name="in_proj",
    )(x2d, g_attn, w_in_p, gvec)


def _diff_attn_kernel(lam_init, bounded_ref, lamv_ref, q_ref, k_ref, v_ref, bias_ref, bias_t_ref, g_ref,
                      g_col_ref, o_ref, s_scr, p_scr, v_ones_scr, vt_scr, pt_scr, pt_alt_scr, at_scr, at_alt_scr):
    seq = k_ref.shape[0]
    nq, nk = seq // TQ, seq // TK
    lv = lamv_ref[...]
    lam = (jnp.exp(jnp.sum(lv[0:1] * lv[1:2], axis=-1, keepdims=True))
           - jnp.exp(jnp.sum(lv[2:3] * lv[3:4], axis=-1, keepdims=True)) + lam_init)
    lane = lax.broadcasted_iota(jnp.int32, (TQ, LANES), 1)
    reload_offset = jnp.maximum(bounded_ref[0], 1) - 1
    head_cols = [slice(hd * LANES, (hd + 1) * LANES) for hd in range(HEADS_PER_STEP)]

    def stacked_q(hd, tile):
        start = tile * TQ if isinstance(tile, int) else pl.multiple_of(tile * TQ, TQ)
        q = q_ref[pl.ds(start, TQ), head_cols[hd]]
        zero = jnp.zeros_like(q)
        return jnp.concatenate([jnp.where(lane < HEAD_DIM, q, zero), jnp.where(lane >= HEAD_DIM, q, zero)], axis=0)

    def bias_class(tile, kj):
        return min(max(kj - tile, -2), 2) + 2

    def far_factor(hd, which):
        f = jnp.exp2(bias_t_ref[hd, which, 0:1, :])
        return jnp.concatenate([f, f], axis=1)

    def unshifted_weights(hd, tile, pt_ref):
        qs = stacked_q(hd, tile)
        sums = {}
        for kj in range(nk):
            rows = slice(kj * TK, (kj + 1) * TK)
            cls = bias_class(tile, kj)
            st = lax.dot_general(k_ref[rows, head_cols[hd]], qs, (((1,), (1,)), ((), ())),
                                 preferred_element_type=F32)
            if cls in (0, 4):
                pt = jnp.exp2(st)
            else:
                bt = bias_t_ref[hd, cls]
                pt = jnp.exp2(st + jnp.concatenate([bt, bt], axis=1))
                cls = 2
            pt_ref[rows, :] = pt
            part = jnp.sum(pt.reshape(TK // 8, 8, 2 * TQ), axis=0)
            sums[cls] = sums[cls] + part if cls in sums else part
        return sums

    def unshifted_output(hd, tile, sums, pt_ref, at_ref):
        factors = {2: None, 0: far_factor(hd, 0), 4: far_factor(hd, 4)}
        key_sums = sum(s if factors[c] is None else s * factors[c] for c, s in sums.items())
        r = 1.0 / jnp.sum(key_sums, axis=0, keepdims=True)
        r = jnp.concatenate([r[:, :TQ], lam * r[:, TQ:]], axis=1)
        scales = {c: r if f is None else r * f for c, f in factors.items()}
        for kj in range(nk):
            rows = slice(kj * TK, (kj + 1) * TK)
            cls = bias_class(tile, kj)
            scale = scales[cls if cls in (0, 4) else 2]
            back = pl.ds(pl.multiple_of(kj * TK + reload_offset, TK), TK)
            at_ref[rows, :] = (pt_ref[back, :TQ] * scale[:, :TQ] - pt_ref[back, TQ:] * scale[:, TQ:]).astype(BF16)
        ot = jnp.dot(vt_scr[hd], at_ref[...], preferred_element_type=F32)
        ms = jnp.mean(ot * ot, axis=0, keepdims=True)
        gcol = jnp.concatenate([g_col_ref[...]] * (TQ // LANES), axis=1)
        ot = ot * lax.rsqrt(ms + EPS) * gcol * (1.0 - lam_init)
        o_ref[tile * TQ:(tile + 1) * TQ, head_cols[hd]] = ot.T.astype(BF16)

    def unshifted_sequence():
        pt_bufs, at_bufs = (pt_scr, pt_alt_scr), (at_scr, at_alt_scr)
        items = [(hd, tile) for hd in range(HEADS_PER_STEP) for tile in range(nq)]
        sums = unshifted_weights(*items[0], pt_bufs[0])
        for n, item in enumerate(items):
            following = unshifted_weights(*items[n + 1], pt_bufs[(n + 1) % 2]) if n + 1 < len(items) else None
            unshifted_output(*item, sums, pt_bufs[n % 2], at_bufs[n % 2])
            sums = following

    def shifted_head(hd):
        v_ones_scr[:, :LANES] = v_ref[:, head_cols[hd]]
        v_ones_scr[:, LANES:] = jnp.ones((seq, LANES), BF16)

        def shifted_tile(tile, carry):
            qs = stacked_q(hd, tile)
            tiles = [jnp.clip(kj - tile, -2, 2) + 2 for kj in range(nk)]
            for kj in range(nk):
                s_scr[:, kj * TK:(kj + 1) * TK] = lax.dot_general(
                    qs, k_ref[kj * TK:(kj + 1) * TK, head_cols[hd]], (((1,), (1,)), ((), ())),
                    preferred_element_type=F32)

            def strip(i, c):
                for mp in range(2):
                    rows = pl.ds(pl.multiple_of(mp * TQ + i * STRIP, STRIP), STRIP)
                    brows = pl.ds(pl.multiple_of(i * STRIP, STRIP), STRIP)
                    t = jnp.concatenate([s_scr[rows, kj * TK:(kj + 1) * TK] + bias_ref[hd, tiles[kj], brows, :]
                                         for kj in range(nk)], axis=1)
                    p_scr[rows, :] = jnp.exp2(t - jnp.max(t, axis=-1, keepdims=True)).astype(BF16)
                return c

            lax.fori_loop(0, TQ // STRIP, strip, 0)
            res = jnp.dot(p_scr[...], v_ones_scr[...], preferred_element_type=F32)
            out = res[:, :LANES] * (1.0 / res[:, LANES:])
            o = out[:TQ] - lam * out[TQ:]
            ms = jnp.mean(o * o, axis=-1, keepdims=True)
            o = o * lax.rsqrt(ms + EPS) * g_ref[...] * (1.0 - lam_init)
            o_ref[pl.ds(pl.multiple_of(tile * TQ, TQ), TQ), head_cols[hd]] = o.astype(BF16)
            return carry

        lax.fori_loop(0, nq, shifted_tile, 0)

    bounded = bounded_ref[0] != 0

    @pl.when(bounded)
    def _():
        for hd in range(HEADS_PER_STEP):
            vt_scr[hd] = v_ref[:, head_cols[hd]].astype(F32).T.astype(BF16)
        unshifted_sequence()

    @pl.when(jnp.logical_not(bounded))
    def _():
        for hd in range(HEADS_PER_STEP):
            shifted_head(hd)


def _diff_attn(dq, dk, dv, bounded, lamv, bias5, bias5_t, subln_g, lam_init, batch, seq):
    seq_block = pl.BlockSpec((seq, HEADS_PER_STEP * LANES), lambda b, h: (b, h))
    bias_block = pl.BlockSpec((HEADS_PER_STEP, 5, TQ, TK), lambda b, h: (h, 0, 0, 0))
    g_col = jnp.broadcast_to(subln_g.reshape(LANES, 1), (LANES, LANES))
    return pl.pallas_call(
        functools.partial(_diff_attn_kernel, lam_init),
        grid=(batch, DIFF_HEADS // HEADS_PER_STEP),
        in_specs=[
            pl.BlockSpec(memory_space=pltpu.SMEM),
            pl.BlockSpec((8, LANES), lambda b, h: (0, 0)),
            seq_block, seq_block, seq_block,
            bias_block, bias_block,
            pl.BlockSpec((1, LANES), lambda b, h: (0, 0)),
            pl.BlockSpec((LANES, LANES), lambda b, h: (0, 0)),
        ],
        out_specs=seq_block,
        out_shape=jax.ShapeDtypeStruct((batch * seq, DIFF_W), BF16),
        scratch_shapes=[pltpu.VMEM((2 * TQ, seq), F32), pltpu.VMEM((2 * TQ, seq), BF16),
                        pltpu.VMEM((seq, 2 * LANES), BF16), pltpu.VMEM((HEADS_PER_STEP, LANES, seq), BF16),
                        pltpu.VMEM((seq, 2 * TQ), F32), pltpu.VMEM((seq, 2 * TQ), F32),
                        pltpu.VMEM((seq, TQ), BF16), pltpu.VMEM((seq, TQ), BF16)],
        compiler_params=_compiler_params(("parallel", "parallel")),
        name="diff_attn",
    )(bounded, lamv, dq, dk, dv, bias5, bias5_t, subln_g, g_col)


WIN_KEYS = 3 * BLOCK
BLOCKS_PER_ITER = 4


def _win_attn_kernel(bounded_ref, sink_ref, q_ref, k_ref, v_ref, bias_ref, o_ref, s_scr, p_scr, v_ones_scr):
    seq = k_ref.shape[0]
    nb = seq // BLOCK
    group_rows = WIN_GROUP * BLOCK
    lane = lax.broadcasted_iota(jnp.int32, (BLOCK, LANES), 1)
    v_ones_scr[:, :LANES] = v_ref[...]
    v_ones_scr[:, LANES:] = jnp.ones(v_ref.shape, BF16)

    def window(n):
        start = pl.multiple_of(jnp.clip(n * BLOCK - BLOCK, 0, seq - WIN_KEYS), BLOCK)
        variant = jnp.where(n == 0, 0, jnp.where(n == nb - 1, 2, 1))
        return start, variant

    def grouped_q(n, kv):
        q = q_ref[pl.ds(pl.multiple_of(n * BLOCK, BLOCK), BLOCK), :]
        zero = jnp.zeros((BLOCK, LANES), q.dtype)
        keep = (lane < HEAD_DIM) if kv == 0 else (lane >= HEAD_DIM)
        return jnp.concatenate([jnp.where(keep, q[:, j * LANES:(j + 1) * LANES], zero)
                                for j in range(WIN_GROUP)], axis=0)

    def group_logits(n, kv, start):
        return lax.dot_general(grouped_q(n, kv), k_ref[pl.ds(start, WIN_KEYS), :], (((1,), (1,)), ((), ())),
                               preferred_element_type=F32)

    def store(n, outs):
        rows = pl.ds(pl.multiple_of(n * BLOCK, BLOCK), BLOCK)
        for j in range(WIN_GROUP):
            blk = jnp.where(lane < HEAD_DIM, outs[0][j * BLOCK:(j + 1) * BLOCK], outs[1][j * BLOCK:(j + 1) * BLOCK])
            o_ref[rows, j * LANES:(j + 1) * LANES] = blk.astype(BF16)

    def unshifted_weights(n, kv):
        start, variant = window(n)
        grows = slice(kv * group_rows, (kv + 1) * group_rows)
        return jnp.exp2(group_logits(n, kv, start) + bias_ref[variant, grows, :]).astype(BF16)

    def unshifted_output(n, kv, p):
        start, _ = window(n)
        res = jnp.dot(p, v_ones_scr[pl.ds(start, WIN_KEYS), :], preferred_element_type=F32)
        sink_w = jnp.concatenate(
            [jnp.broadcast_to(jnp.exp2(jnp.full((1, LANES), sink_ref[kv * WIN_GROUP + g], F32)), (BLOCK, LANES))
             for g in range(WIN_GROUP)], axis=0)
        return res[:, :LANES] * (1.0 / (res[:, LANES:] + sink_w))

    def unshifted_blocks(i, carry):
        items = [(i * BLOCKS_PER_ITER + u, kv) for u in range(BLOCKS_PER_ITER) for kv in range(WIN_KV_HEADS)]
        p, outs = unshifted_weights(*items[0]), []
        for m, (n, kv) in enumerate(items):
            following = unshifted_weights(*items[m + 1]) if m + 1 < len(items) else None
            outs.append(unshifted_output(n, kv, p))
            if kv == WIN_KV_HEADS - 1:
                store(n, outs)
                outs = []
            p = following
        return carry

    def shifted_block(n, carry):
        start, variant = window(n)
        outs = []
        for kv in range(WIN_KV_HEADS):
            s_scr[...] = group_logits(n, kv, start)
            sink_terms = []
            for r in range(0, group_rows, STRIP):
                rows = slice(r, r + STRIP)
                grows = slice(kv * group_rows + r, kv * group_rows + r + STRIP)
                t = s_scr[rows, :] + bias_ref[variant, grows, :]
                sink = sink_ref[(kv * group_rows + r) // BLOCK]
                row_max = jnp.max(t, axis=-1, keepdims=True)
                p_scr[rows, :] = jnp.exp2(t - jnp.maximum(row_max, sink)).astype(BF16)
                m_wide = jnp.maximum(jnp.broadcast_to(row_max, (STRIP, LANES)), sink)
                sink_terms.append(jnp.exp2(sink - m_wide))
            res = jnp.dot(p_scr[...], v_ones_scr[pl.ds(start, WIN_KEYS), :], preferred_element_type=F32)
            l = res[:, LANES:] + jnp.concatenate(sink_terms, axis=0)
            outs.append(res[:, :LANES] * (1.0 / l))
        store(n, outs)
        return carry

    bounded = bounded_ref[0] != 0

    @pl.when(bounded)
    def _():
        lax.fori_loop(0, nb // BLOCKS_PER_ITER, unshifted_blocks, 0)

    @pl.when(jnp.logical_not(bounded))
    def _():
        lax.fori_loop(0, nb, shifted_block, 0)


def _win_attn(wq, wk, wv, bounded, bias3, sink, batch, seq):
    rows = WIN_HEADS * BLOCK
    smem = pl.BlockSpec(memory_space=pltpu.SMEM)
    seq_block = lambda w: pl.BlockSpec((seq, w), lambda b: (b, 0))
    return pl.pallas_call(
        _win_attn_kernel,
        grid=(batch,),
        in_specs=[smem, smem, seq_block(WIN_Q_W), seq_block(WIN_KV_W), seq_block(WIN_KV_W),
                  _resident((3, rows, WIN_KEYS))],
        out_specs=seq_block(WIN_Q_W),
        out_shape=jax.ShapeDtypeStruct((batch * seq, WIN_Q_W), BF16),
        scratch_shapes=[pltpu.VMEM((rows // 2, WIN_KEYS), F32), pltpu.VMEM((rows // 2, WIN_KEYS), BF16),
                        pltpu.VMEM((seq, 2 * LANES), BF16)],
        compiler_params=_compiler_params(("parallel",)),
        name="win_attn",
    )(bounded, sink, wq, wk, wv, bias3)


def _mix_ffn_kernel(x_ref, xp_ref, xn_ref, oa_ref, oap_ref, oan_ref, ob_ref, obp_ref, obn_ref,
                    wa_ref, wb_ref, g_ref, wg_ref, wu_ref, wd_ref, cw_ref, o_ref, act_scr):
    i = pl.program_id(1)
    last = pl.num_programs(1) - 1
    ext_rows = TM_FFN + 2 * HALO
    tile = slice(HALO, HALO + TM_FFN)
    ext = lambda p, c, n: jnp.concatenate([p[...], c[...], n[...]], axis=0)
    x1 = (ext(xp_ref, x_ref, xn_ref)
          + jnp.dot(ext(oap_ref, oa_ref, oan_ref), wa_ref[...], preferred_element_type=F32)
          + jnp.dot(ext(obp_ref, ob_ref, obn_ref), wb_ref[...], preferred_element_type=F32))
    o_ref[...] = x1[tile]
    ms = jnp.mean(x1 * x1, axis=-1, keepdims=True)
    h_ext = (x1 * lax.rsqrt(ms + EPS) * g_ref[...]).astype(BF16)
    row = lax.broadcasted_iota(jnp.int32, (ext_rows, 1), 0)
    outside = ((row < HALO) & (i == 0)) | ((row >= HALO + TM_FFN) & (i == last))
    h_ext = jnp.where(outside, jnp.zeros_like(h_ext), h_ext)
    h = h_ext[tile]
    for c0 in range(0, D_FF, FF_CHUNK):
        cols = slice(c0, min(c0 + FF_CHUNK, D_FF))
        g_ext = jnp.dot(h_ext, wg_ref[:, cols], preferred_element_type=F32)
        g = g_ext[tile]
        g_dn = pltpu.roll(g_ext, 1, 0)[tile]
        g_up = pltpu.roll(g_ext, ext_rows - 1, 0)[tile]
        cw = cw_ref[:, cols]
        u = g_dn * cw[0:1] + g * cw[1:2] + g_up * cw[2:3] + cw[3:4]
        up = jnp.dot(h, wu_ref[:, cols], preferred_element_type=F32)
        act = (u * (1.0 / (1.0 + jnp.exp(-u)))) * up
        act_scr[:, cols] = act.astype(BF16)
    o_ref[...] = o_ref[...] + jnp.dot(act_scr[...], wd_ref[...], preferred_element_type=F32)


def _mix_ffn(x2d, oa, ob, wa, wb, g_ffn, wg, wu, wd, cw, batch, seq):
    nt = seq // TM_FFN
    hb = TM_FFN // HALO
    n_halo = batch * seq // HALO

    def with_halo(width):
        return [pl.BlockSpec((TM_FFN, width), lambda b, i: (b * nt + i, 0)),
                pl.BlockSpec((HALO, width), lambda b, i: (jnp.maximum((b * nt + i) * hb - 1, 0), 0)),
                pl.BlockSpec((HALO, width), lambda b, i: (jnp.minimum((b * nt + i + 1) * hb, n_halo - 1), 0))]

    return pl.pallas_call(
        _mix_ffn_kernel,
        grid=(batch, nt),
        in_specs=(with_halo(D_MODEL) + with_halo(DIFF_W) + with_halo(WIN_Q_W)
                  + [_resident((DIFF_W, D_MODEL)), _resident((WIN_Q_W, D_MODEL)), _resident((1, D_MODEL)),
                     _resident((D_MODEL, D_FF)), _resident((D_MODEL, D_FF)), _resident((D_FF, D_MODEL)),
                     _resident((4, D_FF))]),
        out_specs=pl.BlockSpec((TM_FFN, D_MODEL), lambda b, i: (b * nt + i, 0)),
        out_shape=jax.ShapeDtypeStruct((batch * seq, D_MODEL), F32),
        scratch_shapes=[pltpu.VMEM((TM_FFN, D_FF), BF16)],
        compiler_params=_compiler_params(("parallel", "arbitrary")),
        name="mix_ffn",
    )(x2d, x2d, x2d, oa, oa, oa, ob, ob, ob, wa, wb, g_ffn, wg, wu, wd, cw)


SKEW_W = 1024


def _bias_tables_kernel(far_ref, fwd_ref, bwd_ref, win_vec_ref, bias_ref, bias_t_ref, win_ref):
    def skew(row, rows):
        return pltpu.roll(jnp.broadcast_to(row, (rows, SKEW_W)), 0, 1, stride=1, stride_axis=0)

    for h in range(DIFF_HEADS):
        near = skew(fwd_ref[h:h + 1, :], TQ)
        near_t = skew(bwd_ref[h:h + 1, :], 3 * TK)
        for d in range(3):
            bias_ref[h, d + 1] = near[:, d * TK:(d + 1) * TK]
            bias_t_ref[h, d + 1] = near_t[d * TK:(d + 1) * TK, :TQ]
        for d in (0, 4):
            bias_ref[h, d] = jnp.full((TQ, TK), far_ref[h, d // 4], F32)
            bias_t_ref[h, d] = jnp.full((TK, TQ), far_ref[h, d // 4], F32)
    for h in range(WIN_HEADS):
        band = skew(win_vec_ref[h:h + 1, :], BLOCK)
        for v in range(3):
            win_ref[v, h * BLOCK:(h + 1) * BLOCK, :] = band[:, (2 - v) * BLOCK:(2 - v) * BLOCK + WIN_KEYS]


def _wrapped_offsets(last):
    idx = jnp.arange(SKEW_W)
    return jnp.where(idx <= last, idx, idx - SKEW_W)


def _bias_ranges(rel_bias):
    table = rel_bias.astype(F32).T * LOG2E
    return dict(table=table, hi=jnp.max(table, axis=1), lo=jnp.min(table, axis=1))


def _bias_tables(ranges, win_centre):
    centre = jnp.concatenate([0.5 * (ranges["hi"] + ranges["lo"])[:DIFF_HEADS], win_centre])
    centred = ranges["table"] - centre[:, None]
    rel_fwd = _wrapped_offsets(3 * TK - 1) - TK
    rel_bwd = -_wrapped_offsets(TQ - 1) - TK
    rel_win = _wrapped_offsets(5 * BLOCK - 1) - 2 * BLOCK
    rel_far = jnp.array([-2 * TK, 2 * TK])
    looked_up = centred[:, _rel_bucket(jnp.concatenate([rel_fwd, rel_bwd, rel_win, rel_far]))]
    fwd, bwd = looked_up[:DIFF_HEADS, :SKEW_W], looked_up[:DIFF_HEADS, SKEW_W:2 * SKEW_W]
    far = looked_up[:DIFF_HEADS, 3 * SKEW_W:]
    win_vec = jnp.where(jnp.abs(rel_win) <= WINDOW, looked_up[DIFF_HEADS:, 2 * SKEW_W:3 * SKEW_W], NEG_INF)
    whole = lambda shape: pl.BlockSpec(shape, lambda: (0,) * len(shape))
    tile_shape = (DIFF_HEADS, 5, TQ, TK)
    win_shape = (3, WIN_HEADS * BLOCK, WIN_KEYS)
    diff, diff_t, win = pl.pallas_call(
        _bias_tables_kernel,
        in_specs=[pl.BlockSpec(memory_space=pltpu.SMEM), whole((DIFF_HEADS, SKEW_W)), whole((DIFF_HEADS, SKEW_W)),
                  whole((WIN_HEADS, SKEW_W))],
        out_specs=[whole(tile_shape), whole(tile_shape), whole(win_shape)],
        out_shape=[jax.ShapeDtypeStruct(tile_shape, F32), jax.ShapeDtypeStruct(tile_shape, F32),
                   jax.ShapeDtypeStruct(win_shape, F32)],
        compiler_params=pltpu.CompilerParams(vmem_limit_bytes=VMEM_LIMIT),
        name="bias_tables",
    )(far, fwd, bwd, win_vec)
    return dict(bias5=diff, bias5_t=diff_t, win_bias=win)


def _qk_bound(q_gain, k_gain):
    return HEAD_DIM * NORM_SLACK * jnp.max(jnp.abs(q_gain)) * jnp.max(jnp.abs(k_gain))


def _is_bounded(half_range):
    return (half_range <= MAX_UNSHIFTED_LOGIT).astype(jnp.int32).reshape(1)


def _layer_params(l, ranges, norm_attn_g, w_in, diff_q_norm_g, diff_k_norm_g, diff_lambda_q1, diff_lambda_k1,
                  diff_lambda_q2, diff_lambda_k2, diff_subln_g, win_q_norm_g, win_k_norm_g, win_sink,
                  w_out, norm_ffn_g, w_gate, w_up, conv_w, conv_b, w_down):
    scale = HEAD_DIM ** -0.5
    w = w_in[l].astype(BF16)
    dq, dk, dv = w[:, :DIFF_W], w[:, DIFF_W:2 * DIFF_W], w[:, 2 * DIFF_W:3 * DIFF_W]
    wq = w[:, 3 * DIFF_W:3 * DIFF_W + WIN_Q_W].reshape(D_MODEL, WIN_KV_HEADS, WIN_GROUP, HEAD_DIM)
    wq = wq.transpose(0, 2, 1, 3).reshape(D_MODEL, WIN_Q_W)
    wk = w[:, 3 * DIFF_W + WIN_Q_W:3 * DIFF_W + WIN_Q_W + WIN_KV_W]
    wv = w[:, 3 * DIFF_W + WIN_Q_W + WIN_KV_W:]
    w_in_p = jnp.concatenate([dq, dk, wq, wk, dv, wv], axis=1)
    gvec = jnp.concatenate([jnp.tile(diff_q_norm_g[l], 2 * DIFF_HEADS) * (scale * LOG2E),
                            jnp.tile(diff_k_norm_g[l], 2 * DIFF_HEADS),
                            jnp.tile(win_q_norm_g[l], WIN_HEADS) * (scale * LOG2E),
                            jnp.tile(win_k_norm_g[l], WIN_KV_HEADS)]).reshape(1, NORMED_W).astype(F32)
    lamv = jnp.zeros((8, LANES), F32)
    lamv = lamv.at[0:4, :HEAD_DIM].set(jnp.stack([diff_lambda_q1[l], diff_lambda_k1[l],
                                                  diff_lambda_q2[l], diff_lambda_k2[l]]).astype(F32))
    wo = w_out[l]
    wb = wo[DIFF_W:].reshape(WIN_KV_HEADS, WIN_GROUP, HEAD_DIM, D_MODEL)
    wb = wb.transpose(1, 0, 2, 3).reshape(WIN_Q_W, D_MODEL)
    cw = jnp.concatenate([conv_w[l], conv_b[l][None]], axis=0).astype(F32)
    return dict(
        g_attn=norm_attn_g[l].reshape(1, D_MODEL).astype(F32), w_in_p=w_in_p, gvec=gvec, lamv=lamv,
        subln_g=diff_subln_g[l].reshape(1, 2 * HEAD_DIM).astype(F32),
        wa=wo[:DIFF_W].astype(BF16), wb=wb.astype(BF16),
        g_ffn=norm_ffn_g[l].reshape(1, D_MODEL).astype(F32),
        wg=w_gate[l].astype(BF16), wu=w_up[l].astype(BF16), wd=w_down[l].astype(BF16), cw=cw,
        lam_init=0.8 - 0.6 * math.exp(-0.3 * l),
        diff_bounded=_is_bounded(_qk_bound(diff_q_norm_g[l].astype(F32) * (scale * LOG2E), diff_k_norm_g[l])
                                 + jnp.max(0.5 * (ranges["hi"] - ranges["lo"])[:DIFF_HEADS])),
        **_window_softmax_params(ranges, win_q_norm_g[l].astype(F32) * (scale * LOG2E), win_k_norm_g[l],
                                 win_sink[l].astype(F32) * LOG2E),
    )


def _window_softmax_params(ranges, q_gain, k_gain, sink):
    qk = _qk_bound(q_gain, k_gain)
    hi = jnp.maximum(ranges["hi"][DIFF_HEADS:] + qk, sink)
    lo = jnp.minimum(ranges["lo"][DIFF_HEADS:] - qk, sink)
    centre = 0.5 * (hi + lo)
    return dict(win_sink=sink - centre, win_bounded=_is_bounded(jnp.max(0.5 * (hi - lo))),
                **_bias_tables(ranges, centre))


def _encoder_layer(x, p):
    batch, seq, _ = x.shape
    x2d = x.reshape(batch * seq, D_MODEL)
    dq, dk, wq, wk, dv, wv = _in_proj(x2d, p["g_attn"], p["w_in_p"], p["gvec"])
    oa = _diff_attn(dq, dk, dv, p["diff_bounded"], p["lamv"], p["bias5"], p["bias5_t"], p["subln_g"], p["lam_init"],
                    batch, seq)
    ob = _win_attn(wq, wk, wv, p["win_bounded"], p["win_bias"], p["win_sink"], batch, seq)
    y = _mix_ffn(x2d, oa, ob, p["wa"], p["wb"], p["g_ffn"], p["wg"], p["wu"], p["wd"], p["cw"], batch, seq)
    return y.reshape(batch, seq, D_MODEL)


def kernel(x_prompt, x_sample, norm_attn_g, w_in, diff_q_norm_g, diff_k_norm_g, diff_lambda_q1, diff_lambda_k1, diff_lambda_q2, diff_lambda_k2, diff_subln_g, win_q_norm_g, win_k_norm_g, win_sink, rel_bias, w_out, norm_ffn_g, w_gate, w_up, conv_w, conv_b, w_down):
    depth = w_in.shape[0]
    ranges = _bias_ranges(rel_bias)
    layers = [_layer_params(l, ranges, norm_attn_g, w_in, diff_q_norm_g, diff_k_norm_g, diff_lambda_q1,
                            diff_lambda_k1, diff_lambda_q2, diff_lambda_k2, diff_subln_g, win_q_norm_g,
                            win_k_norm_g, win_sink, w_out, norm_ffn_g, w_gate, w_up, conv_w, conv_b, w_down)
              for l in range(depth)]

    def run(x):
        for p in layers:
            x = _encoder_layer(x, p)
        return x

    return run(x_prompt), run(x_sample)
```

```python
import functools
import math

import jax
import jax.numpy as jnp
from jax import lax
from jax.experimental import pallas as pl
from jax.experimental.pallas import tpu as pltpu

D_MODEL = 1024
HEAD_DIM = 64
DIFF_HEADS = 4
WIN_HEADS = 8
WIN_KV_HEADS = 2
WIN_GROUP = WIN_HEADS // WIN_KV_HEADS
WINDOW = 128
BLOCK = 128
N_BUCKETS = 32
MAX_DISTANCE = 128
D_FF = 2816
EPS = 1e-6
NEG_INF = -1e30

DIFF_W = DIFF_HEADS * 2 * HEAD_DIM
WIN_Q_W = WIN_HEADS * HEAD_DIM
WIN_KV_W = WIN_KV_HEADS * HEAD_DIM
D_IN = 3 * DIFF_W + WIN_Q_W + 2 * WIN_KV_W
NORMED_W = 2 * DIFF_W + WIN_Q_W + WIN_KV_W

LANES = 128
VMEM_LIMIT = 56 * 1024 * 1024

TM_PROJ = 1024
TQ = 256
TK = 256
HEADS_PER_STEP = 2
STRIP = 16
LOG2E = math.log2(math.e)
MAX_UNSHIFTED_LOGIT = 100.0
NORM_SLACK = 1.02
TM_FFN = 1024
FF_CHUNK = 1024
HALO = 16

BF16 = jnp.bfloat16
F32 = jnp.float32


def _rel_bucket(rel):
    nb = N_BUCKETS // 2
    max_exact = nb // 2
    ret = jnp.where(rel > 0, nb, 0)
    n = jnp.abs(rel)
    nf = jnp.maximum(n, 1).astype(F32)
    large = max_exact + (jnp.log(nf / max_exact) / math.log(MAX_DISTANCE / max_exact)
                         * (nb - max_exact)).astype(jnp.int32)
    large = jnp.minimum(large, nb - 1)
    return ret + jnp.where(n < max_exact, n, large)


def _compiler_params(semantics):
    return pltpu.CompilerParams(dimension_semantics=semantics, vmem_limit_bytes=VMEM_LIMIT)


def _resident(shape):
    nd = len(shape)
    return pl.BlockSpec(shape, lambda *_: (0,) * nd, pipeline_mode=pl.Buffered(1))


def _in_proj_kernel(x_ref, g_ref, w_ref, gv_ref, dq_ref, dk_ref, wq_ref, wk_ref, dv_ref, wv_ref):
    x = x_ref[...]
    ms = jnp.mean(x * x, axis=-1, keepdims=True)
    h = (x * lax.rsqrt(ms + EPS) * g_ref[...]).astype(BF16)
    proj = jnp.dot(h, w_ref[...], preferred_element_type=F32)
    first_head = lax.broadcasted_iota(jnp.int32, (TM_PROJ, LANES), 1) < HEAD_DIM

    def head_norm(c0):
        blk = proj[:, c0:c0 + LANES]
        sq = blk * blk
        lo = jnp.sum(jnp.where(first_head, sq, 0.0), axis=-1, keepdims=True)
        hi = jnp.sum(sq, axis=-1, keepdims=True) - lo
        ss = jnp.where(first_head, lo, hi)
        return (blk * lax.rsqrt(ss * (1.0 / HEAD_DIM) + EPS) * gv_ref[:, c0:c0 + LANES]).astype(BF16)

    for ref, base, width in ((dq_ref, 0, DIFF_W), (dk_ref, DIFF_W, DIFF_W), (wq_ref, 2 * DIFF_W, WIN_Q_W),
                             (wk_ref, 2 * DIFF_W + WIN_Q_W, WIN_KV_W)):
        for c in range(0, width, LANES):
            ref[:, c:c + LANES] = head_norm(base + c)
    dv_ref[...] = proj[:, NORMED_W:NORMED_W + DIFF_W].astype(BF16)
    wv_ref[...] = proj[:, NORMED_W + DIFF_W:].astype(BF16)


def _in_proj(x2d, g_attn, w_in_p, gvec):
    t = x2d.shape[0]
    tile = lambda w: pl.BlockSpec((TM_PROJ, w), lambda i: (i, 0))
    widths = (DIFF_W, DIFF_W, WIN_Q_W, WIN_KV_W, DIFF_W, WIN_KV_W)
    return pl.pallas_call(
        _in_proj_kernel,
        grid=(t // TM_PROJ,),
        in_specs=[tile(D_MODEL), _resident((1, D_MODEL)), _resident((D_MODEL, D_IN)),
                  _resident((1, NORMED_W))],
        out_specs=[tile(w) for w in widths],
        out_shape=[jax.ShapeDtypeStruct((t, w), BF16) for w in widths],
        compiler_params=_compiler_params(("parallel",)),
        name="in_proj",
    )(x2d, g_attn, w_in_p, gvec)


def _diff_attn_kernel(lam_init, bounded_ref, lamv_ref, q_ref, k_ref, v_ref, bias_ref, bias_t_ref, g_ref,
                      g_col_ref, o_ref, s_scr, p_scr, v_ones_scr, vt_scr, pt_scr, pt_alt_scr, at_scr, at_alt_scr):
    seq = k_ref.shape[0]
    nq, nk = seq // TQ, seq // TK
    lv = lamv_ref[...]
    lam = (jnp.exp(jnp.sum(lv[0:1] * lv[1:2], axis=-1, keepdims=True))
           - jnp.exp(jnp.sum(lv[2:3] * lv[3:4], axis=-1, keepdims=True)) + lam_init)
    lane = lax.broadcasted_iota(jnp.int32, (TQ, LANES), 1)
    reload_offset = jnp.maximum(bounded_ref[0], 1) - 1
    head_cols = [slice(hd * LANES, (hd + 1) * LANES) for hd in range(HEADS_PER_STEP)]

    def stacked_q(hd, tile):
        start = tile * TQ if isinstance(tile, int) else pl.multiple_of(tile * TQ, TQ)
        q = q_ref[pl.ds(start, TQ), head_cols[hd]]
        zero = jnp.zeros_like(q)
        return jnp.concatenate([jnp.where(lane < HEAD_DIM, q, zero), jnp.where(lane >= HEAD_DIM, q, zero)], axis=0)

    def bias_class(tile, kj):
        return min(max(kj - tile, -2), 2) + 2

    def far_factor(hd, which):
        f = jnp.exp2(bias_t_ref[hd, which, 0:1, :])
        return jnp.concatenate([f, f], axis=1)

    def unshifted_weights(hd, tile, pt_ref):
        qs = stacked_q(hd, tile)
        sums = {}
        for kj in range(nk):
            rows = slice(kj * TK, (kj + 1) * TK)
            cls = bias_class(tile, kj)
            st = lax.dot_general(k_ref[rows, head_cols[hd]], qs, (((1,), (1,)), ((), ())),
                                 preferred_element_type=F32)
            if cls in (0, 4):
                pt = jnp.exp2(st)
            else:
                bt = bias_t_ref[hd, cls]
                pt = jnp.exp2(st + jnp.concatenate([bt, bt], axis=1))
                cls = 2
            pt_ref[rows, :] = pt
            part = jnp.sum(pt.reshape(TK // 8, 8, 2 * TQ), axis=0)
            sums[cls] = sums[cls] + part if cls in sums else part
        return sums

    def unshifted_output(hd, tile, sums, pt_ref, at_ref):
        factors = {2: None, 0: far_factor(hd, 0), 4: far_factor(hd, 4)}
        key_sums = sum(s if factors[c] is None else s * factors[c] for c, s in sums.items())
        r = 1.0 / jnp.sum(key_sums, axis=0, keepdims=True)
        r = jnp.concatenate([r[:, :TQ], lam * r[:, TQ:]], axis=1)
        scales = {c: r if f is None else r * f for c, f in factors.items()}
        for kj in range(nk):
            rows = slice(kj * TK, (kj + 1) * TK)
            cls = bias_class(tile, kj)
            scale = scales[cls if cls in (0, 4) else 2]
            back = pl.ds(pl.multiple_of(kj * TK + reload_offset, TK), TK)
            at_ref[rows, :] = (pt_ref[back, :TQ] * scale[:, :TQ] - pt_ref[back, TQ:] * scale[:, TQ:]).astype(BF16)
        ot = jnp.dot(vt_scr[hd], at_ref[...], preferred_element_type=F32)
        ms = jnp.mean(ot * ot, axis=0, keepdims=True)
        gcol = jnp.concatenate([g_col_ref[...]] * (TQ // LANES), axis=1)
        ot = ot * lax.rsqrt(ms + EPS) * gcol * (1.0 - lam_init)
        o_ref[tile * TQ:(tile + 1) * TQ, head_cols[hd]] = ot.T.astype(BF16)

    def unshifted_sequence():
        pt_bufs, at_bufs = (pt_scr, pt_alt_scr), (at_scr, at_alt_scr)
        items = [(hd, tile) for hd in range(HEADS_PER_STEP) for tile in range(nq)]
        sums = unshifted_weights(*items[0], pt_bufs[0])
        for n, item in enumerate(items):
            following = unshifted_weights(*items[n + 1], pt_bufs[(n + 1) % 2]) if n + 1 < len(items) else None
            unshifted_output(*item, sums, pt_bufs[n % 2], at_bufs[n % 2])
            sums = following

    def shifted_head(hd):
        v_ones_scr[:, :LANES] = v_ref[:, head_cols[hd]]
        v_ones_scr[:, LANES:] = jnp.ones((seq, LANES), BF16)

        def shifted_tile(tile, carry):
            qs = stacked_q(hd, tile)
            tiles = [jnp.clip(kj - tile, -2, 2) + 2 for kj in range(nk)]
            for kj in range(nk):
                s_scr[:, kj * TK:(kj + 1) * TK] = lax.dot_general(
                    qs, k_ref[kj * TK:(kj + 1) * TK, head_cols[hd]], (((1,), (1,)), ((), ())),
                    preferred_element_type=F32)

            def strip(i, c):
                for mp in range(2):
                    rows = pl.ds(pl.multiple_of(mp * TQ + i * STRIP, STRIP), STRIP)
                    brows = pl.ds(pl.multiple_of(i * STRIP, STRIP), STRIP)
                    t = jnp.concatenate([s_scr[rows, kj * TK:(kj + 1) * TK] + bias_ref[hd, tiles[kj], brows, :]
                                         for kj in range(nk)], axis=1)
                    p_scr[rows, :] = jnp.exp2(t - jnp.max(t, axis=-1, keepdims=True)).astype(BF16)
                return c

            lax.fori_loop(0, TQ // STRIP, strip, 0)
            res = jnp.dot(p_scr[...], v_ones_scr[...], preferred_element_type=F32)
            out = res[:, :LANES] * (1.0 / res[:, LANES:])
            o = out[:TQ] - lam * out[TQ:]
            ms = jnp.mean(o * o, axis=-1, keepdims=True)
            o = o * lax.rsqrt(ms + EPS) * g_ref[...] * (1.0 - lam_init)
            o_ref[pl.ds(pl.multiple_of(tile * TQ, TQ), TQ), head_cols[hd]] = o.astype(BF16)
            return carry

        lax.fori_loop(0, nq, shifted_tile, 0)

    bounded = bounded_ref[0] != 0

    @pl.when(bounded)
    def _():
        for hd in range(HEADS_PER_STEP):
            vt_scr[hd] = v_ref[:, head_cols[hd]].astype(F32).T.astype(BF16)
        unshifted_sequence()

    @pl.when(jnp.logical_not(bounded))
    def _():
        for hd in range(HEADS_PER_STEP):
            shifted_head(hd)


def _diff_attn(dq, dk, dv, bounded, lamv, bias5, bias5_t, subln_g, lam_init, batch, seq):
    seq_block = pl.BlockSpec((seq, HEADS_PER_STEP * LANES), lambda b, h: (b, h))
    bias_block = pl.BlockSpec((HEADS_PER_STEP, 5, TQ, TK), lambda b, h: (h, 0, 0, 0))
    g_col = jnp.broadcast_to(subln_g.reshape(LANES, 1), (LANES, LANES))
    return pl.pallas_call(
        functools.partial(_diff_attn_kernel, lam_init),
        grid=(batch, DIFF_HEADS // HEADS_PER_STEP),
        in_specs=[
            pl.BlockSpec(memory_space=pltpu.SMEM),
            pl.BlockSpec((8, LANES), lambda b, h: (0, 0)),
            seq_block, seq_block, seq_block,
            bias_block, bias_block,
            pl.BlockSpec((1, LANES), lambda b, h: (0, 0)),
            pl.BlockSpec((LANES, LANES), lambda b, h: (0, 0)),
        ],
        out_specs=seq_block,
        out_shape=jax.ShapeDtypeStruct((batch * seq, DIFF_W), BF16),
        scratch_shapes=[pltpu.VMEM((2 * TQ, seq), F32), pltpu.VMEM((2 * TQ, seq), BF16),
                        pltpu.VMEM((seq, 2 * LANES), BF16), pltpu.VMEM((HEADS_PER_STEP, LANES, seq), BF16),
                        pltpu.VMEM((seq, 2 * TQ), F32), pltpu.VMEM((seq, 2 * TQ), F32),
                        pltpu.VMEM((seq, TQ), BF16), pltpu.VMEM((seq, TQ), BF16)],
        compiler_params=_compiler_params(("parallel", "parallel")),
        name="diff_attn",
    )(bounded, lamv, dq, dk, dv, bias5, bias5_t, subln_g, g_col)


WIN_KEYS = 3 * BLOCK
BLOCKS_PER_ITER = 4


def _win_attn_kernel(bounded_ref, sink_ref, q_ref, k_ref, v_ref, bias_ref, o_ref, s_scr, p_scr, v_ones_scr):
    seq = k_ref.shape[0]
    nb = seq // BLOCK
    group_rows = WIN_GROUP * BLOCK
    lane = lax.broadcasted_iota(jnp.int32, (BLOCK, LANES), 1)
    v_ones_scr[:, :LANES] = v_ref[...]
    v_ones_scr[:, LANES:] = jnp.ones(v_ref.shape, BF16)

    def window(n):
        start = pl.multiple_of(jnp.clip(n * BLOCK - BLOCK, 0, seq - WIN_KEYS), BLOCK)
        variant = jnp.where(n == 0, 0, jnp.where(n == nb - 1, 2, 1))
        return start, variant

    def grouped_q(n, kv):
        q = q_ref[pl.ds(pl.multiple_of(n * BLOCK, BLOCK), BLOCK), :]
        zero = jnp.zeros((BLOCK, LANES), q.dtype)
        keep = (lane < HEAD_DIM) if kv == 0 else (lane >= HEAD_DIM)
        return jnp.concatenate([jnp.where(keep, q[:, j * LANES:(j + 1) * LANES], zero)
                                for j in range(WIN_GROUP)], axis=0)

    def group_logits(n, kv, start):
        return lax.dot_general(grouped_q(n, kv), k_ref[pl.ds(start, WIN_KEYS), :], (((1,), (1,)), ((), ())),
                               preferred_element_type=F32)

    def store(n, outs):
        rows = pl.ds(pl.multiple_of(n * BLOCK, BLOCK), BLOCK)
        for j in range(WIN_GROUP):
            blk = jnp.where(lane < HEAD_DIM, outs[0][j * BLOCK:(j + 1) * BLOCK], outs[1][j * BLOCK:(j + 1) * BLOCK])
            o_ref[rows, j * LANES:(j + 1) * LANES] = blk.astype(BF16)

    def unshifted_weights(n, kv):
        start, variant = window(n)
        grows = slice(kv * group_rows, (kv + 1) * group_rows)
        return jnp.exp2(group_logits(n, kv, start) + bias_ref[variant, grows, :]).astype(BF16)

    def unshifted_output(n, kv, p):
        start, _ = window(n)
        res = jnp.dot(p, v_ones_scr[pl.ds(start, WIN_KEYS), :], preferred_element_type=F32)
        sink_w = jnp.concatenate(
            [jnp.broadcast_to(jnp.exp2(jnp.full((1, LANES), sink_ref[kv * WIN_GROUP + g], F32)), (BLOCK, LANES))
             for g in range(WIN_GROUP)], axis=0)
        return res[:, :LANES] * (1.0 / (res[:, LANES:] + sink_w))

    def unshifted_blocks(i, carry):
        items = [(i * BLOCKS_PER_ITER + u, kv) for u in range(BLOCKS_PER_ITER) for kv in range(WIN_KV_HEADS)]
        p, outs = unshifted_weights(*items[0]), []
        for m, (n, kv) in enumerate(items):
            following = unshifted_weights(*items[m + 1]) if m + 1 < len(items) else None
            outs.append(unshifted_output(n, kv, p))
            if kv == WIN_KV_HEADS - 1:
                store(n, outs)
                outs = []
            p = following
        return carry

    def shifted_block(n, carry):
        start, variant = window(n)
        outs = []
        for kv in range(WIN_KV_HEADS):
            s_scr[...] = group_logits(n, kv, start)
            sink_terms = []
            for r in range(0, group_rows, STRIP):
                rows = slice(r, r + STRIP)
                grows = slice(kv * group_rows + r, kv * group_rows + r + STRIP)
                t = s_scr[rows, :] + bias_ref[variant, grows, :]
                sink = sink_ref[(kv * group_rows + r) // BLOCK]
                row_max = jnp.max(t, axis=-1, keepdims=True)
                p_scr[rows, :] = jnp.exp2(t - jnp.maximum(row_max, sink)).astype(BF16)
                m_wide = jnp.maximum(jnp.broadcast_to(row_max, (STRIP, LANES)), sink)
                sink_terms.append(jnp.exp2(sink - m_wide))
            res = jnp.dot(p_scr[...], v_ones_scr[pl.ds(start, WIN_KEYS), :], preferred_element_type=F32)
            l = res[:, LANES:] + jnp.concatenate(sink_terms, axis=0)
            outs.append(res[:, :LANES] * (1.0 / l))
        store(n, outs)
        return carry

    bounded = bounded_ref[0] != 0

    @pl.when(bounded)
    def _():
        lax.fori_loop(0, nb // BLOCKS_PER_ITER, unshifted_blocks, 0)

    @pl.when(jnp.logical_not(bounded))
    def _():
        lax.fori_loop(0, nb, shifted_block, 0)


def _win_attn(wq, wk, wv, bounded, bias3, sink, batch, seq):
    rows = WIN_HEADS * BLOCK
    smem = pl.BlockSpec(memory_space=pltpu.SMEM)
    seq_block = lambda w: pl.BlockSpec((seq, w), lambda b: (b, 0))
    return pl.pallas_call(
        _win_attn_kernel,
        grid=(batch,),
        in_specs=[smem, smem, seq_block(WIN_Q_W), seq_block(WIN_KV_W), seq_block(WIN_KV_W),
                  _resident((3, rows, WIN_KEYS))],
        out_specs=seq_block(WIN_Q_W),
        out_shape=jax.ShapeDtypeStruct((batch * seq, WIN_Q_W), BF16),
        scratch_shapes=[pltpu.VMEM((rows // 2, WIN_KEYS), F32), pltpu.VMEM((rows // 2, WIN_KEYS), BF16),
                        pltpu.VMEM((seq, 2 * LANES), BF16)],
        compiler_params=_compiler_params(("parallel",)),
        name="win_attn",
    )(bounded, sink, wq, wk, wv, bias3)


def _mix_ffn_kernel(x_ref, xp_ref, xn_ref, oa_ref, oap_ref, oan_ref, ob_ref, obp_ref, obn_ref,
                    wa_ref, wb_ref, g_ref, wg_ref, wu_ref, wd_ref, cw_ref, o_ref, act_scr):
    i = pl.program_id(1)
    last = pl.num_programs(1) - 1
    ext_rows = TM_FFN + 2 * HALO
    tile = slice(HALO, HALO + TM_FFN)
    ext = lambda p, c, n: jnp.concatenate([p[...], c[...], n[...]], axis=0)
    x1 = (ext(xp_ref, x_ref, xn_ref)
          + jnp.dot(ext(oap_ref, oa_ref, oan_ref), wa_ref[...], preferred_element_type=F32)
          + jnp.dot(ext(obp_ref, ob_ref, obn_ref), wb_ref[...], preferred_element_type=F32))
    o_ref[...] = x1[tile]
    ms = jnp.mean(x1 * x1, axis=-1, keepdims=True)
    h_ext = (x1 * lax.rsqrt(ms + EPS) * g_ref[...]).astype(BF16)
    row = lax.broadcasted_iota(jnp.int32, (ext_rows, 1), 0)
    outside = ((row < HALO) & (i == 0)) | ((row >= HALO + TM_FFN) & (i == last))
    h_ext = jnp.where(outside, jnp.zeros_like(h_ext), h_ext)
    h = h_ext[tile]
    for c0 in range(0, D_FF, FF_CHUNK):
        cols = slice(c0, min(c0 + FF_CHUNK, D_FF))
        g_ext = jnp.dot(h_ext, wg_ref[:, cols], preferred_element_type=F32)
        g = g_ext[tile]
        g_dn = pltpu.roll(g_ext, 1, 0)[tile]
        g_up = pltpu.roll(g_ext, ext_rows - 1, 0)[tile]
        cw = cw_ref[:, cols]
        u = g_dn * cw[0:1] + g * cw[1:2] + g_up * cw[2:3] + cw[3:4]
        up = jnp.dot(h, wu_ref[:, cols], preferred_element_type=F32)
        act = (u * (1.0 / (1.0 + jnp.exp(-u)))) * up
        act_scr[:, cols] = act.astype(BF16)
    o_ref[...] = o_ref[...] + jnp.dot(act_scr[...], wd_ref[...], preferred_element_type=F32)


def _mix_ffn(x2d, oa, ob, wa, wb, g_ffn, wg, wu, wd, cw, batch, seq):
    nt = seq // TM_FFN
    hb = TM_FFN // HALO
    n_halo = batch * seq // HALO

    def with_halo(width):
        return [pl.BlockSpec((TM_FFN, width), lambda b, i: (b * nt + i, 0)),
                pl.BlockSpec((HALO, width), lambda b, i: (jnp.maximum((b * nt + i) * hb - 1, 0), 0)),
                pl.BlockSpec((HALO, width), lambda b, i: (jnp.minimum((b * nt + i + 1) * hb, n_halo - 1), 0))]

    return pl.pallas_call(
        _mix_ffn_kernel,
        grid=(batch, nt),
        in_specs=(with_halo(D_MODEL) + with_halo(DIFF_W) + with_halo(WIN_Q_W)
                  + [_resident((DIFF_W, D_MODEL)), _resident((WIN_Q_W, D_MODEL)), _resident((1, D_MODEL)),
                     _resident((D_MODEL, D_FF)), _resident((D_MODEL, D_FF)), _resident((D_FF, D_MODEL)),
                     _resident((4, D_FF))]),
        out_specs=pl.BlockSpec((TM_FFN, D_MODEL), lambda b, i: (b * nt + i, 0)),
        out_shape=jax.ShapeDtypeStruct((batch * seq, D_MODEL), F32),
        scratch_shapes=[pltpu.VMEM((TM_FFN, D_FF), BF16)],
        compiler_params=_compiler_params(("parallel", "arbitrary")),
        name="mix_ffn",
    )(x2d, x2d, x2d, oa, oa, oa, ob, ob, ob, wa, wb, g_ffn, wg, wu, wd, cw)


SKEW_W = 1024


def _bias_tables_kernel(far_ref, fwd_ref, bwd_ref, win_vec_ref, bias_ref, bias_t_ref, win_ref):
    def skew(row, rows):
        return pltpu.roll(jnp.broadcast_to(row, (rows, SKEW_W)), 0, 1, stride=1, stride_axis=0)

    for h in range(DIFF_HEADS):
        near = skew(fwd_ref[h:h + 1, :], TQ)
        near_t = skew(bwd_ref[h:h + 1, :], 3 * TK)
        for d in range(3):
            bias_ref[h, d + 1] = near[:, d * TK:(d + 1) * TK]
            bias_t_ref[h, d + 1] = near_t[d * TK:(d + 1) * TK, :TQ]
        for d in (0, 4):
            bias_ref[h, d] = jnp.full((TQ, TK), far_ref[h, d // 4], F32)
            bias_t_ref[h, d] = jnp.full((TK, TQ), far_ref[h, d // 4], F32)
    for h in range(WIN_HEADS):
        band = skew(win_vec_ref[h:h + 1, :], BLOCK)
        for v in range(3):
            win_ref[v, h * BLOCK:(h + 1) * BLOCK, :] = band[:, (2 - v) * BLOCK:(2 - v) * BLOCK + WIN_KEYS]


def _wrapped_offsets(last):
    idx = jnp.arange(SKEW_W)
    return jnp.where(idx <= last, idx, idx - SKEW_W)


def _bias_ranges(rel_bias):
    table = rel_bias.astype(F32).T * LOG2E
    return dict(table=table, hi=jnp.max(table, axis=1), lo=jnp.min(table, axis=1))


def _bias_tables(ranges, win_centre):
    centre = jnp.concatenate([0.5 * (ranges["hi"] + ranges["lo"])[:DIFF_HEADS], win_centre])
    centred = ranges["table"] - centre[:, None]
    rel_fwd = _wrapped_offsets(3 * TK - 1) - TK
    rel_bwd = -_wrapped_offsets(TQ - 1) - TK
    rel_win = _wrapped_offsets(5 * BLOCK - 1) - 2 * BLOCK
    rel_far = jnp.array([-2 * TK, 2 * TK])
    looked_up = centred[:, _rel_bucket(jnp.concatenate([rel_fwd, rel_bwd, rel_win, rel_far]))]
    fwd, bwd = looked_up[:DIFF_HEADS, :SKEW_W], looked_up[:DIFF_HEADS, SKEW_W:2 * SKEW_W]
    far = looked_up[:DIFF_HEADS, 3 * SKEW_W:]
    win_vec = jnp.where(jnp.abs(rel_win) <= WINDOW, looked_up[DIFF_HEADS:, 2 * SKEW_W:3 * SKEW_W], NEG_INF)
    whole = lambda shape: pl.BlockSpec(shape, lambda: (0,) * len(shape))
    tile_shape = (DIFF_HEADS, 5, TQ, TK)
    win_shape = (3, WIN_HEADS * BLOCK, WIN_KEYS)
    diff, diff_t, win = pl.pallas_call(
        _bias_tables_kernel,
        in_specs=[pl.BlockSpec(memory_space=pltpu.SMEM), whole((DIFF_HEADS, SKEW_W)), whole((DIFF_HEADS, SKEW_W)),
                  whole((WIN_HEADS, SKEW_W))],
        out_specs=[whole(tile_shape), whole(tile_shape), whole(win_shape)],
        out_shape=[jax.ShapeDtypeStruct(tile_shape, F32), jax.ShapeDtypeStruct(tile_shape, F32),
                   jax.ShapeDtypeStruct(win_shape, F32)],
        compiler_params=pltpu.CompilerParams(vmem_limit_bytes=VMEM_LIMIT),
        name="bias_tables",
    )(far, fwd, bwd, win_vec)
    return dict(bias5=diff, bias5_t=diff_t, win_bias=win)


def _qk_bound(q_gain, k_gain):
    return HEAD_DIM * NORM_SLACK * jnp.max(jnp.abs(q_gain)) * jnp.max(jnp.abs(k_gain))


def _is_bounded(half_range):
    return (half_range <= MAX_UNSHIFTED_LOGIT).astype(jnp.int32).reshape(1)


def _layer_params(l, ranges, norm_attn_g, w_in, diff_q_norm_g, diff_k_norm_g, diff_lambda_q1, diff_lambda_k1,
                  diff_lambda_q2, diff_lambda_k2, diff_subln_g, win_q_norm_g, win_k_norm_g, win_sink,
                  w_out, norm_ffn_g, w_gate, w_up, conv_w, conv_b, w_down):
    scale = HEAD_DIM ** -0.5
    w = w_in[l].astype(BF16)
    dq, dk, dv = w[:, :DIFF_W], w[:, DIFF_W:2 * DIFF_W], w[:, 2 * DIFF_W:3 * DIFF_W]
    wq = w[:, 3 * DIFF_W:3 * DIFF_W + WIN_Q_W].reshape(D_MODEL, WIN_KV_HEADS, WIN_GROUP, HEAD_DIM)
    wq = wq.transpose(0, 2, 1, 3).reshape(D_MODEL, WIN_Q_W)
    wk = w[:, 3 * DIFF_W + WIN_Q_W:3 * DIFF_W + WIN_Q_W + WIN_KV_W]
    wv = w[:, 3 * DIFF_W + WIN_Q_W + WIN_KV_W:]
    w_in_p = jnp.concatenate([dq, dk, wq, wk, dv, wv], axis=1)
    gvec = jnp.concatenate([jnp.tile(diff_q_norm_g[l], 2 * DIFF_HEADS) * (scale * LOG2E),
                            jnp.tile(diff_k_norm_g[l], 2 * DIFF_HEADS),
                            jnp.tile(win_q_norm_g[l], WIN_HEADS) * (scale * LOG2E),
                            jnp.tile(win_k_norm_g[l], WIN_KV_HEADS)]).reshape(1, NORMED_W).astype(F32)
    lamv = jnp.zeros((8, LANES), F32)
    lamv = lamv.at[0:4, :HEAD_DIM].set(jnp.stack([diff_lambda_q1[l], diff_lambda_k1[l],
                                                  diff_lambda_q2[l], diff_lambda_k2[l]]).astype(F32))
    wo = w_out[l]
    wb = wo[DIFF_W:].reshape(WIN_KV_HEADS, WIN_GROUP, HEAD_DIM, D_MODEL)
    wb = wb.transpose(1, 0, 2, 3).reshape(WIN_Q_W, D_MODEL)
    cw = jnp.concatenate([conv_w[l], conv_b[l][None]], axis=0).astype(F32)
    return dict(
        g_attn=norm_attn_g[l].reshape(1, D_MODEL).astype(F32), w_in_p=w_in_p, gvec=gvec, lamv=lamv,
        subln_g=diff_subln_g[l].reshape(1, 2 * HEAD_DIM).astype(F32),
        wa=wo[:DIFF_W].astype(BF16), wb=wb.astype(BF16),
        g_ffn=norm_ffn_g[l].reshape(1, D_MODEL).astype(F32),
        wg=w_gate[l].astype(BF16), wu=w_up[l].astype(BF16), wd=w_down[l].astype(BF16), cw=cw,
        lam_init=0.8 - 0.6 * math.exp(-0.3 * l),
        diff_bounded=_is_bounded(_qk_bound(diff_q_norm_g[l].astype(F32) * (scale * LOG2E), diff_k_norm_g[l])
                                 + jnp.max(0.5 * (ranges["hi"] - ranges["lo"])[:DIFF_HEADS])),
        **_window_softmax_params(ranges, win_q_norm_g[l].astype(F32) * (scale * LOG2E), win_k_norm_g[l],
                                 win_sink[l].astype(F32) * LOG2E),
    )


def _window_softmax_params(ranges, q_gain, k_gain, sink):
    qk = _qk_bound(q_gain, k_gain)
    hi = jnp.maximum(ranges["hi"][DIFF_HEADS:] + qk, sink)
    lo = jnp.minimum(ranges["lo"][DIFF_HEADS:] - qk, sink)
    centre = 0.5 * (hi + lo)
    return dict(win_sink=sink - centre, win_bounded=_is_bounded(jnp.max(0.5 * (hi - lo))),
                **_bias_tables(ranges, centre))


def _encoder_layer(x, p):
    batch, seq, _ = x.shape
    x2d = x.reshape(batch * seq, D_MODEL)
    dq, dk, wq, wk, dv, wv = _in_proj(x2d, p["g_attn"], p["w_in_p"], p["gvec"])
    oa = _diff_attn(dq, dk, dv, p["diff_bounded"], p["lamv"], p["bias5"], p["bias5_t"], p["subln_g"], p["lam_init"],
                    batch, seq)
    ob = _win_attn(wq, wk, wv, p["win_bounded"], p["win_bias"], p["win_sink"], batch, seq)
    y = _mix_ffn(x2d, oa, ob, p["wa"], p["wb"], p["g_ffn"], p["wg"], p["wu"], p["wd"], p["cw"], batch, seq)
    return y.reshape(batch, seq, D_MODEL)


def kernel(x_prompt, x_sample, norm_attn_g, w_in, diff_q_norm_g, diff_k_norm_g, diff_lambda_q1, diff_lambda_k1, diff_lambda_q2, diff_lambda_k2, diff_subln_g, win_q_norm_g, win_k_norm_g, win_sink, rel_bias, w_out, norm_ffn_g, w_gate, w_up, conv_w, conv_b, w_down):
    depth = w_in.shape[0]
    ranges = _bias_ranges(rel_bias)
    layers = [_layer_params(l, ranges, norm_attn_g, w_in, diff_q_norm_g, diff_k_norm_g, diff_lambda_q1,
                            diff_lambda_k1, diff_lambda_q2, diff_lambda_k2, diff_subln_g, win_q_norm_g,
                            win_k_norm_g, win_sink, w_out, norm_ffn_g, w_gate, w_up, conv_w, conv_b, w_down)
              for l in range(depth)]

    def run(x):
        for p in layers:
            x = _encoder_layer(x, p)
        return x

    return run(x_prompt), run(x_sample)
```

```python
import functools
import math

import jax
import jax.numpy as jnp
from jax import lax
from jax.experimental import pallas as pl
from jax.experimental.pallas import tpu as pltpu

D_MODEL = 1024
HEAD_DIM = 64
DIFF_HEADS = 4
WIN_HEADS = 8
WIN_KV_HEADS = 2
WIN_GROUP = WIN_HEADS // WIN_KV_HEADS
WINDOW = 128
BLOCK = 128
N_BUCKETS = 32
MAX_DISTANCE = 128
D_FF = 2816
EPS = 1e-6
NEG_INF = -1e30

DIFF_W = DIFF_HEADS * 2 * HEAD_DIM
WIN_Q_W = WIN_HEADS * HEAD_DIM
WIN_KV_W = WIN_KV_HEADS * HEAD_DIM
D_IN = 3 * DIFF_W + WIN_Q_W + 2 * WIN_KV_W
NORMED_W = 2 * DIFF_W + WIN_Q_W + WIN_KV_W

LANES = 128
VMEM_LIMIT = 56 * 1024 * 1024

TM_PROJ = 1024
TQ = 256
TK = 256
HEADS_PER_STEP = 1
STRIP = 16
LOG2E = math.log2(math.e)
MAX_UNSHIFTED_LOGIT = 100.0
NORM_SLACK = 1.02
TM_FFN = 1024
FF_CHUNK = 1024
HALO = 16

BF16 = jnp.bfloat16
F32 = jnp.float32


def _rel_bucket(rel):
    nb = N_BUCKETS // 2
    max_exact = nb // 2
    ret = jnp.where(rel > 0, nb, 0)
    n = jnp.abs(rel)
    nf = jnp.maximum(n, 1).astype(F32)
    large = max_exact + (jnp.log(nf / max_exact) / math.log(MAX_DISTANCE / max_exact)
                         * (nb - max_exact)).astype(jnp.int32)
    large = jnp.minimum(large, nb - 1)
    return ret + jnp.where(n < max_exact, n, large)


def _compiler_params(semantics):
    return pltpu.CompilerParams(dimension_semantics=semantics, vmem_limit_bytes=VMEM_LIMIT)


def _resident(shape):
    nd = len(shape)
    return pl.BlockSpec(shape, lambda *_: (0,) * nd, pipeline_mode=pl.Buffered(1))


def _in_proj_kernel(x_ref, g_ref, w_ref, gv_ref, dq_ref, dk_ref, wq_ref, wk_ref, dv_ref, wv_ref):
    x = x_ref[...]
    ms = jnp.mean(x * x, axis=-1, keepdims=True)
    h = (x * lax.rsqrt(ms + EPS) * g_ref[...]).astype(BF16)
    proj = jnp.dot(h, w_ref[...], preferred_element_type=F32)
    first_head = lax.broadcasted_iota(jnp.int32, (TM_PROJ, LANES), 1) < HEAD_DIM

    def head_norm(c0):
        blk = proj[:, c0:c0 + LANES]
        sq = blk * blk
        lo = jnp.sum(jnp.where(first_head, sq, 0.0), axis=-1, keepdims=True)
        hi = jnp.sum(sq, axis=-1, keepdims=True) - lo
        ss = jnp.where(first_head, lo, hi)
        return (blk * lax.rsqrt(ss * (1.0 / HEAD_DIM) + EPS) * gv_ref[:, c0:c0 + LANES]).astype(BF16)

    for ref, base, width in ((dq_ref, 0, DIFF_W), (dk_ref, DIFF_W, DIFF_W), (wq_ref, 2 * DIFF_W, WIN_Q_W),
                             (wk_ref, 2 * DIFF_W + WIN_Q_W, WIN_KV_W)):
        for c in range(0, width, LANES):
            ref[:, c:c + LANES] = head_norm(base + c)
    dv_ref[...] = proj[:, NORMED_W:NORMED_W + DIFF_W].astype(BF16)
    wv_ref[...] = proj[:, NORMED_W + DIFF_W:].astype(BF16)


def _in_proj(x2d, g_attn, w_in_p, gvec):
    t = x2d.shape[0]
    tile = lambda w: pl.BlockSpec((TM_PROJ, w), lambda i: (i, 0))
    widths = (DIFF_W, DIFF_W, WIN_Q_W, WIN_KV_W, DIFF_W, WIN_KV_W)
    return pl.pallas_call(
        _in_proj_kernel,
        grid=(t // TM_PROJ,),
        in_specs=[tile(D_MODEL), _resident((1, D_MODEL)), _resident((D_MODEL, D_IN)),
                  _resident((1, NORMED_W))],
        out_specs=[tile(w) for w in widths],
        out_shape=[jax.ShapeDtypeStruct((t, w), BF16) for w in widths],
        compiler_params=_compiler_params(("parallel",)),
        name="in_proj",
    )(x2d, g_attn, w_in_p, gvec)


def _diff_attn_kernel(lam_init, bounded_ref, lamv_ref, q_ref, k_ref, v_ref, bias_ref, bias_t_ref, g_ref,
                      g_col_ref, o_ref, s_scr, p_scr, v_ones_scr, vt_scr, pt_scr, pt_alt_scr, at_scr, at_alt_scr):
    seq = k_ref.shape[0]
    nq, nk = seq // TQ, seq // TK
    lv = lamv_ref[...]
    lam = (jnp.exp(jnp.sum(lv[0:1] * lv[1:2], axis=-1, keepdims=True))
           - jnp.exp(jnp.sum(lv[2:3] * lv[3:4], axis=-1, keepdims=True)) + lam_init)
    lane = lax.broadcasted_iota(jnp.int32, (TQ, LANES), 1)
    reload_offset = jnp.maximum(bounded_ref[0], 1) - 1
    head_cols = [slice(hd * LANES, (hd + 1) * LANES) for hd in range(HEADS_PER_STEP)]

    def stacked_q(hd, tile):
        start = tile * TQ if isinstance(tile, int) else pl.multiple_of(tile * TQ, TQ)
        q = q_ref[pl.ds(start, TQ), head_cols[hd]]
        zero = jnp.zeros_like(q)
        return jnp.concatenate([jnp.where(lane < HEAD_DIM, q, zero), jnp.where(lane >= HEAD_DIM, q, zero)], axis=0)

    def bias_class(tile, kj):
        return min(max(kj - tile, -2), 2) + 2

    def far_factor(hd, which):
        f = jnp.exp2(bias_t_ref[hd, which, 0:1, :])
        return jnp.concatenate([f, f], axis=1)

    def unshifted_weights(hd, tile, pt_ref):
        qs = stacked_q(hd, tile)
        sums = {}
        for kj in range(nk):
            rows = slice(kj * TK, (kj + 1) * TK)
            cls = bias_class(tile, kj)
            st = lax.dot_general(k_ref[rows, head_cols[hd]], qs, (((1,), (1,)), ((), ())),
                                 preferred_element_type=F32)
            if cls in (0, 4):
                pt = jnp.exp2(st)
            else:
                bt = bias_t_ref[hd, cls]
                pt = jnp.exp2(st + jnp.concatenate([bt, bt], axis=1))
                cls = 2
            pt_ref[rows, :] = pt
            part = jnp.sum(pt.reshape(TK // 8, 8, 2 * TQ), axis=0)
            sums[cls] = sums[cls] + part if cls in sums else part
        return sums

    def unshifted_output(hd, tile, sums, pt_ref, at_ref):
        factors = {2: None, 0: far_factor(hd, 0), 4: far_factor(hd, 4)}
        key_sums = sum(s if factors[c] is None else s * factors[c] for c, s in sums.items())
        r = 1.0 / jnp.sum(key_sums, axis=0, keepdims=True)
        r = jnp.concatenate([r[:, :TQ], lam * r[:, TQ:]], axis=1)
        scales = {c: r if f is None else r * f for c, f in factors.items()}
        for kj in range(nk):
            rows = slice(kj * TK, (kj + 1) * TK)
            cls = bias_class(tile, kj)
            scale = scales[cls if cls in (0, 4) else 2]
            back = pl.ds(pl.multiple_of(kj * TK + reload_offset, TK), TK)
            at_ref[rows, :] = (pt_ref[back, :TQ] * scale[:, :TQ] - pt_ref[back, TQ:] * scale[:, TQ:]).astype(BF16)
        ot = jnp.dot(vt_scr[hd], at_ref[...], preferred_element_type=F32)
        ms = jnp.mean(ot * ot, axis=0, keepdims=True)
        gcol = jnp.concatenate([g_col_ref[...]] * (TQ // LANES), axis=1)
        ot = ot * lax.rsqrt(ms + EPS) * gcol * (1.0 - lam_init)
        o_ref[tile * TQ:(tile + 1) * TQ, head_cols[hd]] = ot.T.astype(BF16)

    def unshifted_sequence():
        pt_bufs, at_bufs = (pt_scr, pt_alt_scr), (at_scr, at_alt_scr)
        items = [(hd, tile) for hd in range(HEADS_PER_STEP) for tile in range(nq)]
        sums = unshifted_weights(*items[0], pt_bufs[0])
        for n, item in enumerate(items):
            following = unshifted_weights(*items[n + 1], pt_bufs[(n + 1) % 2]) if n + 1 < len(items) else None
            unshifted_output(*item, sums, pt_bufs[n % 2], at_bufs[n % 2])
            sums = following

    def shifted_head(hd):
        v_ones_scr[:, :LANES] = v_ref[:, head_cols[hd]]
        v_ones_scr[:, LANES:] = jnp.ones((seq, LANES), BF16)

        def shifted_tile(tile, carry):
            qs = stacked_q(hd, tile)
            tiles = [jnp.clip(kj - tile, -2, 2) + 2 for kj in range(nk)]
            for kj in range(nk):
                s_scr[:, kj * TK:(kj + 1) * TK] = lax.dot_general(
                    qs, k_ref[kj * TK:(kj + 1) * TK, head_cols[hd]], (((1,), (1,)), ((), ())),
                    preferred_element_type=F32)

            def strip(i, c):
                for mp in range(2):
                    rows = pl.ds(pl.multiple_of(mp * TQ + i * STRIP, STRIP), STRIP)
                    brows = pl.ds(pl.multiple_of(i * STRIP, STRIP), STRIP)
                    t = jnp.concatenate([s_scr[rows, kj * TK:(kj + 1) * TK] + bias_ref[hd, tiles[kj], brows, :]
                                         for kj in range(nk)], axis=1)
                    p_scr[rows, :] = jnp.exp2(t - jnp.max(t, axis=-1, keepdims=True)).astype(BF16)
                return c

            lax.fori_loop(0, TQ // STRIP, strip, 0)
            res = jnp.dot(p_scr[...], v_ones_scr[...], preferred_element_type=F32)
            out = res[:, :LANES] * (1.0 / res[:, LANES:])
            o = out[:TQ] - lam * out[TQ:]
            ms = jnp.mean(o * o, axis=-1, keepdims=True)
            o = o * lax.rsqrt(ms + EPS) * g_ref[...] * (1.0 - lam_init)
            o_ref[pl.ds(pl.multiple_of(tile * TQ, TQ), TQ), head_cols[hd]] = o.astype(BF16)
            return carry

        lax.fori_loop(0, nq, shifted_tile, 0)

    bounded = bounded_ref[0] != 0

    @pl.when(bounded)
    def _():
        for hd in range(HEADS_PER_STEP):
            vt_scr[hd] = v_ref[:, head_cols[hd]].astype(F32).T.astype(BF16)
        unshifted_sequence()

    @pl.when(jnp.logical_not(bounded))
    def _():
        for hd in range(HEADS_PER_STEP):
            shifted_head(hd)


def _diff_attn(dq, dk, dv, bounded, lamv, bias5, bias5_t, subln_g, lam_init, batch, seq):
    seq_block = pl.BlockSpec((seq, HEADS_PER_STEP * LANES), lambda b, h: (b, h))
    bias_block = pl.BlockSpec((HEADS_PER_STEP, 5, TQ, TK), lambda b, h: (h, 0, 0, 0))
    g_col = jnp.broadcast_to(subln_g.reshape(LANES, 1), (LANES, LANES))
    return pl.pallas_call(
        functools.partial(_diff_attn_kernel, lam_init),
        grid=(batch, DIFF_HEADS // HEADS_PER_STEP),
        in_specs=[
            pl.BlockSpec(memory_space=pltpu.SMEM),
            pl.BlockSpec((8, LANES), lambda b, h: (0, 0)),
            seq_block, seq_block, seq_block,
            bias_block, bias_block,
            pl.BlockSpec((1, LANES), lambda b, h: (0, 0)),
            pl.BlockSpec((LANES, LANES), lambda b, h: (0, 0)),
        ],
        out_specs=seq_block,
        out_shape=jax.ShapeDtypeStruct((batch * seq, DIFF_W), BF16),
        scratch_shapes=[pltpu.VMEM((2 * TQ, seq), F32), pltpu.VMEM((2 * TQ, seq), BF16),
                        pltpu.VMEM((seq, 2 * LANES), BF16), pltpu.VMEM((HEADS_PER_STEP, LANES, seq), BF16),
                        pltpu.VMEM((seq, 2 * TQ), F32), pltpu.VMEM((seq, 2 * TQ), F32),
                        pltpu.VMEM((seq, TQ), BF16), pltpu.VMEM((seq, TQ), BF16)],
        compiler_params=_compiler_params(("parallel", "parallel")),
        name="diff_attn",
    )(bounded, lamv, dq, dk, dv, bias5, bias5_t, subln_g, g_col)


WIN_KEYS = 3 * BLOCK
BLOCKS_PER_ITER = 8


def _win_attn_kernel(bounded_ref, sink_ref, q_ref, k_ref, v_ref, bias_ref, o_ref, s_scr, p_scr, v_ones_scr):
    seq = k_ref.shape[0]
    nb = seq // BLOCK
    group_rows = WIN_GROUP * BLOCK
    lane = lax.broadcasted_iota(jnp.int32, (BLOCK, LANES), 1)
    v_ones_scr[:, :LANES] = v_ref[...]
    v_ones_scr[:, LANES:] = jnp.ones(v_ref.shape, BF16)

    def window(n):
        start = pl.multiple_of(jnp.clip(n * BLOCK - BLOCK, 0, seq - WIN_KEYS), BLOCK)
        variant = jnp.where(n == 0, 0, jnp.where(n == nb - 1, 2, 1))
        return start, variant

    def grouped_q(n, kv):
        q = q_ref[pl.ds(pl.multiple_of(n * BLOCK, BLOCK), BLOCK), :]
        zero = jnp.zeros((BLOCK, LANES), q.dtype)
        keep = (lane < HEAD_DIM) if kv == 0 else (lane >= HEAD_DIM)
        return jnp.concatenate([jnp.where(keep, q[:, j * LANES:(j + 1) * LANES], zero)
                                for j in range(WIN_GROUP)], axis=0)

    def group_logits(n, kv, start):
        return lax.dot_general(grouped_q(n, kv), k_ref[pl.ds(start, WIN_KEYS), :], (((1,), (1,)), ((), ())),
                               preferred_element_type=F32)

    def store(n, outs):
        rows = pl.ds(pl.multiple_of(n * BLOCK, BLOCK), BLOCK)
        for j in range(WIN_GROUP):
            blk = jnp.where(lane < HEAD_DIM, outs[0][j * BLOCK:(j + 1) * BLOCK], outs[1][j * BLOCK:(j + 1) * BLOCK])
            o_ref[rows, j * LANES:(j + 1) * LANES] = blk.astype(BF16)

    def unshifted_weights(n, kv):
        start, variant = window(n)
        grows = slice(kv * group_rows, (kv + 1) * group_rows)
        return jnp.exp2(group_logits(n, kv, start) + bias_ref[variant, grows, :]).astype(BF16)

    def unshifted_output(n, kv, p):
        start, _ = window(n)
        res = jnp.dot(p, v_ones_scr[pl.ds(start, WIN_KEYS), :], preferred_element_type=F32)
        sink_w = jnp.concatenate(
            [jnp.broadcast_to(jnp.exp2(jnp.full((1, LANES), sink_ref[kv * WIN_GROUP + g], F32)), (BLOCK, LANES))
             for g in range(WIN_GROUP)], axis=0)
        return res[:, :LANES] * (1.0 / (res[:, LANES:] + sink_w))

    def unshifted_blocks(i, carry):
        items = [(i * BLOCKS_PER_ITER + u, kv) for u in range(BLOCKS_PER_ITER) for kv in range(WIN_KV_HEADS)]
        p, outs = unshifted_weights(*items[0]), []
        for m, (n, kv) in enumerate(items):
            following = unshifted_weights(*items[m + 1]) if m + 1 < len(items) else None
            outs.append(unshifted_output(n, kv, p))
            if kv == WIN_KV_HEADS - 1:
                store(n, outs)
                outs = []
            p = following
        return carry

    def shifted_block(n, carry):
        start, variant = window(n)
        outs = []
        for kv in range(WIN_KV_HEADS):
            s_scr[...] = group_logits(n, kv, start)
            sink_terms = []
            for r in range(0, group_rows, STRIP):
                rows = slice(r, r + STRIP)
                grows = slice(kv * group_rows + r, kv * group_rows + r + STRIP)
                t = s_scr[rows, :] + bias_ref[variant, grows, :]
                sink = sink_ref[(kv * group_rows + r) // BLOCK]
                row_max = jnp.max(t, axis=-1, keepdims=True)
                p_scr[rows, :] = jnp.exp2(t - jnp.maximum(row_max, sink)).astype(BF16)
                m_wide = jnp.maximum(jnp.broadcast_to(row_max, (STRIP, LANES)), sink)
                sink_terms.append(jnp.exp2(sink - m_wide))
            res = jnp.dot(p_scr[...], v_ones_scr[pl.ds(start, WIN_KEYS), :], preferred_element_type=F32)
            l = res[:, LANES:] + jnp.concatenate(sink_terms, axis=0)
            outs.append(res[:, :LANES] * (1.0 / l))
        store(n, outs)
        return carry

    bounded = bounded_ref[0] != 0

    @pl.when(bounded)
    def _():
        lax.fori_loop(0, nb // BLOCKS_PER_ITER, unshifted_blocks, 0)

    @pl.when(jnp.logical_not(bounded))
    def _():
        lax.fori_loop(0, nb, shifted_block, 0)


def _win_attn(wq, wk, wv, bounded, bias3, sink, batch, seq):
    rows = WIN_HEADS * BLOCK
    smem = pl.BlockSpec(memory_space=pltpu.SMEM)
    seq_block = lambda w: pl.BlockSpec((seq, w), lambda b: (b, 0))
    return pl.pallas_call(
        _win_attn_kernel,
        grid=(batch,),
        in_specs=[smem, smem, seq_block(WIN_Q_W), seq_block(WIN_KV_W), seq_block(WIN_KV_W),
                  _resident((3, rows, WIN_KEYS))],
        out_specs=seq_block(WIN_Q_W),
        out_shape=jax.ShapeDtypeStruct((batch * seq, WIN_Q_W), BF16),
        scratch_shapes=[pltpu.VMEM((rows // 2, WIN_KEYS), F32), pltpu.VMEM((rows // 2, WIN_KEYS), BF16),
                        pltpu.VMEM((seq, 2 * LANES), BF16)],
        compiler_params=_compiler_params(("parallel",)),
        name="win_attn",
    )(bounded, sink, wq, wk, wv, bias3)


def _mix_ffn_kernel(x_ref, xp_ref, xn_ref, oa_ref, oap_ref, oan_ref, ob_ref, obp_ref, obn_ref,
                    wa_ref, wb_ref, g_ref, wg_ref, wu_ref, wd_ref, cw_ref, o_ref, act_scr):
    i = pl.program_id(1)
    last = pl.num_programs(1) - 1
    ext_rows = TM_FFN + 2 * HALO
    tile = slice(HALO, HALO + TM_FFN)
    ext = lambda p, c, n: jnp.concatenate([p[...], c[...], n[...]], axis=0)
    x1 = (ext(xp_ref, x_ref, xn_ref)
          + jnp.dot(ext(oap_ref, oa_ref, oan_ref), wa_ref[...], preferred_element_type=F32)
          + jnp.dot(ext(obp_ref, ob_ref, obn_ref), wb_ref[...], preferred_element_type=F32))
    o_ref[...] = x1[tile]
    ms = jnp.mean(x1 * x1, axis=-1, keepdims=True)
    h_ext = (x1 * lax.rsqrt(ms + EPS) * g_ref[...]).astype(BF16)
    row = lax.broadcasted_iota(jnp.int32, (ext_rows, 1), 0)
    outside = ((row < HALO) & (i == 0)) | ((row >= HALO + TM_FFN) & (i == last))
    h_ext = jnp.where(outside, jnp.zeros_like(h_ext), h_ext)
    h = h_ext[tile]
    for c0 in range(0, D_FF, FF_CHUNK):
        cols = slice(c0, min(c0 + FF_CHUNK, D_FF))
        g_ext = jnp.dot(h_ext, wg_ref[:, cols], preferred_element_type=F32)
        g = g_ext[tile]
        g_dn = pltpu.roll(g_ext, 1, 0)[tile]
        g_up = pltpu.roll(g_ext, ext_rows - 1, 0)[tile]
        cw = cw_ref[:, cols]
        u = g_dn * cw[0:1] + g * cw[1:2] + g_up * cw[2:3] + cw[3:4]
        up = jnp.dot(h, wu_ref[:, cols], preferred_element_type=F32)
        act = (u * (1.0 / (1.0 + jnp.exp(-u)))) * up
        act_scr[:, cols] = act.astype(BF16)
    o_ref[...] = o_ref[...] + jnp.dot(act_scr[...], wd_ref[...], preferred_element_type=F32)


def _mix_ffn(x2d, oa, ob, wa, wb, g_ffn, wg, wu, wd, cw, batch, seq):
    nt = seq // TM_FFN
    hb = TM_FFN // HALO
    n_halo = batch * seq // HALO

    def with_halo(width):
        return [pl.BlockSpec((TM_FFN, width), lambda b, i: (b * nt + i, 0)),
                pl.BlockSpec((HALO, width), lambda b, i: (jnp.maximum((b * nt + i) * hb - 1, 0), 0)),
                pl.BlockSpec((HALO, width), lambda b, i: (jnp.minimum((b * nt + i + 1) * hb, n_halo - 1), 0))]

    return pl.pallas_call(
        _mix_ffn_kernel,
        grid=(batch, nt),
        in_specs=(with_halo(D_MODEL) + with_halo(DIFF_W) + with_halo(WIN_Q_W)
                  + [_resident((DIFF_W, D_MODEL)), _resident((WIN_Q_W, D_MODEL)), _resident((1, D_MODEL)),
                     _resident((D_MODEL, D_FF)), _resident((D_MODEL, D_FF)), _resident((D_FF, D_MODEL)),
                     _resident((4, D_FF))]),
        out_specs=pl.BlockSpec((TM_FFN, D_MODEL), lambda b, i: (b * nt + i, 0)),
        out_shape=jax.ShapeDtypeStruct((batch * seq, D_MODEL), F32),
        scratch_shapes=[pltpu.VMEM((TM_FFN, D_FF), BF16)],
        compiler_params=_compiler_params(("parallel", "arbitrary")),
        name="mix_ffn",
    )(x2d, x2d, x2d, oa, oa, oa, ob, ob, ob, wa, wb, g_ffn, wg, wu, wd, cw)


SKEW_W = 1024


def _bias_tables_kernel(far_ref, fwd_ref, bwd_ref, win_vec_ref, bias_ref, bias_t_ref, win_ref):
    def skew(row, rows):
        return pltpu.roll(jnp.broadcast_to(row, (rows, SKEW_W)), 0, 1, stride=1, stride_axis=0)

    for h in range(DIFF_HEADS):
        near = skew(fwd_ref[h:h + 1, :], TQ)
        near_t = skew(bwd_ref[h:h + 1, :], 3 * TK)
        for d in range(3):
            bias_ref[h, d + 1] = near[:, d * TK:(d + 1) * TK]
            bias_t_ref[h, d + 1] = near_t[d * TK:(d + 1) * TK, :TQ]
        for d in (0, 4):
            bias_ref[h, d] = jnp.full((TQ, TK), far_ref[h, d // 4], F32)
            bias_t_ref[h, d] = jnp.full((TK, TQ), far_ref[h, d // 4], F32)
    for h in range(WIN_HEADS):
        band = skew(win_vec_ref[h:h + 1, :], BLOCK)
        for v in range(3):
            win_ref[v, h * BLOCK:(h + 1) * BLOCK, :] = band[:, (2 - v) * BLOCK:(2 - v) * BLOCK + WIN_KEYS]


def _wrapped_offsets(last):
    idx = jnp.arange(SKEW_W)
    return jnp.where(idx <= last, idx, idx - SKEW_W)


def _bias_ranges(rel_bias):
    table = rel_bias.astype(F32).T * LOG2E
    return dict(table=table, hi=jnp.max(table, axis=1), lo=jnp.min(table, axis=1))


def _bias_tables(ranges, win_centre):
    centre = jnp.concatenate([0.5 * (ranges["hi"] + ranges["lo"])[:DIFF_HEADS], win_centre])
    centred = ranges["table"] - centre[:, None]
    rel_fwd = _wrapped_offsets(3 * TK - 1) - TK
    rel_bwd = -_wrapped_offsets(TQ - 1) - TK
    rel_win = _wrapped_offsets(5 * BLOCK - 1) - 2 * BLOCK
    rel_far = jnp.array([-2 * TK, 2 * TK])
    looked_up = centred[:, _rel_bucket(jnp.concatenate([rel_fwd, rel_bwd, rel_win, rel_far]))]
    fwd, bwd = looked_up[:DIFF_HEADS, :SKEW_W], looked_up[:DIFF_HEADS, SKEW_W:2 * SKEW_W]
    far = looked_up[:DIFF_HEADS, 3 * SKEW_W:]
    win_vec = jnp.where(jnp.abs(rel_win) <= WINDOW, looked_up[DIFF_HEADS:, 2 * SKEW_W:3 * SKEW_W], NEG_INF)
    whole = lambda shape: pl.BlockSpec(shape, lambda: (0,) * len(shape))
    tile_shape = (DIFF_HEADS, 5, TQ, TK)
    win_shape = (3, WIN_HEADS * BLOCK, WIN_KEYS)
    diff, diff_t, win = pl.pallas_call(
        _bias_tables_kernel,
        in_specs=[pl.BlockSpec(memory_space=pltpu.SMEM), whole((DIFF_HEADS, SKEW_W)), whole((DIFF_HEADS, SKEW_W)),
                  whole((WIN_HEADS, SKEW_W))],
        out_specs=[whole(tile_shape), whole(tile_shape), whole(win_shape)],
        out_shape=[jax.ShapeDtypeStruct(tile_shape, F32), jax.ShapeDtypeStruct(tile_shape, F32),
                   jax.ShapeDtypeStruct(win_shape, F32)],
        compiler_params=pltpu.CompilerParams(vmem_limit_bytes=VMEM_LIMIT),
        name="bias_tables",
    )(far, fwd, bwd, win_vec)
    return dict(bias5=diff, bias5_t=diff_t, win_bias=win)


def _qk_bound(q_gain, k_gain):
    return HEAD_DIM * NORM_SLACK * jnp.max(jnp.abs(q_gain)) * jnp.max(jnp.abs(k_gain))


def _is_bounded(half_range):
    return (half_range <= MAX_UNSHIFTED_LOGIT).astype(jnp.int32).reshape(1)


def _layer_params(l, ranges, norm_attn_g, w_in, diff_q_norm_g, diff_k_norm_g, diff_lambda_q1, diff_lambda_k1,
                  diff_lambda_q2, diff_lambda_k2, diff_subln_g, win_q_norm_g, win_k_norm_g, win_sink,
                  w_out, norm_ffn_g, w_gate, w_up, conv_w, conv_b, w_down):
    scale = HEAD_DIM ** -0.5
    w = w_in[l].astype(BF16)
    dq, dk, dv = w[:, :DIFF_W], w[:, DIFF_W:2 * DIFF_W], w[:, 2 * DIFF_W:3 * DIFF_W]
    wq = w[:, 3 * DIFF_W:3 * DIFF_W + WIN_Q_W].reshape(D_MODEL, WIN_KV_HEADS, WIN_GROUP, HEAD_DIM)
    wq = wq.transpose(0, 2, 1, 3).reshape(D_MODEL, WIN_Q_W)
    wk = w[:, 3 * DIFF_W + WIN_Q_W:3 * DIFF_W + WIN_Q_W + WIN_KV_W]
    wv = w[:, 3 * DIFF_W + WIN_Q_W + WIN_KV_W:]
    w_in_p = jnp.concatenate([dq, dk, wq, wk, dv, wv], axis=1)
    gvec = jnp.concatenate([jnp.tile(diff_q_norm_g[l], 2 * DIFF_HEADS) * (scale * LOG2E),
                            jnp.tile(diff_k_norm_g[l], 2 * DIFF_HEADS),
                            jnp.tile(win_q_norm_g[l], WIN_HEADS) * (scale * LOG2E),
                            jnp.tile(win_k_norm_g[l], WIN_KV_HEADS)]).reshape(1, NORMED_W).astype(F32)
    lamv = jnp.zeros((8, LANES), F32)
    lamv = lamv.at[0:4, :HEAD_DIM].set(jnp.stack([diff_lambda_q1[l], diff_lambda_k1[l],
                                                  diff_lambda_q2[l], diff_lambda_k2[l]]).astype(F32))
    wo = w_out[l]
    wb = wo[DIFF_W:].reshape(WIN_KV_HEADS, WIN_GROUP, HEAD_DIM, D_MODEL)
    wb = wb.transpose(1, 0, 2, 3).reshape(WIN_Q_W, D_MODEL)
    cw = jnp.concatenate([conv_w[l], conv_b[l][None]], axis=0).astype(F32)
    return dict(
        g_attn=norm_attn_g[l].reshape(1, D_MODEL).astype(F32), w_in_p=w_in_p, gvec=gvec, lamv=lamv,
        subln_g=diff_subln_g[l].reshape(1, 2 * HEAD_DIM).astype(F32),
        wa=wo[:DIFF_W].astype(BF16), wb=wb.astype(BF16),
        g_ffn=norm_ffn_g[l].reshape(1, D_MODEL).astype(F32),
        wg=w_gate[l].astype(BF16), wu=w_up[l].astype(BF16), wd=w_down[l].astype(BF16), cw=cw,
        lam_init=0.8 - 0.6 * math.exp(-0.3 * l),
        diff_bounded=_is_bounded(_qk_bound(diff_q_norm_g[l].astype(F32) * (scale * LOG2E), diff_k_norm_g[l])
                                 + jnp.max(0.5 * (ranges["hi"] - ranges["lo"])[:DIFF_HEADS])),
        **_window_softmax_params(ranges, win_q_norm_g[l].astype(F32) * (scale * LOG2E), win_k_norm_g[l],
                                 win_sink[l].astype(F32) * LOG2E),
    )


def _window_softmax_params(ranges, q_gain, k_gain, sink):
    qk = _qk_bound(q_gain, k_gain)
    hi = jnp.maximum(ranges["hi"][DIFF_HEADS:] + qk, sink)
    lo = jnp.minimum(ranges["lo"][DIFF_HEADS:] - qk, sink)
    centre = 0.5 * (hi + lo)
    return dict(win_sink=sink - centre, win_bounded=_is_bounded(jnp.max(0.5 * (hi - lo))),
                **_bias_tables(ranges, centre))


def _encoder_layer(x, p):
    batch, seq, _ = x.shape
    x2d = x.reshape(batch * seq, D_MODEL)
    dq, dk, wq, wk, dv, wv = _in_proj(x2d, p["g_attn"], p["w_in_p"], p["gvec"])
    oa = _diff_attn(dq, dk, dv, p["diff_bounded"], p["lamv"], p["bias5"], p["bias5_t"], p["subln_g"], p["lam_init"],
                    batch, seq)
    ob = _win_attn(wq, wk, wv, p["win_bounded"], p["win_bias"], p["win_sink"], batch, seq)
    y = _mix_ffn(x2d, oa, ob, p["wa"], p["wb"], p["g_ffn"], p["wg"], p["wu"], p["wd"], p["cw"], batch, seq)
    return y.reshape(batch, seq, D_MODEL)


def kernel(x_prompt, x_sample, norm_attn_g, w_in, diff_q_norm_g, diff_k_norm_g, diff_lambda_q1, diff_lambda_k1, diff_lambda_q2, diff_lambda_k2, diff_subln_g, win_q_norm_g, win_k_norm_g, win_sink, rel_bias, w_out, norm_ffn_g, w_gate, w_up, conv_w, conv_b, w_down):
    depth = w_in.shape[0]
    ranges = _bias_ranges(rel_bias)
    layers = [_layer_params(l, ranges, norm_attn_g, w_in, diff_q_norm_g, diff_k_norm_g, diff_lambda_q1,
                            diff_lambda_k1, diff_lambda_q2, diff_lambda_k2, diff_subln_g, win_q_norm_g,
                            win_k_norm_g, win_sink, w_out, norm_ffn_g, w_gate, w_up, conv_w, conv_b, w_down)
              for l in range(depth)]

    def run(x):
        for p in layers:
            x = _encoder_layer(x, p)
        return x

    return run(x_prompt), run(x_sample)
```
